```python
import jax, jax.numpy as jnp
from jax import lax
import numpy as np

D_MODEL = 4096
BATCH = 8
SEQ = 2048
DEPTH = 2

HEAD_DIM = 128
N_HEADS_TOTAL = D_MODEL // HEAD_DIM
NSA_HEADS = N_HEADS_TOTAL // 2
NSA_KV_HEADS = NSA_HEADS // 4
NSA_GROUP = NSA_HEADS // NSA_KV_HEADS
SB_HEADS = N_HEADS_TOTAL - NSA_HEADS
CMP_BLOCK = 32
CMP_STRIDE = 16
SEL_BLOCK = 64
SEL_TOP_N = 16
WINDOW = 512
Q_BLOCK = 128
SEL_Q_BLOCK = 32
ROPE_THETA = 10000.0
POOL_GROUPS = 4
POOL_WINDOWS = (2, 4, 8, 16)
POOL_GROUP_DIM = D_MODEL // POOL_GROUPS
D_FF = 11008
CONV_WIDTH = 3
LN_EPS = 1e-5
NEG_INF = -1e30
FORCE_SCORE = 1e9
DEEPNORM_ALPHA = (2 * DEPTH) ** 0.25
DEEPNORM_BETA = (8 * DEPTH) ** -0.25
N_ATTN_LAYERS = (DEPTH + 1) // 2
N_POOL_LAYERS = DEPTH // 2

Q_NSA_DIM = NSA_HEADS * HEAD_DIM
KV_NSA_DIM = NSA_KV_HEADS * HEAD_DIM
GATE_DIM = 3 * NSA_HEADS
SB_DIM = SB_HEADS * HEAD_DIM
IN_SEGMENTS = (Q_NSA_DIM, KV_NSA_DIM, KV_NSA_DIM, KV_NSA_DIM, KV_NSA_DIM, KV_NSA_DIM, KV_NSA_DIM,
               GATE_DIM, SB_DIM, SB_DIM, SB_DIM)
IN_VALUE_SEGMENT = (False, False, True, False, True, False, True, False, False, False, True)
IN_DIM = sum(IN_SEGMENTS)

kernel_name = "nsa_stickbreak_pool_convffn_deepnorm"


def layer_norm(x, g, b):
    xf = x.astype(jnp.float32)
    mu = xf.mean(-1, keepdims=True)
    var = jnp.square(xf - mu).mean(-1, keepdims=True)
    return ((xf - mu) * lax.rsqrt(var + LN_EPS) * g + b).astype(x.dtype)


def rope_cos_sin(pos):
    inv_freq = 1.0 / (ROPE_THETA ** (jnp.arange(0, HEAD_DIM, 2, dtype=jnp.float32) / HEAD_DIM))
    ang = pos.astype(jnp.float32)[:, None] * inv_freq[None, :]
    return jnp.cos(ang), jnp.sin(ang)


def apply_rope(x, cos, sin):
    xf = x.astype(jnp.float32)
    x1, x2 = xf[..., :HEAD_DIM // 2], xf[..., HEAD_DIM // 2:]
    c, s = cos[None, :, None, :], sin[None, :, None, :]
    return jnp.concatenate([x1 * c - x2 * s, x1 * s + x2 * c], axis=-1).astype(x.dtype)


def nsa_attention(q, k_cmp, v_cmp, k_slc, v_slc, k_win, v_win, gate_logits,
                  w_cmp_k, w_cmp_v, pe_cmp_k, pe_cmp_v):
    B, T = q.shape[:2]
    scale = HEAD_DIM ** -0.5
    qg = q.reshape(B, T, NSA_KV_HEADS, NSA_GROUP, HEAD_DIM)
    t_pos = jnp.arange(T)

    n_cmp = (T - CMP_BLOCK) // CMP_STRIDE + 1
    blk_idx = np.arange(n_cmp)[:, None] * CMP_STRIDE + np.arange(CMP_BLOCK)[None, :]
    kc_blocks = k_cmp[:, blk_idx] + pe_cmp_k[None, None, :, None, :]
    vc_blocks = v_cmp[:, blk_idx] + pe_cmp_v[None, None, :, None, :]
    kc = jnp.einsum('bnlhd,lde->bnhe', kc_blocks, w_cmp_k)
    vc = jnp.einsum('bnlhd,lde->bnhe', vc_blocks, w_cmp_v)
    cmp_end = np.arange(n_cmp) * CMP_STRIDE + CMP_BLOCK - 1
    c_cos, c_sin = rope_cos_sin(jnp.asarray(cmp_end))
    kc = apply_rope(kc, c_cos, c_sin)
    valid_cmp = jnp.asarray(cmp_end)[None, :] <= t_pos[:, None]
    s_cmp = jnp.einsum('btkgd,bnkd->bkgtn', qg, kc).astype(jnp.float32) * scale
    s_cmp = jnp.where(valid_cmp, s_cmp, NEG_INF)
    p_cmp = jax.nn.softmax(s_cmp, axis=-1) * valid_cmp.any(-1, keepdims=True)
    o_cmp = jnp.einsum('bkgtn,bnkd->btkgd', p_cmp.astype(vc.dtype), vc)

    n_blk = T // SEL_BLOCK
    c_start = np.arange(n_cmp)[:, None] * CMP_STRIDE
    b_start = np.arange(n_blk)[None, :] * SEL_BLOCK
    overlap = np.clip(np.minimum(c_start + CMP_BLOCK, b_start + SEL_BLOCK) - np.maximum(c_start, b_start), 0, None)
    overlap = jnp.asarray(overlap / CMP_BLOCK, jnp.float32)
    imp = jnp.einsum('bkgtn,nj->bktj', p_cmp, overlap)
    blk = jnp.arange(n_blk)[None, :]
    cur = (t_pos // SEL_BLOCK)[:, None]
    forced = (blk == 0) | (blk == cur) | (blk == cur - 1)
    valid_blk = blk * SEL_BLOCK <= t_pos[:, None]
    imp = jnp.where(forced, FORCE_SCORE, jnp.where(valid_blk, imp, NEG_INF))
    n_top = min(SEL_TOP_N, n_blk)
    _, sel_idx = lax.top_k(imp, n_top)

    ks_blocks = k_slc.reshape(B, n_blk, SEL_BLOCK, NSA_KV_HEADS, HEAD_DIM).transpose(0, 3, 1, 2, 4)
    vs_blocks = v_slc.reshape(B, n_blk, SEL_BLOCK, NSA_KV_HEADS, HEAD_DIM).transpose(0, 3, 1, 2, 4)
    n_sq = T // SEL_Q_BLOCK
    q_sel = qg.reshape(B, n_sq, SEL_Q_BLOCK, NSA_KV_HEADS, NSA_GROUP, HEAD_DIM).transpose(1, 0, 2, 3, 4, 5)
    idx_sel = sel_idx.reshape(B, NSA_KV_HEADS, n_sq, SEL_Q_BLOCK, n_top).transpose(2, 0, 1, 3, 4)
    gather = jax.vmap(jax.vmap(lambda blocks, ids: blocks[ids]))

    def sel_chunk(args):
        qc, ic, start = args
        kg = gather(ks_blocks, ic)
        vg = gather(vs_blocks, ic)
        s = jnp.einsum('bqkgd,bkqnld->bkgqnl', qc, kg).astype(jnp.float32) * scale
        kpos = ic[..., None] * SEL_BLOCK + jnp.arange(SEL_BLOCK)
        tq = start + jnp.arange(SEL_Q_BLOCK)
        mask = kpos <= tq[None, None, :, None, None]
        s = jnp.where(mask[:, :, None], s, NEG_INF)
        p = jax.nn.softmax(s.reshape(s.shape[:4] + (-1,)), axis=-1).reshape(s.shape)
        return jnp.einsum('bkgqnl,bkqnld->bqkgd', p.astype(vg.dtype), vg)

    o_slc = lax.map(sel_chunk, (q_sel, idx_sel, jnp.arange(n_sq) * SEL_Q_BLOCK))
    o_slc = o_slc.transpose(1, 0, 2, 3, 4, 5).reshape(B, T, NSA_KV_HEADS, NSA_GROUP, HEAD_DIM)

    kw_pad = jnp.pad(k_win, ((0, 0), (WINDOW, 0), (0, 0), (0, 0)))
    vw_pad = jnp.pad(v_win, ((0, 0), (WINDOW, 0), (0, 0), (0, 0)))
    n_qb = T // Q_BLOCK
    q_win = qg.reshape(B, n_qb, Q_BLOCK, NSA_KV_HEADS, NSA_GROUP, HEAD_DIM).transpose(1, 0, 2, 3, 4, 5)

    def win_chunk(args):
        qc, start = args
        kb = lax.dynamic_slice_in_dim(kw_pad, start, WINDOW + Q_BLOCK, axis=1)
        vb = lax.dynamic_slice_in_dim(vw_pad, start, WINDOW + Q_BLOCK, axis=1)
        s = jnp.einsum('bqkgd,bskd->bkgqs', qc, kb).astype(jnp.float32) * scale
        tq = start + jnp.arange(Q_BLOCK)
        kpos = start - WINDOW + jnp.arange(WINDOW + Q_BLOCK)
        diff = tq[:, None] - kpos[None, :]
        mask = (diff >= 0) & (diff < WINDOW) & (kpos >= 0)[None, :]
        p = jax.nn.softmax(jnp.where(mask, s, NEG_INF), axis=-1)
        return jnp.einsum('bkgqs,bskd->bqkgd', p.astype(vb.dtype), vb)

    o_win = lax.map(win_chunk, (q_win, jnp.arange(n_qb) * Q_BLOCK))
    o_win = o_win.transpose(1, 0, 2, 3, 4, 5).reshape(B, T, NSA_KV_HEADS, NSA_GROUP, HEAD_DIM)

    g = jax.nn.sigmoid(gate_logits.astype(jnp.float32)).astype(q.dtype)
    g = g.reshape(B, T, 3, NSA_KV_HEADS, NSA_GROUP, 1)
    o = g[:, :, 0] * o_cmp + g[:, :, 1] * o_slc + g[:, :, 2] * o_win
    return o.reshape(B, T, NSA_HEADS * HEAD_DIM)


def stick_breaking_attention(q, k, v):
    B, T, H, _ = q.shape
    scale = HEAD_DIM ** -0.5
    n_qb = T // Q_BLOCK
    q_blocks = q.reshape(B, n_qb, Q_BLOCK, H, HEAD_DIM).transpose(1, 0, 2, 3, 4)
    kpos = jnp.arange(T)

    def chunk(args):
        qc, start = args
        z = jnp.einsum('bqhd,bshd->bhqs', qc, k).astype(jnp.float32) * scale
        tq = start + jnp.arange(Q_BLOCK)
        mask = kpos[None, :] < tq[:, None]
        neg_log_1m_beta = jnp.where(mask, jax.nn.softplus(z), 0.0)
        later = lax.cumsum(neg_log_1m_beta, axis=3, reverse=True) - neg_log_1m_beta
        a = jnp.where(mask, jnp.exp(jax.nn.log_sigmoid(z) - later), 0.0)
        return jnp.einsum('bhqs,bshd->bqhd', a.astype(v.dtype), v)

    o = lax.map(chunk, (q_blocks, jnp.arange(n_qb) * Q_BLOCK))
    return o.transpose(1, 0, 2, 3, 4).reshape(B, T, H * HEAD_DIM)


def attention_mixer(x, w_in, w_out, w_cmp_k, w_cmp_v, pe_cmp_k, pe_cmp_v):
    B, T, _ = x.shape
    h = jnp.einsum('btd,de->bte', x, w_in)
    cuts = [int(c) for c in np.cumsum(IN_SEGMENTS)[:-1]]
    q_a, kc, vc, ks, vs, kw, vw, gates, q_b, k_b, v_b = jnp.split(h, cuts, axis=-1)
    heads = lambda t, n: t.reshape(B, T, n, HEAD_DIM)
    cos, sin = rope_cos_sin(jnp.arange(T))
    q_a = apply_rope(heads(q_a, NSA_HEADS), cos, sin)
    ks = apply_rope(heads(ks, NSA_KV_HEADS), cos, sin)
    kw = apply_rope(heads(kw, NSA_KV_HEADS), cos, sin)
    o_a = nsa_attention(q_a, heads(kc, NSA_KV_HEADS), heads(vc, NSA_KV_HEADS), ks, heads(vs, NSA_KV_HEADS),
                        kw, heads(vw, NSA_KV_HEADS), gates, w_cmp_k, w_cmp_v, pe_cmp_k, pe_cmp_v)
    o_b = stick_breaking_attention(heads(q_b, SB_HEADS), heads(k_b, SB_HEADS), heads(v_b, SB_HEADS))
    return jnp.einsum('bte,ed->btd', jnp.concatenate([o_a, o_b], axis=-1), w_out)


def pool_mixer(x, w_pool, pool_scale):
    B, T, D = x.shape
    cs = jnp.cumsum(x.astype(jnp.float32), axis=1)
    t = jnp.arange(T)
    diffs = []
    for gi, w in enumerate(POOL_WINDOWS):
        sl = slice(gi * POOL_GROUP_DIM, (gi + 1) * POOL_GROUP_DIM)
        csg = cs[..., sl]
        prev = jnp.pad(csg, ((0, 0), (w, 0), (0, 0)))[:, :T]
        count = jnp.minimum(t + 1, w).astype(jnp.float32)[None, :, None]
        diffs.append((csg - prev) / count - x[..., sl].astype(jnp.float32))
    d = jnp.stack(diffs, axis=2).astype(x.dtype)
    y = jnp.einsum('btgc,gce->btge', d, w_pool).reshape(B, T, D)
    return y * pool_scale


def conv_ffn(x, w_up, conv_w, conv_b, w_down):
    T = x.shape[1]
    h = jnp.einsum('btd,df->btf', x, w_up)
    hp = jnp.pad(h, ((0, 0), (CONV_WIDTH - 1, 0), (0, 0)))
    hc = conv_b
    for k in range(CONV_WIDTH):
        hc = hc + hp[:, k:k + T] * conv_w[k]
    gate, val = jnp.split(hc, 2, axis=-1)
    return jnp.einsum('btf,fd->btd', jax.nn.silu(gate) * val, w_down)


def setup_inputs(seed: int = 0) -> dict:
    key = jax.random.key(seed)
    ks = jax.random.split(key, 18)
    nrm = lambda k, shape, s: jax.random.normal(k, shape, jnp.float32) * s
    col_scale = np.concatenate([np.full(n, DEEPNORM_BETA if v else 1.0, np.float32)
                                for n, v in zip(IN_SEGMENTS, IN_VALUE_SEGMENT)])
    return {
        "x": nrm(ks[0], (BATCH, SEQ, D_MODEL), 1.0),
        "attn_w_in": nrm(ks[1], (N_ATTN_LAYERS, D_MODEL, IN_DIM), D_MODEL ** -0.5) * jnp.asarray(col_scale),
        "attn_w_out": nrm(ks[2], (N_ATTN_LAYERS, D_MODEL, D_MODEL), D_MODEL ** -0.5 * DEEPNORM_BETA),
        "cmp_w_k": nrm(ks[3], (N_ATTN_LAYERS, CMP_BLOCK, HEAD_DIM, HEAD_DIM), (CMP_BLOCK * HEAD_DIM) ** -0.5),
        "cmp_w_v": nrm(ks[4], (N_ATTN_LAYERS, CMP_BLOCK, HEAD_DIM, HEAD_DIM), (CMP_BLOCK * HEAD_DIM) ** -0.5),
        "cmp_pe_k": nrm(ks[5], (N_ATTN_LAYERS, CMP_BLOCK, HEAD_DIM), 0.1),
        "cmp_pe_v": nrm(ks[6], (N_ATTN_LAYERS, CMP_BLOCK, HEAD_DIM), 0.1),
        "pool_w": nrm(ks[7], (N_POOL_LAYERS, POOL_GROUPS, POOL_GROUP_DIM, POOL_GROUP_DIM),
                       POOL_GROUP_DIM ** -0.5 * DEEPNORM_BETA),
        "pool_scale": 1.0 + nrm(ks[8], (N_POOL_LAYERS, D_MODEL), 0.1),
        "ffn_w_up": nrm(ks[9], (DEPTH, D_MODEL, 2 * D_FF), D_MODEL ** -0.5),
        "ffn_conv_w": nrm(ks[10], (DEPTH, CONV_WIDTH, 2 * D_FF), CONV_WIDTH ** -0.5),
        "ffn_conv_b": nrm(ks[11], (DEPTH, 2 * D_FF), 0.02),
        "ffn_w_down": nrm(ks[12], (DEPTH, D_FF, D_MODEL), D_FF ** -0.5 * DEEPNORM_BETA),
        "ln_mix_g": 1.0 + nrm(ks[13], (DEPTH, D_MODEL), 0.05),
        "ln_mix_b": nrm(ks[14], (DEPTH, D_MODEL), 0.02),
        "ln_ffn_g": 1.0 + nrm(ks[15], (DEPTH, D_MODEL), 0.05),
        "ln_ffn_b": nrm(ks[16], (DEPTH, D_MODEL), 0.02),
    }


def reference(x, attn_w_in, attn_w_out, cmp_w_k, cmp_w_v, cmp_pe_k, cmp_pe_v, pool_w, pool_scale,
              ffn_w_up, ffn_conv_w, ffn_conv_b, ffn_w_down, ln_mix_g, ln_mix_b, ln_ffn_g, ln_ffn_b):
    for layer in range(DEPTH):
        i = layer // 2
        if layer % 2 == 0:
            m = attention_mixer(x, attn_w_in[i], attn_w_out[i], cmp_w_k[i], cmp_w_v[i], cmp_pe_k[i], cmp_pe_v[i])
        else:
            m = pool_mixer(x, pool_w[i], pool_scale[i])
        x = layer_norm(DEEPNORM_ALPHA * x + m, ln_mix_g[layer], ln_mix_b[layer])
        f = conv_ffn(x, ffn_w_up[layer], ffn_conv_w[layer], ffn_conv_b[layer], ffn_w_down[layer])
        x = layer_norm(DEEPNORM_ALPHA * x + f, ln_ffn_g[layer], ln_ffn_b[layer])
    return x
```

```python
import functools

import jax
import jax.numpy as jnp
import numpy as np
from jax import lax
from jax.experimental import pallas as pl
from jax.experimental.pallas import tpu as pltpu

F32 = jnp.float32
BF16 = jnp.bfloat16

HEAD_DIM = 128
NSA_HEADS = 16
NSA_KV_HEADS = 4
NSA_GROUP = 4
SB_HEADS = 16
CMP_BLOCK = 32
CMP_STRIDE = 16
SEL_BLOCK = 64
SEL_TOP_N = 16
WINDOW = 512
ROPE_THETA = 10000.0
POOL_WINDOWS = (2, 4, 8, 16)
CONV_WIDTH = 3
LN_EPS = 1e-5
NEG_INF = -1e30
FORCE_SCORE = 1e9
DEPTH = 2
DEEPNORM_ALPHA = (2 * DEPTH) ** 0.25
SCALE = HEAD_DIM ** -0.5

COL_Q_A, COL_KC, COL_VC, COL_KS, COL_VS, COL_KW, COL_VW, COL_Q_B, COL_K_B, COL_V_B = 0, 16, 20, 24, 28, 32, 36, 40, 56, 72
N_HEAD_COLS = 88
Q_NSA_DIM = NSA_HEADS * HEAD_DIM
KV_NSA_DIM = NSA_KV_HEADS * HEAD_DIM
GATE_DIM = 3 * NSA_HEADS
GATE_OFF = Q_NSA_DIM + 6 * KV_NSA_DIM

VMEM_LIMIT_BYTES = 56 * 1024 * 1024


def _params(sem, vmem=VMEM_LIMIT_BYTES):
    return pltpu.CompilerParams(dimension_semantics=sem, vmem_limit_bytes=vmem)


def _dot(a, b):
    return jnp.dot(a, b, preferred_element_type=F32)


def _dot_nt(a, b):
    return lax.dot_general(a, b, (((1,), (1,)), ((), ())), preferred_element_type=F32)


def _layer_norm_rows(y, g, b):
    mu = jnp.mean(y, axis=-1, keepdims=True)
    d = y - mu
    var = jnp.mean(d * d, axis=-1, keepdims=True)
    return d * lax.rsqrt(var + LN_EPS) * g + b


def _inproj_kernel(a_ref, b_ref, cos_ref, sin_ref, o_ref, *, heads_per_tile):
    j = pl.program_id(1)
    acc = _dot(a_ref[...], b_ref[...])
    head0 = j * heads_per_tile
    is_rope = (head0 < COL_KC) | ((head0 >= COL_KS) & (head0 < COL_VS)) | ((head0 >= COL_KW) & (head0 < COL_VW))

    @pl.when(is_rope)
    def _():
        c = cos_ref[...]
        s = sin_ref[...]
        for hh in range(heads_per_tile):
            blk = acc[:, hh * HEAD_DIM:(hh + 1) * HEAD_DIM]
            rot = pltpu.roll(blk, HEAD_DIM // 2, axis=1)
            o_ref[:, hh * HEAD_DIM:(hh + 1) * HEAD_DIM] = (blk * c + rot * s).astype(o_ref.dtype)

    @pl.when(jnp.logical_not(is_rope))
    def _():
        o_ref[...] = acc.astype(o_ref.dtype)


def _inproj(xb, w_heads, cos_t, sin_t, seq, *, tm, tn):
    m, kdim = xb.shape
    n = w_heads.shape[1]
    assert tn % HEAD_DIM == 0 and KV_NSA_DIM % tn == 0 and seq % tm == 0
    return pl.pallas_call(
        functools.partial(_inproj_kernel, heads_per_tile=tn // HEAD_DIM),
        grid=(m // tm, n // tn),
        in_specs=[
            pl.BlockSpec((tm, kdim), lambda i, j: (i, 0)),
            pl.BlockSpec((kdim, tn), lambda i, j: (0, j)),
            pl.BlockSpec((tm, HEAD_DIM), lambda i, j: (i % (seq // tm), 0)),
            pl.BlockSpec((tm, HEAD_DIM), lambda i, j: (i % (seq // tm), 0)),
        ],
        out_specs=pl.BlockSpec((tm, tn), lambda i, j: (i, j)),
        out_shape=jax.ShapeDtypeStruct((m, n), BF16),
        compiler_params=_params(("parallel", "arbitrary")),
        name="inproj_rope",
    )(xb, w_heads, cos_t, sin_t)


def _gate_kernel(a_ref, b_ref, o_ref):
    o_ref[...] = jax.nn.sigmoid(_dot(a_ref[...], b_ref[...]))


def _gate_proj(xb, w_gate, *, tm):
    m, kdim = xb.shape
    n = w_gate.shape[1]
    return pl.pallas_call(
        _gate_kernel,
        grid=(m // tm,),
        in_specs=[pl.BlockSpec((tm, kdim), lambda i: (i, 0)), pl.BlockSpec((kdim, n), lambda i: (0, 0))],
        out_specs=pl.BlockSpec((tm, n), lambda i: (i, 0)),
        out_shape=jax.ShapeDtypeStruct((m, n), F32),
        compiler_params=_params(("parallel",)),
        name="gate_proj",
    )(xb, w_gate)


def _compress_kernel(k2_ref, v2_ref, wk_ref, wv_ref, pek_ref, pev_ref, cos_ref, sin_ref, kc_ref, vc_ref, *, n_cmp):
    half = wk_ref.shape[0] // 2
    n_rows = k2_ref.shape[0]
    row = lax.broadcasted_iota(jnp.int32, (n_rows, HEAD_DIM), 0)

    def compress(x2_ref, w_ref, pe_ref):
        x2 = x2_ref[...]
        lo = _dot(x2, w_ref[:half, :])
        hi = _dot(x2, w_ref[half:, :])
        pe_term = _dot(pe_ref[...], w_ref[...])[0:1, :]
        return lo + pltpu.roll(hi, n_rows - 1, axis=0) + pe_term

    kc = compress(k2_ref, wk_ref, pek_ref)
    vc = compress(v2_ref, wv_ref, pev_ref)
    kc = kc * cos_ref[...] + pltpu.roll(kc, HEAD_DIM // 2, axis=1) * sin_ref[...]
    kc_ref[...] = jnp.where(row < n_cmp, kc, 0.0).astype(kc_ref.dtype)
    vc_ref[...] = jnp.where(row < n_cmp, vc, 0.0).astype(vc_ref.dtype)


def _compress(k2, v2, wk, wv, pek, pev, cos_c, sin_c, n_cmp):
    bsz, nkv, n_rows, wide = k2.shape
    blk4 = pl.BlockSpec((None, None, n_rows, wide), lambda b, k: (b, k, 0, 0))
    full = lambda a: pl.BlockSpec(a.shape, lambda b, k: (0,) * a.ndim)
    out_blk = pl.BlockSpec((None, None, n_rows, HEAD_DIM), lambda b, k: (b, k, 0, 0))
    out_sds = jax.ShapeDtypeStruct((bsz, nkv, n_rows, HEAD_DIM), BF16)
    return pl.pallas_call(
        functools.partial(_compress_kernel, n_cmp=n_cmp),
        grid=(bsz, nkv),
        in_specs=[blk4, blk4, full(wk), full(wv), full(pek), full(pev), full(cos_c), full(sin_c)],
        out_specs=[out_blk, out_blk],
        out_shape=[out_sds, out_sds],
        compiler_params=_params(("parallel", "parallel")),
        name="nsa_compress",
    )(k2, v2, wk, wv, pek, pev, cos_c, sin_c)


def _nsa_kernel(q_ref, kc_ref, vc_ref, ks_ref, vs_ref, kw_ref, vw_ref, g_ref, ovl_ref, eind_ref, o_ref,
                m_sc, l_sc, acc_sc, *, tq, n_cmp, n_blk):
    qi = pl.program_id(2)
    t0 = qi * tq
    rows = NSA_GROUP * tq
    q = q_ref[...]
    q4 = jnp.concatenate([q[:, g * HEAD_DIM:(g + 1) * HEAD_DIM] for g in range(NSA_GROUP)], axis=0)
    t_row = t0 + (lax.broadcasted_iota(jnp.int32, (rows, HEAD_DIM), 0) & (tq - 1))
    lane = lax.broadcasted_iota(jnp.int32, (rows, HEAD_DIM), 1)

    s = _dot_nt(q4, kc_ref[...]) * SCALE
    valid = (lane * CMP_STRIDE + (CMP_BLOCK - 1) <= t_row) & (lane < n_cmp)
    s = jnp.where(valid, s, NEG_INF)
    e = jnp.exp(s - jnp.max(s, axis=-1, keepdims=True))
    p = e / jnp.sum(e, axis=-1, keepdims=True)
    p = jnp.where(t_row >= CMP_BLOCK - 1, p, 0.0)
    o_cmp = _dot(p.astype(BF16), vc_ref[...])

    psum = p[0:tq] + p[tq:2 * tq] + p[2 * tq:3 * tq] + p[3 * tq:4 * tq]
    imp_t = lax.dot_general(ovl_ref[...], psum, (((1,), (1,)), ((), ())), precision=lax.Precision.HIGHEST,
                            preferred_element_type=F32)[0:n_blk, :]
    jrow = lax.broadcasted_iota(jnp.int32, (n_blk, tq), 0)
    t_lane = t0 + lax.broadcasted_iota(jnp.int32, (n_blk, tq), 1)
    cur = t_lane >> 6
    forced = (jrow == 0) | (jrow == cur) | (jrow == cur - 1)
    imp_t = jnp.where(forced, FORCE_SCORE, jnp.where(jrow * SEL_BLOCK <= t_lane, imp_t, NEG_INF))
    rank = jnp.zeros((n_blk, tq), F32)
    for jp in range(n_blk):
        other = imp_t[jp:jp + 1, :]
        ahead = (other > imp_t) | ((other == imp_t) & (jrow > jp))
        rank = rank + jnp.where(ahead, 1.0, 0.0)
    bias_t = jnp.where(rank < SEL_TOP_N, 0.0, NEG_INF)
    bias_t = jnp.concatenate([bias_t, jnp.zeros((HEAD_DIM - n_blk, tq), F32)], axis=0)
    bias = bias_t.T.astype(BF16)
    q_aug = jnp.concatenate([q4, jnp.concatenate([bias] * NSA_GROUP, axis=0)], axis=1)

    def flash(first, count, score_fn, v_ref):
        m_sc[...] = jnp.full(m_sc.shape, NEG_INF, F32)
        l_sc[...] = jnp.zeros(l_sc.shape, F32)
        acc_sc[...] = jnp.zeros(acc_sc.shape, F32)

        def body(it, carry):
            s0 = pl.multiple_of((first + it) * tq, tq)
            sc = score_fn(s0)
            m_old = m_sc[...]
            m_new = jnp.maximum(m_old, jnp.max(sc, axis=-1, keepdims=True))
            pr = jnp.exp(sc - m_new)
            corr = jnp.exp(m_old - m_new)
            l_sc[...] = corr * l_sc[...] + jnp.sum(pr, axis=-1, keepdims=True)
            acc_sc[...] = corr * acc_sc[...] + _dot(pr.astype(BF16), v_ref[pl.ds(s0, tq), :])
            m_sc[...] = m_new
            return carry

        lax.fori_loop(0, count, body, 0)
        return acc_sc[...] / l_sc[...]

    def slc_scores(s0):
        k_aug = jnp.concatenate([ks_ref[pl.ds(s0, tq), :], eind_ref[pl.ds(s0, tq), :]], axis=1)
        sc = _dot_nt(q_aug, k_aug) * SCALE
        return jnp.where(s0 + lane <= t_row, sc, NEG_INF)

    def win_scores(s0):
        sc = _dot_nt(q4, kw_ref[pl.ds(s0, tq), :]) * SCALE
        diff = t_row - (s0 + lane)
        return jnp.where((diff >= 0) & (diff < WINDOW), sc, NEG_INF)

    o_slc = flash(0, qi + 1, slc_scores, vs_ref)
    first_w = jnp.maximum(qi - WINDOW // tq, 0)
    o_win = flash(first_w, qi + 1 - first_w, win_scores, vw_ref)

    gates = g_ref[...]
    for g in range(NSA_GROUP):
        sl = slice(g * tq, (g + 1) * tq)
        o = (gates[:, g:g + 1] * o_cmp[sl]
             + gates[:, NSA_GROUP + g:NSA_GROUP + g + 1] * o_slc[sl]
             + gates[:, 2 * NSA_GROUP + g:2 * NSA_GROUP + g + 1] * o_win[sl])
        o_ref[:, g * HEAD_DIM:(g + 1) * HEAD_DIM] = o.astype(o_ref.dtype)


def _nsa(h, kc, vc, gates, ovl_t, eind, bsz, seq, *, tq):
    assert tq == HEAD_DIM and seq % tq == 0 and WINDOW % tq == 0
    m = h.shape[0]
    nq = seq // tq
    n_cmp = (seq - CMP_BLOCK) // CMP_STRIDE + 1
    n_blk = seq // SEL_BLOCK
    qw = NSA_GROUP * HEAD_DIM
    kv_spec = lambda col0: pl.BlockSpec((seq, HEAD_DIM), lambda b, k, i: (b, col0 + k))
    cmp_spec = pl.BlockSpec((None, None, kc.shape[2], HEAD_DIM), lambda b, k, i: (b, k, 0, 0))
    return pl.pallas_call(
        functools.partial(_nsa_kernel, tq=tq, n_cmp=n_cmp, n_blk=n_blk),
        grid=(bsz, NSA_KV_HEADS, nq),
        in_specs=[
            pl.BlockSpec((tq, qw), lambda b, k, i: (b * nq + i, k)),
            cmp_spec, cmp_spec,
            kv_spec(COL_KS), kv_spec(COL_VS), kv_spec(COL_KW), kv_spec(COL_VW),
            pl.BlockSpec((tq, HEAD_DIM), lambda b, k, i: (b * nq + i, k)),
            pl.BlockSpec(ovl_t.shape, lambda b, k, i: (0, 0)),
            pl.BlockSpec(eind.shape, lambda b, k, i: (0, 0)),
        ],
        out_specs=pl.BlockSpec((tq, qw), lambda b, k, i: (b * nq + i, k)),
        out_shape=jax.ShapeDtypeStruct((m, Q_NSA_DIM), BF16),
        scratch_shapes=[pltpu.VMEM((NSA_GROUP * tq, 1), F32), pltpu.VMEM((NSA_GROUP * tq, 1), F32),
                        pltpu.VMEM((NSA_GROUP * tq, HEAD_DIM), F32)],
        compiler_params=_params(("parallel", "parallel", "arbitrary")),
        name="nsa_attention",
    )(h, kc, vc, h, h, h, h, gates, ovl_t, eind)


def _sb_kernel(q_ref, k_ref, v_ref, u_ref, o_ref, carry_sc, acc_sc, *, tq):
    qi = pl.program_id(2)
    t0 = qi * tq
    q = q_ref[...]
    t_row = t0 + lax.broadcasted_iota(jnp.int32, (tq, tq), 0)
    lane = lax.broadcasted_iota(jnp.int32, (tq, tq), 1)
    carry_sc[...] = jnp.zeros(carry_sc.shape, F32)
    acc_sc[...] = jnp.zeros(acc_sc.shape, F32)

    def body(it, c):
        s0 = pl.multiple_of((qi - it) * tq, tq)
        z = _dot_nt(q, k_ref[pl.ds(s0, tq), :]) * SCALE
        mask = s0 + lane < t_row
        sp = jnp.maximum(z, 0.0) + jnp.log(1.0 + jnp.exp(-jnp.abs(z)))
        sp = jnp.where(mask, sp, 0.0)
        sp_hi = sp.astype(BF16)
        sp_lo = (sp - sp_hi.astype(F32)).astype(BF16)
        later = _dot(sp_hi, u_ref[...]) + _dot(sp_lo, u_ref[...]) + carry_sc[...]
        a = jnp.where(mask, jnp.exp(z - sp - later), 0.0)
        acc_sc[...] += _dot(a.astype(BF16), v_ref[pl.ds(s0, tq), :])
        carry_sc[...] += jnp.sum(sp, axis=-1, keepdims=True)
        return c

    lax.fori_loop(0, qi + 1, body, 0)
    o_ref[...] = acc_sc[...].astype(o_ref.dtype)


def _stick_breaking(h, u_tri, bsz, seq, *, tq):
    m = h.shape[0]
    nq = seq // tq
    return pl.pallas_call(
        functools.partial(_sb_kernel, tq=tq),
        grid=(bsz, SB_HEADS, nq),
        in_specs=[
            pl.BlockSpec((tq, HEAD_DIM), lambda b, hh, i: (b * nq + i, COL_Q_B + hh)),
            pl.BlockSpec((seq, HEAD_DIM), lambda b, hh, i: (b, COL_K_B + hh)),
            pl.BlockSpec((seq, HEAD_DIM), lambda b, hh, i: (b, COL_V_B + hh)),
            pl.BlockSpec(u_tri.shape, lambda b, hh, i: (0, 0)),
        ],
        out_specs=pl.BlockSpec((tq, HEAD_DIM), lambda b, hh, i: (b * nq + i, hh)),
        out_shape=jax.ShapeDtypeStruct((m, SB_HEADS * HEAD_DIM), BF16),
        scratch_shapes=[pltpu.VMEM((tq, 1), F32), pltpu.VMEM((tq, HEAD_DIM), F32)],
        compiler_params=_params(("parallel", "parallel", "arbitrary")),
        name="stick_breaking",
    )(h, h, h, u_tri)


def _ln_epilogue(acc, x_ref, g_ref, b_ref, of_ref, ob_ref):
    out = _layer_norm_rows(DEEPNORM_ALPHA * x_ref[...] + acc, g_ref[...], b_ref[...])
    of_ref[...] = out
    ob_ref[...] = out.astype(ob_ref.dtype)


def _mm_ln_kernel(a_ref, w_ref, x_ref, g_ref, b_ref, of_ref, ob_ref, acc_sc, *, nk):
    k = pl.program_id(1)

    @pl.when(k == 0)
    def _():
        acc_sc[...] = jnp.zeros(acc_sc.shape, F32)

    acc_sc[...] += _dot(a_ref[...], w_ref[...])

    @pl.when(k == nk - 1)
    def _():
        _ln_epilogue(acc_sc[...], x_ref, g_ref, b_ref, of_ref, ob_ref)


def _mm2_ln_kernel(a0_ref, a1_ref, w_ref, x_ref, g_ref, b_ref, of_ref, ob_ref, acc_sc, *, nk0, nk):
    k = pl.program_id(1)

    @pl.when(k == 0)
    def _():
        acc_sc[...] = jnp.zeros(acc_sc.shape, F32)

    @pl.when(k < nk0)
    def _():
        acc_sc[...] += _dot(a0_ref[...], w_ref[...])

    @pl.when(k >= nk0)
    def _():
        acc_sc[...] += _dot(a1_ref[...], w_ref[...])

    @pl.when(k == nk - 1)
    def _():
        _ln_epilogue(acc_sc[...], x_ref, g_ref, b_ref, of_ref, ob_ref)


def _row_resident_specs(tm, n):
    row = pl.BlockSpec((tm, n), lambda i, k: (i, 0), pipeline_mode=pl.Buffered(1))
    vec = pl.BlockSpec((1, n), lambda i, k: (0, 0))
    return row, vec


def _matmul_ln(a_list, w, x, g, b, *, tm, tk, name):
    m, n = x.shape
    kdim = w.shape[0]
    nk = kdim // tk
    row, vec = _row_resident_specs(tm, n)
    if len(a_list) == 1:
        kern = functools.partial(_mm_ln_kernel, nk=nk)
        a_specs = [pl.BlockSpec((tm, tk), lambda i, k: (i, k))]
    else:
        nk0 = a_list[0].shape[1] // tk
        kern = functools.partial(_mm2_ln_kernel, nk0=nk0, nk=nk)
        a_specs = [pl.BlockSpec((tm, tk), lambda i, k: (i, jnp.minimum(k, nk0 - 1))),
                   pl.BlockSpec((tm, tk), lambda i, k: (i, jnp.maximum(k - nk0, 0)))]
    return pl.pallas_call(
        kern,
        grid=(m // tm, nk),
        in_specs=a_specs + [pl.BlockSpec((tk, n), lambda i, k: (k, 0)), row, vec, vec],
        out_specs=[row, row],
        out_shape=[jax.ShapeDtypeStruct((m, n), F32), jax.ShapeDtypeStruct((m, n), BF16)],
        scratch_shapes=[pltpu.VMEM((tm, n), F32)],
        compiler_params=_params(("parallel", "arbitrary")),
        name=name,
    )(*a_list, w, x, g, b)


def _ffn_up_kernel(a_ref, halo_ref, wg_ref, wv_ref, cwg_ref, cwv_ref, cbg_ref, cbv_ref, o_ref, *, tm, halo, tiles_per_seq):
    i = pl.program_id(0)
    keep = jnp.where(i % tiles_per_seq == 0, 0.0, 1.0)
    a = a_ref[...]
    ah = halo_ref[...]

    def conv(w_ref, cw_ref, cb_ref):
        ext = jnp.concatenate([_dot(ah, w_ref[...]) * keep, _dot(a, w_ref[...])], axis=0)
        cw = cw_ref[...]
        out = cb_ref[...]
        for kk in range(CONV_WIDTH):
            off = halo - (CONV_WIDTH - 1) + kk
            out = out + ext[off:off + tm, :] * cw[kk:kk + 1, :]
        return out

    gate = conv(wg_ref, cwg_ref, cbg_ref)
    val = conv(wv_ref, cwv_ref, cbv_ref)
    o_ref[...] = (gate * jax.nn.sigmoid(gate) * val).astype(o_ref.dtype)


def _ffn_up(xb, w_up, conv_w, conv_b, seq, *, tm, tn, halo=16):
    m, kdim = xb.shape
    f_pad = w_up.shape[1] // 2
    nj = f_pad // tn
    assert seq % tm == 0 and tm % halo == 0
    hb = tm // halo
    return pl.pallas_call(
        functools.partial(_ffn_up_kernel, tm=tm, halo=halo, tiles_per_seq=seq // tm),
        grid=(m // tm, nj),
        in_specs=[
            pl.BlockSpec((tm, kdim), lambda i, j: (i, 0)),
            pl.BlockSpec((halo, kdim), lambda i, j: (jnp.maximum(i * hb - 1, 0), 0)),
            pl.BlockSpec((kdim, tn), lambda i, j: (0, j)),
            pl.BlockSpec((kdim, tn), lambda i, j: (0, nj + j)),
            pl.BlockSpec((CONV_WIDTH, tn), lambda i, j: (0, j)),
            pl.BlockSpec((CONV_WIDTH, tn), lambda i, j: (0, nj + j)),
            pl.BlockSpec((1, tn), lambda i, j: (0, j)),
            pl.BlockSpec((1, tn), lambda i, j: (0, nj + j)),
        ],
        out_specs=pl.BlockSpec((tm, tn), lambda i, j: (i, j)),
        out_shape=jax.ShapeDtypeStruct((m, f_pad), BF16),
        compiler_params=_params(("parallel", "arbitrary")),
        name="ffn_up_conv_gate",
    )(xb, xb, w_up, w_up, conv_w, conv_w, conv_b, conv_b)


def _pool_kernel(x_ref, halo_ref, w_ref, ps_ref, g_ref, b_ref, of_ref, ob_ref, d_sc, *, tm, halo, tiles_per_seq, group_dim):
    i = pl.program_id(0)
    keep = jnp.where(i % tiles_per_seq == 0, 0.0, 1.0)
    t_in_seq = (i % tiles_per_seq) * tm + lax.broadcasted_iota(jnp.int32, (tm, group_dim), 0)
    for gi, win in enumerate(POOL_WINDOWS):
        cs = slice(gi * group_dim, (gi + 1) * group_dim)
        ext = jnp.concatenate([halo_ref[:, cs] * keep, x_ref[:, cs]], axis=0)
        acc = ext[halo:halo + tm, :]
        for back in range(1, win):
            acc = acc + ext[halo - back:halo - back + tm, :]
        count = jnp.minimum(t_in_seq + 1, win).astype(F32)
        d = acc / count - ext[halo:halo + tm, :]
        d_sc[:, cs] = _dot(d.astype(BF16), w_ref[gi])
    y = d_sc[...] * ps_ref[...]
    _ln_epilogue(y, x_ref, g_ref, b_ref, of_ref, ob_ref)


def _pool_mixer_ln(x, w_pool, pool_scale, g, b, seq, *, tm, halo=16):
    m, n = x.shape
    ngrp, group_dim = w_pool.shape[0], w_pool.shape[1]
    assert seq % tm == 0 and tm % halo == 0 and max(POOL_WINDOWS) <= halo
    hb = tm // halo
    row = pl.BlockSpec((tm, n), lambda i: (i, 0))
    vec = pl.BlockSpec((1, n), lambda i: (0, 0))
    return pl.pallas_call(
        functools.partial(_pool_kernel, tm=tm, halo=halo, tiles_per_seq=seq // tm, group_dim=group_dim),
        grid=(m // tm,),
        in_specs=[row, pl.BlockSpec((halo, n), lambda i: (jnp.maximum(i * hb - 1, 0), 0)),
                  pl.BlockSpec(w_pool.shape, lambda i: (0, 0, 0), pipeline_mode=pl.Buffered(1)), vec, vec, vec],
        out_specs=[row, row],
        out_shape=[jax.ShapeDtypeStruct((m, n), F32), jax.ShapeDtypeStruct((m, n), BF16)],
        scratch_shapes=[pltpu.VMEM((tm, n), F32)],
        compiler_params=_params(("parallel",)),
        name="pool_mixer_ln",
    )(x, x, w_pool, pool_scale, g, b)


def _rope_tables(pos):
    inv_freq = 1.0 / (ROPE_THETA ** (jnp.arange(0, HEAD_DIM, 2, dtype=F32) / HEAD_DIM))
    ang = pos.astype(F32)[:, None] * inv_freq[None, :]
    cos, sin = jnp.cos(ang), jnp.sin(ang)
    return jnp.concatenate([cos, cos], axis=-1), jnp.concatenate([-sin, sin], axis=-1)


def _overlap_t(seq):
    n_cmp = (seq - CMP_BLOCK) // CMP_STRIDE + 1
    n_blk = seq // SEL_BLOCK
    c_start = np.arange(n_cmp)[:, None] * CMP_STRIDE
    b_start = np.arange(n_blk)[None, :] * SEL_BLOCK
    ov = np.clip(np.minimum(c_start + CMP_BLOCK, b_start + SEL_BLOCK) - np.maximum(c_start, b_start), 0, None) / CMP_BLOCK
    out = np.zeros((HEAD_DIM, HEAD_DIM), np.float32)
    out[:n_blk, :n_cmp] = ov.T
    return jnp.asarray(out)


def _block_indicator(seq):
    out = np.zeros((seq, HEAD_DIM), np.float32)
    out[np.arange(seq), np.arange(seq) // SEL_BLOCK] = 1.0
    return jnp.asarray(out, BF16)


def _strict_upper_ones(n):
    return jnp.asarray(np.triu(np.ones((n, n), np.float32), 1).T, BF16)


def _pad_cols(w, n):
    return jnp.pad(w, ((0, 0), (0, n - w.shape[1])))


def _attention_layer(x, xb, w_in, w_out, w_cmp_k, w_cmp_v, pe_k, pe_v, g, b, bsz, seq):
    d = x.shape[1]
    n_cmp = (seq - CMP_BLOCK) // CMP_STRIDE + 1
    n_rows = seq // CMP_STRIDE
    w_heads = jnp.concatenate([w_in[:, :GATE_OFF], w_in[:, GATE_OFF + GATE_DIM:]], axis=1).astype(BF16)
    gate_cols = np.array([[GATE_OFF + c * NSA_HEADS + k * NSA_GROUP + gg for c in range(3) for gg in range(NSA_GROUP)]
                          for k in range(NSA_KV_HEADS)])
    w_gate = jnp.concatenate([_pad_cols(w_in[:, gate_cols[k]], HEAD_DIM) for k in range(NSA_KV_HEADS)], axis=1).astype(BF16)

    cos_t, sin_t = _rope_tables(jnp.arange(seq))
    h = _inproj(xb, w_heads, cos_t, sin_t, seq, tm=1024, tn=512)
    gates = _gate_proj(xb, w_gate, tm=1024)

    def blocks16(col0):
        part = h[:, col0 * HEAD_DIM:(col0 + NSA_KV_HEADS) * HEAD_DIM]
        part = part.reshape(bsz, n_rows, CMP_STRIDE, NSA_KV_HEADS, HEAD_DIM).transpose(0, 3, 1, 2, 4)
        return part.reshape(bsz, NSA_KV_HEADS, n_rows, CMP_STRIDE * HEAD_DIM)

    cmp_end = jnp.arange(n_rows) * CMP_STRIDE + CMP_BLOCK - 1
    cos_c, sin_c = _rope_tables(cmp_end)
    flat_pe = lambda pe: jnp.broadcast_to(pe.reshape(1, CMP_BLOCK * HEAD_DIM), (8, CMP_BLOCK * HEAD_DIM)).astype(BF16)
    flat_w = lambda w: w.reshape(CMP_BLOCK * HEAD_DIM, HEAD_DIM).astype(BF16)
    kc, vc = _compress(blocks16(COL_KC), blocks16(COL_VC), flat_w(w_cmp_k), flat_w(w_cmp_v),
                       flat_pe(pe_k), flat_pe(pe_v), cos_c, sin_c, n_cmp)

    o_a = _nsa(h, kc, vc, gates, _overlap_t(seq), _block_indicator(seq), bsz, seq, tq=128)
    o_b = _stick_breaking(h, _strict_upper_ones(256), bsz, seq, tq=256)
    return _matmul_ln([o_a, o_b], w_out.astype(BF16), x, g, b, tm=512, tk=512, name="out_proj_ln")


def _ffn_layer(x, xb, w_up, conv_w, conv_b, w_down, g, b, seq, *, tn=512, tk=1024):
    d_ff = w_down.shape[0]
    f_pad = -(-d_ff // tk) * tk
    assert f_pad % tn == 0
    split = lambda a: jnp.concatenate([_pad_cols(a[:, :d_ff], f_pad), _pad_cols(a[:, d_ff:], f_pad)], axis=1)
    w_up_p = split(w_up).astype(BF16)
    conv_w_p = split(conv_w)
    conv_b_p = split(conv_b.reshape(1, -1))
    w_down_p = jnp.pad(w_down, ((0, f_pad - d_ff), (0, 0))).astype(BF16)
    gated = _ffn_up(xb, w_up_p, conv_w_p, conv_b_p, seq, tm=1024, tn=tn)
    return _matmul_ln([gated], w_down_p, x, g, b, tm=512, tk=tk, name="ffn_down_ln")


def kernel(x, attn_w_in, attn_w_out, cmp_w_k, cmp_w_v, cmp_pe_k, cmp_pe_v, pool_w, pool_scale, ffn_w_up, ffn_conv_w,
           ffn_conv_b, ffn_w_down, ln_mix_g, ln_mix_b, ln_ffn_g, ln_ffn_b):
    bsz, seq, d = x.shape
    xf = x.reshape(bsz * seq, d)
    xb = xf.astype(BF16)
    vec = lambda a, layer: a[layer].reshape(1, d)
    for layer in range(DEPTH):
        i = layer // 2
        if layer % 2 == 0:
            xf, xb = _attention_layer(xf, xb, attn_w_in[i], attn_w_out[i], cmp_w_k[i], cmp_w_v[i], cmp_pe_k[i],
                                      cmp_pe_v[i], vec(ln_mix_g, layer), vec(ln_mix_b, layer), bsz, seq)
        else:
            xf, xb = _pool_mixer_ln(xf, pool_w[i].astype(BF16), pool_scale[i].reshape(1, d), vec(ln_mix_g, layer),
                                    vec(ln_mix_b, layer), seq, tm=256)
        xf, xb = _ffn_layer(xf, xb, ffn_w_up[layer], ffn_conv_w[layer], ffn_conv_b[layer], ffn_w_down[layer],
                            vec(ln_ffn_g, layer), vec(ln_ffn_b, layer), seq)
    return xf.reshape(bsz, seq, d)
```

```python
import functools

import jax
import jax.numpy as jnp
import numpy as np
from jax import lax
from jax.experimental import pallas as pl
from jax.experimental.pallas import tpu as pltpu

F32 = jnp.float32
BF16 = jnp.bfloat16

HEAD_DIM = 128
NSA_HEADS = 16
NSA_KV_HEADS = 4
NSA_GROUP = 4
SB_HEADS = 16
CMP_BLOCK = 32
CMP_STRIDE = 16
SEL_BLOCK = 64
SEL_TOP_N = 16
WINDOW = 512
ROPE_THETA = 10000.0
POOL_WINDOWS = (2, 4, 8, 16)
CONV_WIDTH = 3
LN_EPS = 1e-5
NEG_INF = -1e30
FORCE_SCORE = 1e9
DEPTH = 2
DEEPNORM_ALPHA = (2 * DEPTH) ** 0.25
SCALE = HEAD_DIM ** -0.5
LOG2E = 1.4426950408889634

COL_Q_A, COL_KC, COL_VC, COL_KS, COL_VS, COL_KW, COL_VW, COL_Q_B, COL_K_B, COL_V_B = 0, 16, 20, 24, 28, 32, 36, 40, 56, 72
N_HEAD_COLS = 88
Q_NSA_DIM = NSA_HEADS * HEAD_DIM
KV_NSA_DIM = NSA_KV_HEADS * HEAD_DIM
GATE_DIM = 3 * NSA_HEADS
GATE_OFF = Q_NSA_DIM + 6 * KV_NSA_DIM

VMEM_LIMIT_BYTES = 56 * 1024 * 1024


def _params(sem, vmem=VMEM_LIMIT_BYTES):
    return pltpu.CompilerParams(dimension_semantics=sem, vmem_limit_bytes=vmem)


def _dot(a, b):
    return jnp.dot(a, b, preferred_element_type=F32)


def _dot_nt(a, b):
    return lax.dot_general(a, b, (((1,), (1,)), ((), ())), preferred_element_type=F32)


def _layer_norm_rows(y, g, b):
    mu = jnp.mean(y, axis=-1, keepdims=True)
    d = y - mu
    var = jnp.mean(d * d, axis=-1, keepdims=True)
    return d * lax.rsqrt(var + LN_EPS) * g + b


def _transpose_bf16(x):
    return x.astype(F32).T.astype(BF16)


def _inproj_kernel(a_ref, b_ref, cos_ref, sin_ref, o_ref, *, heads_per_tile):
    j = pl.program_id(1)
    acc = _dot(a_ref[...], b_ref[...])
    head0 = j * heads_per_tile
    is_rope = (head0 < COL_KC) | ((head0 >= COL_KS) & (head0 < COL_VS)) | ((head0 >= COL_KW) & (head0 < COL_VW))

    is_query = (head0 < COL_KC) | ((head0 >= COL_Q_B) & (head0 < COL_K_B))
    q_scale = jnp.where(is_query, SCALE * LOG2E, 1.0).astype(F32)

    @pl.when(is_rope)
    def _():
        c = cos_ref[...] * q_scale
        s = sin_ref[...] * q_scale
        for hh in range(heads_per_tile):
            blk = acc[:, hh * HEAD_DIM:(hh + 1) * HEAD_DIM]
            rot = pltpu.roll(blk, HEAD_DIM // 2, axis=1)
            o_ref[:, hh * HEAD_DIM:(hh + 1) * HEAD_DIM] = (blk * c + rot * s).astype(o_ref.dtype)

    @pl.when(jnp.logical_not(is_rope))
    def _():
        o_ref[...] = (acc * q_scale).astype(o_ref.dtype)


def _inproj(xb, w_heads, cos_t, sin_t, seq, *, tm, tn):
    m, kdim = xb.shape
    n = w_heads.shape[1]
    assert tn % HEAD_DIM == 0 and KV_NSA_DIM % tn == 0 and seq % tm == 0
    return pl.pallas_call(
        functools.partial(_inproj_kernel, heads_per_tile=tn // HEAD_DIM),
        grid=(m // tm, n // tn),
        in_specs=[
            pl.BlockSpec((tm, kdim), lambda i, j: (i, 0)),
            pl.BlockSpec((kdim, tn), lambda i, j: (0, j)),
            pl.BlockSpec((tm, HEAD_DIM), lambda i, j: (i % (seq // tm), 0)),
            pl.BlockSpec((tm, HEAD_DIM), lambda i, j: (i % (seq // tm), 0)),
        ],
        out_specs=pl.BlockSpec((tm, tn), lambda i, j: (i, j)),
        out_shape=jax.ShapeDtypeStruct((m, n), BF16),
        compiler_params=_params(("parallel", "arbitrary")),
        name="inproj_rope",
    )(xb, w_heads, cos_t, sin_t)


def _gate_kernel(a_ref, b_ref, o_ref):
    o_ref[...] = jax.nn.sigmoid(_dot(a_ref[...], b_ref[...]))


def _gate_proj(xb, w_gate, *, tm):
    m, kdim = xb.shape
    n = w_gate.shape[1]
    return pl.pallas_call(
        _gate_kernel,
        grid=(m // tm,),
        in_specs=[pl.BlockSpec((tm, kdim), lambda i: (i, 0)), pl.BlockSpec((kdim, n), lambda i: (0, 0))],
        out_specs=pl.BlockSpec((tm, n), lambda i: (i, 0)),
        out_shape=jax.ShapeDtypeStruct((m, n), F32),
        compiler_params=_params(("parallel",)),
        name="gate_proj",
    )(xb, w_gate)


def _compress_kernel(k2_ref, v2_ref, wk_ref, wv_ref, pek_ref, pev_ref, cos_ref, sin_ref, kc_ref, vct_ref, *, n_cmp):
    half = wk_ref.shape[0] // 2
    n_rows = k2_ref.shape[0]
    row = lax.broadcasted_iota(jnp.int32, (n_rows, HEAD_DIM), 0)

    def compress(x2_ref, w_ref, pe_ref):
        x2 = x2_ref[...]
        lo = _dot(x2, w_ref[:half, :])
        hi = _dot(x2, w_ref[half:, :])
        pe_term = _dot(pe_ref[...], w_ref[...])[0:1, :]
        return lo + pltpu.roll(hi, n_rows - 1, axis=0) + pe_term

    kc = compress(k2_ref, wk_ref, pek_ref)
    vc = compress(v2_ref, wv_ref, pev_ref)
    kc = kc * cos_ref[...] + pltpu.roll(kc, HEAD_DIM // 2, axis=1) * sin_ref[...]
    kc_ref[...] = jnp.where(row < n_cmp, kc, 0.0).astype(kc_ref.dtype)
    vct_ref[...] = jnp.where(row < n_cmp, vc, 0.0).T.astype(vct_ref.dtype)


def _compress(k2, v2, wk, wv, pek, pev, cos_c, sin_c, n_cmp):
    bsz, nkv, n_rows, wide = k2.shape
    assert n_rows == HEAD_DIM
    blk4 = pl.BlockSpec((None, None, n_rows, wide), lambda b, k: (b, k, 0, 0))
    full = lambda a: pl.BlockSpec(a.shape, lambda b, k: (0,) * a.ndim)
    out_blk = pl.BlockSpec((None, None, n_rows, HEAD_DIM), lambda b, k: (b, k, 0, 0))
    out_sds = jax.ShapeDtypeStruct((bsz, nkv, n_rows, HEAD_DIM), BF16)
    return pl.pallas_call(
        functools.partial(_compress_kernel, n_cmp=n_cmp),
        grid=(bsz, nkv),
        in_specs=[blk4, blk4, full(wk), full(wv), full(pek), full(pev), full(cos_c), full(sin_c)],
        out_specs=[out_blk, out_blk],
        out_shape=[out_sds, out_sds],
        compiler_params=_params(("parallel", "parallel")),
        name="nsa_compress",
    )(k2, v2, wk, wv, pek, pev, cos_c, sin_c)


def _nsa_kernel(q_ref, kc_ref, vct_ref, ks_ref, vs_ref, kw_ref, vw_ref, g_ref, ovl_ref, eind_ref, o_ref,
                vst_sc, vwt_sc, m_sc, l_sc, acc_sc, *, tq, tk, n_cmp, n_blk):
    qi = pl.program_id(2)
    t0 = qi * tq
    cols = NSA_GROUP * tq

    @pl.when(qi == 0)
    def _():
        for kt in range(vst_sc.shape[0]):
            vst_sc[kt] = _transpose_bf16(vs_ref[kt * tk:(kt + 1) * tk, :])
            vwt_sc[kt] = _transpose_bf16(vw_ref[kt * tk:(kt + 1) * tk, :])

    q = q_ref[...]
    q4 = jnp.concatenate([q[:, g * HEAD_DIM:(g + 1) * HEAD_DIM] for g in range(NSA_GROUP)], axis=0)
    key_minus_query = (lax.broadcasted_iota(jnp.int32, (tk, cols), 0)
                       - (lax.broadcasted_iota(jnp.int32, (tk, cols), 1) & (tq - 1)))

    n_rows = kc_ref.shape[0]
    row = lax.broadcasted_iota(jnp.int32, (n_rows, cols), 0)
    t_abs = t0 + (lax.broadcasted_iota(jnp.int32, (n_rows, cols), 1) & (tq - 1))
    s = _dot_nt(kc_ref[...], q4)
    valid = (row * CMP_STRIDE + (CMP_BLOCK - 1) <= t_abs) & (row < n_cmp)
    s = jnp.where(valid, s, NEG_INF)
    e = jnp.exp2(s - jnp.max(s, axis=0, keepdims=True))
    p = e * (1.0 / jnp.sum(e, axis=0, keepdims=True))
    p = jnp.where(t_abs >= CMP_BLOCK - 1, p, 0.0)
    o_cmp = _dot(vct_ref[...], p.astype(BF16))

    psum = p[:, 0:tq] + p[:, tq:2 * tq] + p[:, 2 * tq:3 * tq] + p[:, 3 * tq:4 * tq]
    imp = jnp.dot(ovl_ref[...], psum, precision=lax.Precision.HIGHEST, preferred_element_type=F32)[0:n_blk, :]
    jrow = lax.broadcasted_iota(jnp.int32, (n_blk, tq), 0)
    t_lane = t0 + lax.broadcasted_iota(jnp.int32, (n_blk, tq), 1)
    cur = t_lane >> 6
    forced = (jrow == 0) | (jrow == cur) | (jrow == cur - 1)
    imp = jnp.where(forced, FORCE_SCORE, jnp.where(jrow * SEL_BLOCK <= t_lane, imp, NEG_INF))
    rank = jnp.zeros((n_blk, tq), F32)
    for jp in range(n_blk):
        other = imp[jp:jp + 1, :]
        ahead = (other > imp) | ((other == imp) & (jrow > jp))
        rank = rank + jnp.where(ahead, 1.0, 0.0)
    bias_t = jnp.where(rank < SEL_TOP_N, 0.0, NEG_INF)
    bias_t = jnp.concatenate([bias_t, jnp.zeros((HEAD_DIM - n_blk, tq), F32)], axis=0)
    bias = bias_t.T.astype(BF16)
    q_aug = jnp.concatenate([q4, jnp.concatenate([bias] * NSA_GROUP, axis=0)], axis=1)

    def slc_scores(kt):
        s0 = pl.multiple_of(kt * tk, tk)
        k_aug = jnp.concatenate([ks_ref[pl.ds(s0, tk), :], eind_ref[pl.ds(s0, tk), :]], axis=1)
        return _dot_nt(k_aug, q_aug)

    def online_step(sc, vt):
        m_old = m_sc[...]
        m_new = jnp.maximum(m_old, jnp.max(sc, axis=0, keepdims=True))
        pr = jnp.exp2(sc - m_new)
        corr = jnp.exp2(m_old - m_new)
        l_sc[...] = corr * l_sc[...] + jnp.sum(pr, axis=0, keepdims=True)
        acc_sc[...] = corr * acc_sc[...] + _dot(vt, pr.astype(BF16))
        m_sc[...] = m_new

    m_sc[...] = jnp.full(m_sc.shape, NEG_INF, F32)
    l_sc[...] = jnp.zeros(l_sc.shape, F32)
    acc_sc[...] = jnp.zeros(acc_sc.shape, F32)

    def slc_body(kt, carry):
        online_step(slc_scores(kt), vst_sc[kt])
        return carry

    lax.fori_loop(0, qi, slc_body, 0)
    online_step(jnp.where(key_minus_query <= 0, slc_scores(qi), NEG_INF), vst_sc[qi])
    o_slc = acc_sc[...] * (1.0 / l_sc[...])

    def win_tile(back):
        kt = qi - back
        kt_c = jnp.maximum(kt, 0)
        sc = _dot_nt(kw_ref[pl.ds(pl.multiple_of(kt_c * tk, tk), tk), :], q4)
        hi, lo = back * tk, back * tk - WINDOW
        if lo >= -(tq - 1):
            lo_dyn = jnp.where(kt >= 0, lo, tk) if back else lo
            sc = jnp.where(key_minus_query > lo_dyn, sc, NEG_INF)
        elif back:
            sc = jnp.where(kt >= 0, sc, NEG_INF)
        if hi < tk - 1:
            sc = jnp.where(key_minus_query <= hi, sc, NEG_INF)
        return sc, vwt_sc[kt_c]

    w_tiles = [win_tile(back) for back in range(WINDOW // tk, -1, -1)]
    m_w = functools.reduce(jnp.maximum, [jnp.max(sc, axis=0, keepdims=True) for sc, _ in w_tiles])
    l_w = jnp.zeros_like(m_w)
    o_win = jnp.zeros((HEAD_DIM, cols), F32)
    for sc, vt in w_tiles:
        pr = jnp.exp2(sc - m_w)
        l_w = l_w + jnp.sum(pr, axis=0, keepdims=True)
        o_win = o_win + _dot(vt, pr.astype(BF16))
    o_win = o_win * (1.0 / l_w)

    gates_t = g_ref[...].T
    for g in range(NSA_GROUP):
        sl = slice(g * tq, (g + 1) * tq)
        o = (gates_t[g:g + 1, :] * o_cmp[:, sl]
             + gates_t[NSA_GROUP + g:NSA_GROUP + g + 1, :] * o_slc[:, sl]
             + gates_t[2 * NSA_GROUP + g:2 * NSA_GROUP + g + 1, :] * o_win[:, sl])
        o_ref[:, g * HEAD_DIM:(g + 1) * HEAD_DIM] = o.T.astype(o_ref.dtype)


def _nsa(h, kc, vct, gates, ovl, eind, bsz, seq, *, tq, tk):
    assert tq == tk and seq % tq == 0 and WINDOW % tk == 0
    m = h.shape[0]
    nq = seq // tq
    n_cmp = (seq - CMP_BLOCK) // CMP_STRIDE + 1
    n_blk = seq // SEL_BLOCK
    qw = NSA_GROUP * HEAD_DIM
    cols = NSA_GROUP * tq
    kv_spec = lambda col0: pl.BlockSpec((seq, HEAD_DIM), lambda b, k, i: (b, col0 + k))
    cmp_spec = pl.BlockSpec((None, None, kc.shape[2], HEAD_DIM), lambda b, k, i: (b, k, 0, 0))
    return pl.pallas_call(
        functools.partial(_nsa_kernel, tq=tq, tk=tk, n_cmp=n_cmp, n_blk=n_blk),
        grid=(bsz, NSA_KV_HEADS, nq),
        in_specs=[
            pl.BlockSpec((tq, qw), lambda b, k, i: (b * nq + i, k)),
            cmp_spec, cmp_spec,
            kv_spec(COL_KS), kv_spec(COL_VS), kv_spec(COL_KW), kv_spec(COL_VW),
            pl.BlockSpec((tq, HEAD_DIM), lambda b, k, i: (b * nq + i, k)),
            pl.BlockSpec(ovl.shape, lambda b, k, i: (0, 0)),
            pl.BlockSpec(eind.shape, lambda b, k, i: (0, 0)),
        ],
        out_specs=pl.BlockSpec((tq, qw), lambda b, k, i: (b * nq + i, k)),
        out_shape=jax.ShapeDtypeStruct((m, Q_NSA_DIM), BF16),
        scratch_shapes=[pltpu.VMEM((seq // tk, HEAD_DIM, tk), BF16), pltpu.VMEM((seq // tk, HEAD_DIM, tk), BF16),
                        pltpu.VMEM((1, cols), F32), pltpu.VMEM((1, cols), F32), pltpu.VMEM((HEAD_DIM, cols), F32)],
        compiler_params=_params(("arbitrary", "arbitrary", "arbitrary")),
        name="nsa_attention",
    )(h, kc, vct, h, h, h, h, gates, ovl, eind)


def _sb_kernel(q_ref, k_ref, v_ref, u_ref, o_ref, carry_sc, acc_sc, *, tq, tk, heads):
    qi = pl.program_id(2)
    query_minus_key = lax.broadcasted_iota(jnp.int32, (tq, tk), 0) - lax.broadcasted_iota(jnp.int32, (tq, tk), 1)
    carry_sc[...] = jnp.zeros(carry_sc.shape, F32)
    acc_sc[...] = jnp.zeros(acc_sc.shape, F32)
    u = u_ref[...]

    def tile(kt, diagonal):
        s0 = pl.multiple_of(kt * tk, tk)
        for hh in range(heads):
            cs = slice(hh * HEAD_DIM, (hh + 1) * HEAD_DIM)
            z = _dot_nt(q_ref[:, cs], k_ref[pl.ds(s0, tk), cs])
            sp = jnp.maximum(z, 0.0) + jnp.log2(1.0 + jnp.exp2(-jnp.abs(z)))
            if diagonal:
                mask = query_minus_key > 0
                sp = jnp.where(mask, sp, 0.0)
            sums = _dot(sp.astype(BF16), u)
            carry = carry_sc[hh]
            later = sums[:, :tk] + jnp.concatenate([carry] * (tk // HEAD_DIM), axis=1)
            a = jnp.exp2(z - sp - later)
            if diagonal:
                a = jnp.where(mask, a, 0.0)
            acc_sc[hh] += _dot(a.astype(BF16), v_ref[pl.ds(s0, tk), cs])
            carry_sc[hh] = carry + sums[:, tk:]

    tile(qi, True)

    def body(it, c):
        tile(qi - 1 - it, False)
        return c

    lax.fori_loop(0, qi, body, 0)
    for hh in range(heads):
        o_ref[:, hh * HEAD_DIM:(hh + 1) * HEAD_DIM] = acc_sc[hh].astype(o_ref.dtype)


def _stick_breaking(h, u_tri, bsz, seq, *, tq, tk, heads):
    assert seq % tq == 0 and tq == tk and u_tri.shape == (tk, tk + HEAD_DIM)
    assert SB_HEADS % heads == 0 and COL_Q_B % heads == 0 and COL_K_B % heads == 0 and COL_V_B % heads == 0
    m = h.shape[0]
    nq = seq // tq
    wide = heads * HEAD_DIM
    return pl.pallas_call(
        functools.partial(_sb_kernel, tq=tq, tk=tk, heads=heads),
        grid=(bsz, SB_HEADS // heads, nq),
        in_specs=[
            pl.BlockSpec((tq, wide), lambda b, hg, i: (b * nq + i, COL_Q_B // heads + hg)),
            pl.BlockSpec((seq, wide), lambda b, hg, i: (b, COL_K_B // heads + hg)),
            pl.BlockSpec((seq, wide), lambda b, hg, i: (b, COL_V_B // heads + hg)),
            pl.BlockSpec(u_tri.shape, lambda b, hg, i: (0, 0)),
        ],
        out_specs=pl.BlockSpec((tq, wide), lambda b, hg, i: (b * nq + i, hg)),
        out_shape=jax.ShapeDtypeStruct((m, SB_HEADS * HEAD_DIM), BF16),
        scratch_shapes=[pltpu.VMEM((heads, tq, HEAD_DIM), F32), pltpu.VMEM((heads, tq, HEAD_DIM), F32)],
        compiler_params=_params(("parallel", "parallel", "arbitrary")),
        name="stick_breaking",
    )(h, h, h, u_tri)


def _ln_epilogue(acc, x_ref, g_ref, b_ref, of_ref, ob_ref):
    out = _layer_norm_rows(DEEPNORM_ALPHA * x_ref[...] + acc, g_ref[...], b_ref[...])
    of_ref[...] = out
    ob_ref[...] = out.astype(ob_ref.dtype)


def _mm_ln_kernel(a_ref, w_ref, x_ref, g_ref, b_ref, of_ref, ob_ref, acc_sc, *, nk):
    k = pl.program_id(1)

    @pl.when(k == 0)
    def _():
        acc_sc[...] = jnp.zeros(acc_sc.shape, F32)

    acc_sc[...] += _dot(a_ref[...], w_ref[...])

    @pl.when(k == nk - 1)
    def _():
        _ln_epilogue(acc_sc[...], x_ref, g_ref, b_ref, of_ref, ob_ref)


def _mm2_ln_kernel(a0_ref, a1_ref, w_ref, x_ref, g_ref, b_ref, of_ref, ob_ref, acc_sc, *, nk0, nk):
    k = pl.program_id(1)

    @pl.when(k == 0)
    def _():
        acc_sc[...] = jnp.zeros(acc_sc.shape, F32)

    @pl.when(k < nk0)
    def _():
        acc_sc[...] += _dot(a0_ref[...], w_ref[...])

    @pl.when(k >= nk0)
    def _():
        acc_sc[...] += _dot(a1_ref[...], w_ref[...])

    @pl.when(k == nk - 1)
    def _():
        _ln_epilogue(acc_sc[...], x_ref, g_ref, b_ref, of_ref, ob_ref)


def _row_resident_specs(tm, n):
    row = pl.BlockSpec((tm, n), lambda i, k: (i, 0), pipeline_mode=pl.Buffered(1))
    vec = pl.BlockSpec((1, n), lambda i, k: (0, 0))
    return row, vec


def _matmul_ln(a_list, w, x, g, b, *, tm, tk, name):
    m, n = x.shape
    kdim = w.shape[0]
    nk = kdim // tk
    row, vec = _row_resident_specs(tm, n)
    if len(a_list) == 1:
        kern = functools.partial(_mm_ln_kernel, nk=nk)
        a_specs = [pl.BlockSpec((tm, tk), lambda i, k: (i, k))]
    else:
        nk0 = a_list[0].shape[1] // tk
        kern = functools.partial(_mm2_ln_kernel, nk0=nk0, nk=nk)
        a_specs = [pl.BlockSpec((tm, tk), lambda i, k: (i, jnp.minimum(k, nk0 - 1))),
                   pl.BlockSpec((tm, tk), lambda i, k: (i, jnp.maximum(k - nk0, 0)))]
    return pl.pallas_call(
        kern,
        grid=(m // tm, nk),
        in_specs=a_specs + [pl.BlockSpec((tk, n), lambda i, k: (k, 0)), row, vec, vec],
        out_specs=[row, row],
        out_shape=[jax.ShapeDtypeStruct((m, n), F32), jax.ShapeDtypeStruct((m, n), BF16)],
        scratch_shapes=[pltpu.VMEM((tm, n), F32)],
        compiler_params=_params(("parallel", "arbitrary")),
        name=name,
    )(*a_list, w, x, g, b)


def _ffn_up_kernel(a_ref, halo_ref, wg_ref, wv_ref, cwg_ref, cwv_ref, cbg_ref, cbv_ref, o_ref, *, tm, halo, tiles_per_seq):
    i = pl.program_id(0)
    keep = jnp.where(i % tiles_per_seq == 0, 0.0, 1.0)
    a = a_ref[...]
    ah = halo_ref[...]

    def conv(w_ref, cw_ref, cb_ref):
        ext = jnp.concatenate([_dot(ah, w_ref[...]) * keep, _dot(a, w_ref[...])], axis=0)
        cw = cw_ref[...]
        out = cb_ref[...]
        for kk in range(CONV_WIDTH):
            off = halo - (CONV_WIDTH - 1) + kk
            out = out + ext[off:off + tm, :] * cw[kk:kk + 1, :]
        return out

    gate = conv(wg_ref, cwg_ref, cbg_ref)
    val = conv(wv_ref, cwv_ref, cbv_ref)
    o_ref[...] = (gate * jax.nn.sigmoid(gate) * val).astype(o_ref.dtype)


def _ffn_up(xb, w_up, conv_w, conv_b, seq, *, tm, tn, halo=16):
    m, kdim = xb.shape
    f_pad = w_up.shape[1] // 2
    nj = f_pad // tn
    assert seq % tm == 0 and tm % halo == 0
    hb = tm // halo
    return pl.pallas_call(
        functools.partial(_ffn_up_kernel, tm=tm, halo=halo, tiles_per_seq=seq // tm),
        grid=(m // tm, nj),
        in_specs=[
            pl.BlockSpec((tm, kdim), lambda i, j: (i, 0)),
            pl.BlockSpec((halo, kdim), lambda i, j: (jnp.maximum(i * hb - 1, 0), 0)),
            pl.BlockSpec((kdim, tn), lambda i, j: (0, j)),
            pl.BlockSpec((kdim, tn), lambda i, j: (0, nj + j)),
            pl.BlockSpec((CONV_WIDTH, tn), lambda i, j: (0, j)),
            pl.BlockSpec((CONV_WIDTH, tn), lambda i, j: (0, nj + j)),
            pl.BlockSpec((1, tn), lambda i, j: (0, j)),
            pl.BlockSpec((1, tn), lambda i, j: (0, nj + j)),
        ],
        out_specs=pl.BlockSpec((tm, tn), lambda i, j: (i, j)),
        out_shape=jax.ShapeDtypeStruct((m, f_pad), BF16),
        compiler_params=_params(("parallel", "arbitrary")),
        name="ffn_up_conv_gate",
    )(xb, xb, w_up, w_up, conv_w, conv_w, conv_b, conv_b)


def _pool_kernel(x_ref, halo_ref, w_ref, ps_ref, g_ref, b_ref, of_ref, ob_ref, d_sc, *, tm, halo, tiles_per_seq, group_dim):
    i = pl.program_id(0)
    keep = jnp.where(i % tiles_per_seq == 0, 0.0, 1.0)
    t_in_seq = (i % tiles_per_seq) * tm + lax.broadcasted_iota(jnp.int32, (tm, group_dim), 0)
    for gi, win in enumerate(POOL_WINDOWS):
        cs = slice(gi * group_dim, (gi + 1) * group_dim)
        ext = jnp.concatenate([halo_ref[:, cs] * keep, x_ref[:, cs]], axis=0)
        acc = ext[halo:halo + tm, :]
        for back in range(1, win):
            acc = acc + ext[halo - back:halo - back + tm, :]
        count = jnp.minimum(t_in_seq + 1, win).astype(F32)
        d = acc / count - ext[halo:halo + tm, :]
        d_sc[:, cs] = _dot(d.astype(BF16), w_ref[gi])
    y = d_sc[...] * ps_ref[...]
    _ln_epilogue(y, x_ref, g_ref, b_ref, of_ref, ob_ref)


def _pool_mixer_ln(x, w_pool, pool_scale, g, b, seq, *, tm, halo=16):
    m, n = x.shape
    ngrp, group_dim = w_pool.shape[0], w_pool.shape[1]
    assert seq % tm == 0 and tm % halo == 0 and max(POOL_WINDOWS) <= halo
    hb = tm // halo
    row = pl.BlockSpec((tm, n), lambda i: (i, 0))
    vec = pl.BlockSpec((1, n), lambda i: (0, 0))
    return pl.pallas_call(
        functools.partial(_pool_kernel, tm=tm, halo=halo, tiles_per_seq=seq // tm, group_dim=group_dim),
        grid=(m // tm,),
        in_specs=[row, pl.BlockSpec((halo, n), lambda i: (jnp.maximum(i * hb - 1, 0), 0)),
                  pl.BlockSpec(w_pool.shape, lambda i: (0, 0, 0), pipeline_mode=pl.Buffered(1)), vec, vec, vec],
        out_specs=[row, row],
        out_shape=[jax.ShapeDtypeStruct((m, n), F32), jax.ShapeDtypeStruct((m, n), BF16)],
        scratch_shapes=[pltpu.VMEM((tm, n), F32)],
        compiler_params=_params(("parallel",)),
        name="pool_mixer_ln",
    )(x, x, w_pool, pool_scale, g, b)


def _rope_tables(pos):
    inv_freq = 1.0 / (ROPE_THETA ** (jnp.arange(0, HEAD_DIM, 2, dtype=F32) / HEAD_DIM))
    ang = pos.astype(F32)[:, None] * inv_freq[None, :]
    cos, sin = jnp.cos(ang), jnp.sin(ang)
    return jnp.concatenate([cos, cos], axis=-1), jnp.concatenate([-sin, sin], axis=-1)


def _overlap_t(seq):
    n_cmp = (seq - CMP_BLOCK) // CMP_STRIDE + 1
    n_blk = seq // SEL_BLOCK
    c_start = np.arange(n_cmp)[:, None] * CMP_STRIDE
    b_start = np.arange(n_blk)[None, :] * SEL_BLOCK
    ov = np.clip(np.minimum(c_start + CMP_BLOCK, b_start + SEL_BLOCK) - np.maximum(c_start, b_start), 0, None) / CMP_BLOCK
    out = np.zeros((HEAD_DIM, HEAD_DIM), np.float32)
    out[:n_blk, :n_cmp] = ov.T
    return jnp.asarray(out)


def _block_indicator(seq):
    out = np.zeros((seq, HEAD_DIM), np.float32)
    out[np.arange(seq), np.arange(seq) // SEL_BLOCK] = 1.0
    return jnp.asarray(out, BF16)


def _later_sum_matrix(n):
    u = np.tril(np.ones((n, n), np.float32), -1)
    return jnp.asarray(np.concatenate([u, np.ones((n, HEAD_DIM), np.float32)], axis=1), BF16)


def _pad_cols(w, n):
    return jnp.pad(w, ((0, 0), (0, n - w.shape[1])))


def _attention_layer(x, xb, w_in, w_out, w_cmp_k, w_cmp_v, pe_k, pe_v, g, b, bsz, seq):
    d = x.shape[1]
    n_cmp = (seq - CMP_BLOCK) // CMP_STRIDE + 1
    n_rows = seq // CMP_STRIDE
    w_heads = jnp.concatenate([w_in[:, :GATE_OFF], w_in[:, GATE_OFF + GATE_DIM:]], axis=1).astype(BF16)
    gate_cols = np.array([[GATE_OFF + c * NSA_HEADS + k * NSA_GROUP + gg for c in range(3) for gg in range(NSA_GROUP)]
                          for k in range(NSA_KV_HEADS)])
    w_gate = jnp.concatenate([_pad_cols(w_in[:, gate_cols[k]], HEAD_DIM) for k in range(NSA_KV_HEADS)], axis=1).astype(BF16)

    cos_t, sin_t = _rope_tables(jnp.arange(seq))
    h = _inproj(xb, w_heads, cos_t, sin_t, seq, tm=1024, tn=512)
    gates = _gate_proj(xb, w_gate, tm=1024)

    def blocks16(col0):
        part = h[:, col0 * HEAD_DIM:(col0 + NSA_KV_HEADS) * HEAD_DIM]
        part = part.reshape(bsz, n_rows, CMP_STRIDE, NSA_KV_HEADS, HEAD_DIM).transpose(0, 3, 1, 2, 4)
        return part.reshape(bsz, NSA_KV_HEADS, n_rows, CMP_STRIDE * HEAD_DIM)

    cmp_end = jnp.arange(n_rows) * CMP_STRIDE + CMP_BLOCK - 1
    cos_c, sin_c = _rope_tables(cmp_end)
    flat_pe = lambda pe: jnp.broadcast_to(pe.reshape(1, CMP_BLOCK * HEAD_DIM), (8, CMP_BLOCK * HEAD_DIM)).astype(BF16)
    flat_w = lambda w: w.reshape(CMP_BLOCK * HEAD_DIM, HEAD_DIM).astype(BF16)
    kc, vct = _compress(blocks16(COL_KC), blocks16(COL_VC), flat_w(w_cmp_k), flat_w(w_cmp_v),
                        flat_pe(pe_k), flat_pe(pe_v), cos_c, sin_c, n_cmp)

    o_a = _nsa(h, kc, vct, gates, _overlap_t(seq), _block_indicator(seq), bsz, seq, tq=256, tk=256)
    o_b = _stick_breaking(h, _later_sum_matrix(256), bsz, seq, tq=256, tk=256, heads=4)
    return _matmul_ln([o_a, o_b], w_out.astype(BF16), x, g, b, tm=512, tk=512, name="out_proj_ln")


def _ffn_layer(x, xb, w_up, conv_w, conv_b, w_down, g, b, seq, *, tn=512, tk=1024):
    d_ff = w_down.shape[0]
    f_pad = -(-d_ff // tk) * tk
    assert f_pad % tn == 0
    split = lambda a: jnp.concatenate([_pad_cols(a[:, :d_ff], f_pad), _pad_cols(a[:, d_ff:], f_pad)], axis=1)
    w_up_p = split(w_up).astype(BF16)
    conv_w_p = split(conv_w)
    conv_b_p = split(conv_b.reshape(1, -1))
    w_down_p = jnp.pad(w_down, ((0, f_pad - d_ff), (0, 0))).astype(BF16)
    gated = _ffn_up(xb, w_up_p, conv_w_p, conv_b_p, seq, tm=1024, tn=tn)
    return _matmul_ln([gated], w_down_p, x, g, b, tm=512, tk=tk, name="ffn_down_ln")


def kernel(x, attn_w_in, attn_w_out, cmp_w_k, cmp_w_v, cmp_pe_k, cmp_pe_v, pool_w, pool_scale, ffn_w_up, ffn_conv_w,
           ffn_conv_b, ffn_w_down, ln_mix_g, ln_mix_b, ln_ffn_g, ln_ffn_b):
    bsz, seq, d = x.shape
    xf = x.reshape(bsz * seq, d)
    xb = xf.astype(BF16)
    vec = lambda a, layer: a[layer].reshape(1, d)
    for layer in range(DEPTH):
        i = layer // 2
        if layer % 2 == 0:
            xf, xb = _attention_layer(xf, xb, attn_w_in[i], attn_w_out[i], cmp_w_k[i], cmp_w_v[i], cmp_pe_k[i],
                                      cmp_pe_v[i], vec(ln_mix_g, layer), vec(ln_mix_b, layer), bsz, seq)
        else:
            xf, xb = _pool_mixer_ln(xf, pool_w[i].astype(BF16), pool_scale[i].reshape(1, d), vec(ln_mix_g, layer),
                                    vec(ln_mix_b, layer), seq, tm=256)
        xf, xb = _ffn_layer(xf, xb, ffn_w_up[layer], ffn_conv_w[layer], ffn_conv_b[layer], ffn_w_down[layer],
                            vec(ln_ffn_g, layer), vec(ln_ffn_b, layer), seq)
    return xf.reshape(bsz, seq, d)
```

```python
import functools

import jax
import jax.numpy as jnp
import numpy as np
from jax import lax
from jax.experimental import pallas as pl
from jax.experimental.pallas import tpu as pltpu

F32 = jnp.float32
BF16 = jnp.bfloat16

HEAD_DIM = 128
NSA_HEADS = 16
NSA_KV_HEADS = 4
NSA_GROUP = 4
SB_HEADS = 16
CMP_BLOCK = 32
CMP_STRIDE = 16
SEL_BLOCK = 64
SEL_TOP_N = 16
WINDOW = 512
ROPE_THETA = 10000.0
POOL_WINDOWS = (2, 4, 8, 16)
CONV_WIDTH = 3
LN_EPS = 1e-5
NEG_INF = -1e30
FORCE_SCORE = 1e9
DEPTH = 2
DEEPNORM_ALPHA = (2 * DEPTH) ** 0.25
SCALE = HEAD_DIM ** -0.5
LOG2E = 1.4426950408889634

COL_Q_A, COL_KC, COL_VC, COL_KS, COL_VS, COL_KW, COL_VW, COL_Q_B, COL_K_B, COL_V_B = 0, 16, 20, 24, 28, 32, 36, 40, 56, 72
N_HEAD_COLS = 88
Q_NSA_DIM = NSA_HEADS * HEAD_DIM
KV_NSA_DIM = NSA_KV_HEADS * HEAD_DIM
GATE_DIM = 3 * NSA_HEADS
GATE_OFF = Q_NSA_DIM + 6 * KV_NSA_DIM

VMEM_LIMIT_BYTES = 56 * 1024 * 1024


def _params(sem, vmem=VMEM_LIMIT_BYTES):
    return pltpu.CompilerParams(dimension_semantics=sem, vmem_limit_bytes=vmem)


def _dot(a, b):
    return jnp.dot(a, b, preferred_element_type=F32)


def _dot_nt(a, b):
    return lax.dot_general(a, b, (((1,), (1,)), ((), ())), preferred_element_type=F32)


def _layer_norm_rows(y, g, b):
    mu = jnp.mean(y, axis=-1, keepdims=True)
    d = y - mu
    var = jnp.mean(d * d, axis=-1, keepdims=True)
    return d * lax.rsqrt(var + LN_EPS) * g + b


def _transpose_bf16(x):
    return x.astype(F32).T.astype(BF16)


def _inproj_kernel(a_ref, b_ref, cos_ref, sin_ref, o_ref, *, heads_per_tile, chunk):
    j = pl.program_id(1)
    head0 = j * heads_per_tile
    is_rope = (head0 < COL_KC) | ((head0 >= COL_KS) & (head0 < COL_VS)) | ((head0 >= COL_KW) & (head0 < COL_VW))
    is_query = (head0 < COL_KC) | ((head0 >= COL_Q_B) & (head0 < COL_K_B))
    q_scale = jnp.where(is_query, SCALE * LOG2E, 1.0).astype(F32)
    cos_sel = jnp.where(is_rope, q_scale, 0.0)
    pass_sel = jnp.where(is_rope, 0.0, q_scale)
    for c in range(a_ref.shape[0] // chunk):
        rows = slice(c * chunk, (c + 1) * chunk)
        acc = _dot(a_ref[rows, :], b_ref[...])
        cs = cos_ref[rows, :] * cos_sel + pass_sel
        sn = sin_ref[rows, :] * cos_sel
        for hh in range(heads_per_tile):
            blk = acc[:, hh * HEAD_DIM:(hh + 1) * HEAD_DIM]
            rot = pltpu.roll(blk, HEAD_DIM // 2, axis=1)
            o_ref[rows, hh * HEAD_DIM:(hh + 1) * HEAD_DIM] = (blk * cs + rot * sn).astype(o_ref.dtype)


def _inproj(xb, w_heads, cos_t, sin_t, seq, *, tm, tn, chunk=256):
    m, kdim = xb.shape
    n = w_heads.shape[1]
    assert tn % HEAD_DIM == 0 and KV_NSA_DIM % tn == 0 and seq % tm == 0 and tm % chunk == 0
    return pl.pallas_call(
        functools.partial(_inproj_kernel, heads_per_tile=tn // HEAD_DIM, chunk=chunk),
        grid=(m // tm, n // tn),
        in_specs=[
            pl.BlockSpec((tm, kdim), lambda i, j: (i, 0)),
            pl.BlockSpec((kdim, tn), lambda i, j: (0, j)),
            pl.BlockSpec((tm, HEAD_DIM), lambda i, j: (i % (seq // tm), 0)),
            pl.BlockSpec((tm, HEAD_DIM), lambda i, j: (i % (seq // tm), 0)),
        ],
        out_specs=pl.BlockSpec((tm, tn), lambda i, j: (i, j)),
        out_shape=jax.ShapeDtypeStruct((m, n), BF16),
        compiler_params=_params(("parallel", "arbitrary")),
        name="inproj_rope",
    )(xb, w_heads, cos_t, sin_t)


def _gate_kernel(a_ref, b_ref, o_ref):
    o_ref[...] = jax.nn.sigmoid(_dot(a_ref[...], b_ref[...]))


def _gate_proj(xb, w_gate, *, tm):
    m, kdim = xb.shape
    n = w_gate.shape[1]
    return pl.pallas_call(
        _gate_kernel,
        grid=(m // tm,),
        in_specs=[pl.BlockSpec((tm, kdim), lambda i: (i, 0)), pl.BlockSpec((kdim, n), lambda i: (0, 0))],
        out_specs=pl.BlockSpec((tm, n), lambda i: (i, 0)),
        out_shape=jax.ShapeDtypeStruct((m, n), F32),
        compiler_params=_params(("parallel",)),
        name="gate_proj",
    )(xb, w_gate)


def _compress_kernel(k2_ref, v2_ref, wk_ref, wv_ref, pek_ref, pev_ref, cos_ref, sin_ref, kc_ref, vct_ref, *, n_cmp):
    half = wk_ref.shape[0] // 2
    n_rows = k2_ref.shape[0]
    row = lax.broadcasted_iota(jnp.int32, (n_rows, HEAD_DIM), 0)

    def compress(x2_ref, w_ref, pe_ref):
        x2 = x2_ref[...]
        lo = _dot(x2, w_ref[:half, :])
        hi = _dot(x2, w_ref[half:, :])
        pe_term = _dot(pe_ref[...], w_ref[...])[0:1, :]
        return lo + pltpu.roll(hi, n_rows - 1, axis=0) + pe_term

    kc = compress(k2_ref, wk_ref, pek_ref)
    vc = compress(v2_ref, wv_ref, pev_ref)
    kc = kc * cos_ref[...] + pltpu.roll(kc, HEAD_DIM // 2, axis=1) * sin_ref[...]
    kc_ref[...] = jnp.where(row < n_cmp, kc, 0.0).astype(kc_ref.dtype)
    vct_ref[...] = jnp.where(row < n_cmp, vc, 0.0).T.astype(vct_ref.dtype)


def _compress(k2, v2, wk, wv, pek, pev, cos_c, sin_c, n_cmp):
    bsz, nkv, n_rows, wide = k2.shape
    assert n_rows == HEAD_DIM
    blk4 = pl.BlockSpec((None, None, n_rows, wide), lambda b, k: (b, k, 0, 0))
    full = lambda a: pl.BlockSpec(a.shape, lambda b, k: (0,) * a.ndim)
    out_blk = pl.BlockSpec((None, None, n_rows, HEAD_DIM), lambda b, k: (b, k, 0, 0))
    out_sds = jax.ShapeDtypeStruct((bsz, nkv, n_rows, HEAD_DIM), BF16)
    return pl.pallas_call(
        functools.partial(_compress_kernel, n_cmp=n_cmp),
        grid=(bsz, nkv),
        in_specs=[blk4, blk4, full(wk), full(wv), full(pek), full(pev), full(cos_c), full(sin_c)],
        out_specs=[out_blk, out_blk],
        out_shape=[out_sds, out_sds],
        compiler_params=_params(("parallel", "parallel")),
        name="nsa_compress",
    )(k2, v2, wk, wv, pek, pev, cos_c, sin_c)


def _nsa_kernel(q_ref, kc_ref, vct_ref, ks_ref, vs_ref, kw_ref, vw_ref, g_ref, ovl_ref, eind_ref, o_ref,
                vst_sc, vwt_sc, m_sc, l_sc, acc_sc, *, tq, tk, n_cmp, n_blk):
    qi = pl.program_id(2)
    t0 = qi * tq
    cols = NSA_GROUP * tq

    @pl.when(qi == 0)
    def _():
        for kt in range(vst_sc.shape[0]):
            vst_sc[kt] = _transpose_bf16(vs_ref[kt * tk:(kt + 1) * tk, :])
            vwt_sc[kt] = _transpose_bf16(vw_ref[kt * tk:(kt + 1) * tk, :])

    q = q_ref[...]
    q4 = jnp.concatenate([q[:, g * HEAD_DIM:(g + 1) * HEAD_DIM] for g in range(NSA_GROUP)], axis=0)
    key_minus_query = (lax.broadcasted_iota(jnp.int32, (tk, cols), 0)
                       - (lax.broadcasted_iota(jnp.int32, (tk, cols), 1) & (tq - 1)))

    n_rows = kc_ref.shape[0]
    row = lax.broadcasted_iota(jnp.int32, (n_rows, cols), 0)
    t_abs = t0 + (lax.broadcasted_iota(jnp.int32, (n_rows, cols), 1) & (tq - 1))
    s = _dot_nt(kc_ref[...], q4)
    valid = (row * CMP_STRIDE + (CMP_BLOCK - 1) <= t_abs) & (row < n_cmp)
    s = jnp.where(valid, s, NEG_INF)
    e = jnp.exp2(s - jnp.max(s, axis=0, keepdims=True))
    p = e * (1.0 / jnp.sum(e, axis=0, keepdims=True))
    p = jnp.where(t_abs >= CMP_BLOCK - 1, p, 0.0)
    o_cmp = _dot(vct_ref[...], p.astype(BF16))

    psum = p[:, 0:tq] + p[:, tq:2 * tq] + p[:, 2 * tq:3 * tq] + p[:, 3 * tq:4 * tq]
    imp = jnp.dot(ovl_ref[...], psum, precision=lax.Precision.HIGHEST, preferred_element_type=F32)[0:n_blk, :]
    jrow = lax.broadcasted_iota(jnp.int32, (n_blk, tq), 0)
    t_lane = t0 + lax.broadcasted_iota(jnp.int32, (n_blk, tq), 1)
    cur = t_lane >> 6
    forced = (jrow == 0) | (jrow == cur) | (jrow == cur - 1)
    imp = jnp.where(forced, FORCE_SCORE, jnp.where(jrow * SEL_BLOCK <= t_lane, imp, NEG_INF))
    rank = jnp.zeros((n_blk, tq), F32)
    for jp in range(n_blk):
        other = imp[jp:jp + 1, :]
        ahead = (other > imp) | ((other == imp) & (jrow > jp))
        rank = rank + jnp.where(ahead, 1.0, 0.0)
    bias_t = jnp.where(rank < SEL_TOP_N, 0.0, NEG_INF)
    bias_t = jnp.concatenate([bias_t, jnp.zeros((HEAD_DIM - n_blk, tq), F32)], axis=0)
    bias = bias_t.T.astype(BF16)
    q_aug = jnp.concatenate([q4, jnp.concatenate([bias] * NSA_GROUP, axis=0)], axis=1)

    def slc_scores(kt):
        s0 = pl.multiple_of(kt * tk, tk)
        k_aug = jnp.concatenate([ks_ref[pl.ds(s0, tk), :], eind_ref[pl.ds(s0, tk), :]], axis=1)
        return _dot_nt(k_aug, q_aug)

    def online_step(sc, vt):
        m_old = m_sc[...]
        m_new = jnp.maximum(m_old, jnp.max(sc, axis=0, keepdims=True))
        pr = jnp.exp2(sc - m_new)
        corr = jnp.exp2(m_old - m_new)
        l_sc[...] = corr * l_sc[...] + jnp.sum(pr, axis=0, keepdims=True)
        acc_sc[...] = corr * acc_sc[...] + _dot(vt, pr.astype(BF16))
        m_sc[...] = m_new

    m_sc[...] = jnp.full(m_sc.shape, NEG_INF, F32)
    l_sc[...] = jnp.zeros(l_sc.shape, F32)
    acc_sc[...] = jnp.zeros(acc_sc.shape, F32)

    def slc_body(kt, carry):
        online_step(slc_scores(kt), vst_sc[kt])
        return carry

    lax.fori_loop(0, qi, slc_body, 0)
    online_step(jnp.where(key_minus_query <= 0, slc_scores(qi), NEG_INF), vst_sc[qi])
    o_slc = acc_sc[...] * (1.0 / l_sc[...])

    def win_tile(back):
        kt = qi - back
        kt_c = jnp.maximum(kt, 0)
        sc = _dot_nt(kw_ref[pl.ds(pl.multiple_of(kt_c * tk, tk), tk), :], q4)
        hi, lo = back * tk, back * tk - WINDOW
        if lo >= -(tq - 1):
            lo_dyn = jnp.where(kt >= 0, lo, tk) if back else lo
            sc = jnp.where(key_minus_query > lo_dyn, sc, NEG_INF)
        elif back:
            sc = jnp.where(kt >= 0, sc, NEG_INF)
        if hi < tk - 1:
            sc = jnp.where(key_minus_query <= hi, sc, NEG_INF)
        return sc, vwt_sc[kt_c]

    w_tiles = [win_tile(back) for back in range(WINDOW // tk, -1, -1)]
    m_w = functools.reduce(jnp.maximum, [jnp.max(sc, axis=0, keepdims=True) for sc, _ in w_tiles])
    l_w = jnp.zeros_like(m_w)
    o_win = jnp.zeros((HEAD_DIM, cols), F32)
    for sc, vt in w_tiles:
        pr = jnp.exp2(sc - m_w)
        l_w = l_w + jnp.sum(pr, axis=0, keepdims=True)
        o_win = o_win + _dot(vt, pr.astype(BF16))
    o_win = o_win * (1.0 / l_w)

    gates_t = g_ref[...].T
    for g in range(NSA_GROUP):
        sl = slice(g * tq, (g + 1) * tq)
        o = (gates_t[g:g + 1, :] * o_cmp[:, sl]
             + gates_t[NSA_GROUP + g:NSA_GROUP + g + 1, :] * o_slc[:, sl]
             + gates_t[2 * NSA_GROUP + g:2 * NSA_GROUP + g + 1, :] * o_win[:, sl])
        o_ref[:, g * HEAD_DIM:(g + 1) * HEAD_DIM] = o.T.astype(o_ref.dtype)


def _nsa(h, kc, vct, gates, ovl, eind, bsz, seq, *, tq, tk):
    assert tq == tk and seq % tq == 0 and WINDOW % tk == 0
    m = h.shape[0]
    nq = seq // tq
    n_cmp = (seq - CMP_BLOCK) // CMP_STRIDE + 1
    n_blk = seq // SEL_BLOCK
    qw = NSA_GROUP * HEAD_DIM
    cols = NSA_GROUP * tq
    kv_spec = lambda col0: pl.BlockSpec((seq, HEAD_DIM), lambda b, k, i: (b, col0 + k))
    cmp_spec = pl.BlockSpec((None, None, kc.shape[2], HEAD_DIM), lambda b, k, i: (b, k, 0, 0))
    return pl.pallas_call(
        functools.partial(_nsa_kernel, tq=tq, tk=tk, n_cmp=n_cmp, n_blk=n_blk),
        grid=(bsz, NSA_KV_HEADS, nq),
        in_specs=[
            pl.BlockSpec((tq, qw), lambda b, k, i: (b * nq + i, k)),
            cmp_spec, cmp_spec,
            kv_spec(COL_KS), kv_spec(COL_VS), kv_spec(COL_KW), kv_spec(COL_VW),
            pl.BlockSpec((tq, HEAD_DIM), lambda b, k, i: (b * nq + i, k)),
            pl.BlockSpec(ovl.shape, lambda b, k, i: (0, 0)),
            pl.BlockSpec(eind.shape, lambda b, k, i: (0, 0)),
        ],
        out_specs=pl.BlockSpec((tq, qw), lambda b, k, i: (b * nq + i, k)),
        out_shape=jax.ShapeDtypeStruct((m, Q_NSA_DIM), BF16),
        scratch_shapes=[pltpu.VMEM((seq // tk, HEAD_DIM, tk), BF16), pltpu.VMEM((seq // tk, HEAD_DIM, tk), BF16),
                        pltpu.VMEM((1, cols), F32), pltpu.VMEM((1, cols), F32), pltpu.VMEM((HEAD_DIM, cols), F32)],
        compiler_params=_params(("arbitrary", "arbitrary", "arbitrary")),
        name="nsa_attention",
    )(h, kc, vct, h, h, h, h, gates, ovl, eind)


def _sb_kernel(q_ref, k_ref, v_ref, u_ref, o_ref, carry_sc, acc_sc, *, tq, tk, heads):
    qi = pl.program_id(2)
    query_minus_key = lax.broadcasted_iota(jnp.int32, (tq, tk), 0) - lax.broadcasted_iota(jnp.int32, (tq, tk), 1)
    carry_sc[...] = jnp.zeros(carry_sc.shape, F32)
    acc_sc[...] = jnp.zeros(acc_sc.shape, F32)
    u = u_ref[...]

    def tile(kt, diagonal):
        s0 = pl.multiple_of(kt * tk, tk)
        for hh in range(heads):
            cs = slice(hh * HEAD_DIM, (hh + 1) * HEAD_DIM)
            z = _dot_nt(q_ref[:, cs], k_ref[pl.ds(s0, tk), cs])
            sp = jnp.maximum(z, 0.0) + jnp.log2(1.0 + jnp.exp2(-jnp.abs(z)))
            if diagonal:
                mask = query_minus_key > 0
                sp = jnp.where(mask, sp, 0.0)
            sums = _dot(sp.astype(BF16), u)
            carry = carry_sc[hh]
            later = sums[:, :tk] + jnp.concatenate([carry] * (tk // HEAD_DIM), axis=1)
            a = jnp.exp2(z - sp - later)
            if diagonal:
                a = jnp.where(mask, a, 0.0)
            acc_sc[hh] += _dot(a.astype(BF16), v_ref[pl.ds(s0, tk), cs])
            carry_sc[hh] = carry + sums[:, tk:]

    tile(qi, True)

    def body(it, c):
        tile(qi - 1 - it, False)
        return c

    lax.fori_loop(0, qi, body, 0)
    for hh in range(heads):
        o_ref[:, hh * HEAD_DIM:(hh + 1) * HEAD_DIM] = acc_sc[hh].astype(o_ref.dtype)


def _stick_breaking(h, u_tri, bsz, seq, *, tq, tk, heads):
    assert seq % tq == 0 and tq == tk and u_tri.shape == (tk, tk + HEAD_DIM)
    assert SB_HEADS % heads == 0 and COL_Q_B % heads == 0 and COL_K_B % heads == 0 and COL_V_B % heads == 0
    m = h.shape[0]
    nq = seq // tq
    wide = heads * HEAD_DIM
    return pl.pallas_call(
        functools.partial(_sb_kernel, tq=tq, tk=tk, heads=heads),
        grid=(bsz, SB_HEADS // heads, nq),
        in_specs=[
            pl.BlockSpec((tq, wide), lambda b, hg, i: (b * nq + i, COL_Q_B // heads + hg)),
            pl.BlockSpec((seq, wide), lambda b, hg, i: (b, COL_K_B // heads + hg)),
            pl.BlockSpec((seq, wide), lambda b, hg, i: (b, COL_V_B // heads + hg)),
            pl.BlockSpec(u_tri.shape, lambda b, hg, i: (0, 0)),
        ],
        out_specs=pl.BlockSpec((tq, wide), lambda b, hg, i: (b * nq + i, hg)),
        out_shape=jax.ShapeDtypeStruct((m, SB_HEADS * HEAD_DIM), BF16),
        scratch_shapes=[pltpu.VMEM((heads, tq, HEAD_DIM), F32), pltpu.VMEM((heads, tq, HEAD_DIM), F32)],
        compiler_params=_params(("parallel", "parallel", "arbitrary")),
        name="stick_breaking",
    )(h, h, h, u_tri)


def _ln_epilogue(y, x_ref, g_ref, b_ref, of_ref, ob_ref, rows=slice(None)):
    out = _layer_norm_rows(DEEPNORM_ALPHA * x_ref[rows, :] + y, g_ref[...], b_ref[...])
    of_ref[rows, :] = out
    ob_ref[rows, :] = out.astype(ob_ref.dtype)


def _mm_ln_kernel(*refs, k_splits, chunk):
    n_a = len(k_splits) - 1
    a_refs = refs[:n_a]
    w_ref, x_ref, g_ref, b_ref, of_ref, ob_ref, acc_sc = refs[n_a:]
    k = pl.program_id(1)
    nk = k_splits[-1]

    for s, a_ref in enumerate(a_refs):
        lo, hi = k_splits[s], k_splits[s + 1]
        first = lo == 0
        last = hi == nk

        if first:
            @pl.when(k == 0)
            def _(a_ref=a_ref):
                acc_sc[...] = _dot(a_ref[...], w_ref[...])

        lo_acc, hi_acc = lo + (1 if first else 0), hi - (1 if last else 0)
        if hi_acc > lo_acc:
            @pl.when((k >= lo_acc) & (k < hi_acc))
            def _(a_ref=a_ref):
                acc_sc[...] += _dot(a_ref[...], w_ref[...])

        if last:
            @pl.when(k == nk - 1)
            def _(a_ref=a_ref):
                for c in range(acc_sc.shape[0] // chunk):
                    rows = slice(c * chunk, (c + 1) * chunk)
                    y = acc_sc[rows, :] + _dot(a_ref[rows, :], w_ref[...])
                    _ln_epilogue(y, x_ref, g_ref, b_ref, of_ref, ob_ref, rows)


def _row_resident_specs(tm, n):
    row = pl.BlockSpec((tm, n), lambda i, k: (i, 0), pipeline_mode=pl.Buffered(1))
    vec = pl.BlockSpec((1, n), lambda i, k: (0, 0))
    return row, vec


def _matmul_ln(a_list, w, x, g, b, *, tm, tk, name, chunk=128):
    m, n = x.shape
    kdim = w.shape[0]
    nk = kdim // tk
    k_splits = [0]
    for a in a_list:
        assert a.shape[1] % tk == 0
        k_splits.append(k_splits[-1] + a.shape[1] // tk)
    assert k_splits[-1] == nk and nk >= 2 and tm % chunk == 0
    row, vec = _row_resident_specs(tm, n)

    def a_spec(lo, hi):
        return pl.BlockSpec((tm, tk), lambda i, k: (i, jnp.clip(k - lo, 0, hi - lo - 1)))

    return pl.pallas_call(
        functools.partial(_mm_ln_kernel, k_splits=tuple(k_splits), chunk=chunk),
        grid=(m // tm, nk),
        in_specs=[a_spec(k_splits[s], k_splits[s + 1]) for s in range(len(a_list))]
        + [pl.BlockSpec((tk, n), lambda i, k: (k, 0)), row, vec, vec],
        out_specs=[row, row],
        out_shape=[jax.ShapeDtypeStruct((m, n), F32), jax.ShapeDtypeStruct((m, n), BF16)],
        scratch_shapes=[pltpu.VMEM((tm, n), F32)],
        compiler_params=_params(("parallel", "arbitrary")),
        name=name,
    )(*a_list, w, x, g, b)


def _stage_tiles_kernel(x_ref, o_ref, *, tiles_in, tiles_out):
    valid = pl.program_id(0) % tiles_out < tiles_in
    o_ref[...] = jnp.where(valid, x_ref[...], 0.0).astype(o_ref.dtype)


def _stage_ffn_weight(w_all, layer, d_ff, f_pad, axis, *, tile=256):
    assert d_ff % tile == 0 and f_pad % tile == 0
    tiles_in, tiles_out = d_ff // tile, f_pad // tile
    shape = list(w_all.shape[1:])
    sections = shape[axis] // d_ff
    src = lambda t: (t // tiles_out) * tiles_in + jnp.minimum(t % tiles_out, tiles_in - 1)
    if axis == 0:
        blk, out_shape = (tile, shape[1]), (sections * f_pad, shape[1])
        in_spec = pl.BlockSpec((None,) + blk, lambda t: (layer, src(t), 0))
        out_spec = pl.BlockSpec(blk, lambda t: (t, 0))
    else:
        blk, out_shape = (shape[0], tile), (shape[0], sections * f_pad)
        in_spec = pl.BlockSpec((None,) + blk, lambda t: (layer, 0, src(t)))
        out_spec = pl.BlockSpec(blk, lambda t: (0, t))
    return pl.pallas_call(
        functools.partial(_stage_tiles_kernel, tiles_in=tiles_in, tiles_out=tiles_out),
        grid=(sections * tiles_out,),
        in_specs=[in_spec],
        out_specs=out_spec,
        out_shape=jax.ShapeDtypeStruct(out_shape, BF16),
        compiler_params=_params(("parallel",)),
        name="stage_ffn_weight",
    )(w_all)


def _ffn_up_kernel(a_ref, halo_ref, wg_ref, wv_ref, cwg_ref, cwv_ref, cbg_ref, cbv_ref, o_ref, head_sc, hg_sc, hv_sc, *, tm, chunk, halo, tiles_per_seq):
    i = pl.program_id(0)

    @pl.when(pl.program_id(1) == 0)
    def _():
        seq_start = i % tiles_per_seq == 0
        head_sc[0:halo, :] = jnp.where(seq_start, jnp.zeros_like(halo_ref[...]), halo_ref[...])
        head_sc[halo:, :] = a_ref[...]

    def conv(h_sc, cw_ref, cb_ref):
        cw = cw_ref[...]
        out = cb_ref[...]
        for kk in range(CONV_WIDTH):
            out = out + h_sc[pl.ds(halo - (CONV_WIDTH - 1) + kk, chunk), :] * cw[kk:kk + 1, :]
        return out

    for c in range(tm // chunk):
        p = c % 2
        rows = slice(c * chunk, (c + 1) * chunk)
        if c == 0:
            lhs = head_sc[0:halo + chunk, :]
            hg_sc[p] = _dot(lhs, wg_ref[...])
            hv_sc[p] = _dot(lhs, wv_ref[...])
        else:
            lhs = head_sc[halo + c * chunk:halo + (c + 1) * chunk, :]
            hg_sc[p, 0:halo, :] = hg_sc[1 - p, chunk:, :]
            hv_sc[p, 0:halo, :] = hv_sc[1 - p, chunk:, :]
            hg_sc[p, halo:, :] = _dot(lhs, wg_ref[...])
            hv_sc[p, halo:, :] = _dot(lhs, wv_ref[...])
        gate = conv(hg_sc.at[p], cwg_ref, cbg_ref)
        val = conv(hv_sc.at[p], cwv_ref, cbv_ref)
        o_ref[rows, :] = (gate * jax.nn.sigmoid(gate) * val).astype(o_ref.dtype)


def _ffn_up(xb, w_up, conv_w, conv_b, seq, *, tm, tn, chunk, halo=16):
    m, kdim = xb.shape
    f_pad = w_up.shape[1] // 2
    nj = f_pad // tn
    assert seq % tm == 0 and tm % chunk == 0 and chunk % halo == 0
    hb = tm // halo
    return pl.pallas_call(
        functools.partial(_ffn_up_kernel, tm=tm, chunk=chunk, halo=halo, tiles_per_seq=seq // tm),
        grid=(m // tm, nj),
        in_specs=[
            pl.BlockSpec((tm, kdim), lambda i, j: (i, 0)),
            pl.BlockSpec((halo, kdim), lambda i, j: (jnp.maximum(i * hb - 1, 0), 0)),
            pl.BlockSpec((kdim, tn), lambda i, j: (0, j)),
            pl.BlockSpec((kdim, tn), lambda i, j: (0, nj + j)),
            pl.BlockSpec((CONV_WIDTH, tn), lambda i, j: (0, j)),
            pl.BlockSpec((CONV_WIDTH, tn), lambda i, j: (0, nj + j)),
            pl.BlockSpec((1, tn), lambda i, j: (0, j)),
            pl.BlockSpec((1, tn), lambda i, j: (0, nj + j)),
        ],
        out_specs=pl.BlockSpec((tm, tn), lambda i, j: (i, j)),
        out_shape=jax.ShapeDtypeStruct((m, f_pad), BF16),
        scratch_shapes=[pltpu.VMEM((halo + tm, kdim), BF16), pltpu.VMEM((2, halo + chunk, tn), F32),
                        pltpu.VMEM((2, halo + chunk, tn), F32)],
        compiler_params=_params(("parallel", "arbitrary")),
        name="ffn_up_conv_gate",
    )(xb, xb, w_up, w_up, conv_w, conv_w, conv_b, conv_b)


def _pool_kernel(x_ref, halo_ref, w_ref, ps_ref, g_ref, b_ref, of_ref, ob_ref, d_sc, *, tm, halo, tiles_per_seq, group_dim):
    i = pl.program_id(0)
    keep = jnp.where(i % tiles_per_seq == 0, 0.0, 1.0)
    t_in_seq = (i % tiles_per_seq) * tm + lax.broadcasted_iota(jnp.int32, (tm, group_dim), 0)
    for gi, win in enumerate(POOL_WINDOWS):
        cs = slice(gi * group_dim, (gi + 1) * group_dim)
        ext = jnp.concatenate([halo_ref[:, cs] * keep, x_ref[:, cs]], axis=0)
        acc = ext[halo:halo + tm, :]
        for back in range(1, win):
            acc = acc + ext[halo - back:halo - back + tm, :]
        count = jnp.minimum(t_in_seq + 1, win).astype(F32)
        d = acc / count - ext[halo:halo + tm, :]
        d_sc[:, cs] = _dot(d.astype(BF16), w_ref[gi])
    y = d_sc[...] * ps_ref[...]
    _ln_epilogue(y, x_ref, g_ref, b_ref, of_ref, ob_ref)


def _pool_mixer_ln(x, w_pool, pool_scale, g, b, seq, *, tm, halo=16):
    m, n = x.shape
    ngrp, group_dim = w_pool.shape[0], w_pool.shape[1]
    assert seq % tm == 0 and tm % halo == 0 and max(POOL_WINDOWS) <= halo
    hb = tm // halo
    row = pl.BlockSpec((tm, n), lambda i: (i, 0))
    vec = pl.BlockSpec((1, n), lambda i: (0, 0))
    return pl.pallas_call(
        functools.partial(_pool_kernel, tm=tm, halo=halo, tiles_per_seq=seq // tm, group_dim=group_dim),
        grid=(m // tm,),
        in_specs=[row, pl.BlockSpec((halo, n), lambda i: (jnp.maximum(i * hb - 1, 0), 0)),
                  pl.BlockSpec(w_pool.shape, lambda i: (0, 0, 0), pipeline_mode=pl.Buffered(1)), vec, vec, vec],
        out_specs=[row, row],
        out_shape=[jax.ShapeDtypeStruct((m, n), F32), jax.ShapeDtypeStruct((m, n), BF16)],
        scratch_shapes=[pltpu.VMEM((tm, n), F32)],
        compiler_params=_params(("parallel",)),
        name="pool_mixer_ln",
    )(x, x, w_pool, pool_scale, g, b)


def _rope_tables(pos):
    inv_freq = 1.0 / (ROPE_THETA ** (jnp.arange(0, HEAD_DIM, 2, dtype=F32) / HEAD_DIM))
    ang = pos.astype(F32)[:, None] * inv_freq[None, :]
    cos, sin = jnp.cos(ang), jnp.sin(ang)
    return jnp.concatenate([cos, cos], axis=-1), jnp.concatenate([-sin, sin], axis=-1)


def _overlap_t(seq):
    n_cmp = (seq - CMP_BLOCK) // CMP_STRIDE + 1
    n_blk = seq // SEL_BLOCK
    c_start = np.arange(n_cmp)[:, None] * CMP_STRIDE
    b_start = np.arange(n_blk)[None, :] * SEL_BLOCK
    ov = np.clip(np.minimum(c_start + CMP_BLOCK, b_start + SEL_BLOCK) - np.maximum(c_start, b_start), 0, None) / CMP_BLOCK
    out = np.zeros((HEAD_DIM, HEAD_DIM), np.float32)
    out[:n_blk, :n_cmp] = ov.T
    return jnp.asarray(out)


def _block_indicator(seq):
    out = np.zeros((seq, HEAD_DIM), np.float32)
    out[np.arange(seq), np.arange(seq) // SEL_BLOCK] = 1.0
    return jnp.asarray(out, BF16)


def _later_sum_matrix(n):
    u = np.tril(np.ones((n, n), np.float32), -1)
    return jnp.asarray(np.concatenate([u, np.ones((n, HEAD_DIM), np.float32)], axis=1), BF16)


def _pad_cols(w, n):
    return jnp.pad(w, ((0, 0), (0, n - w.shape[1])))


def _attention_layer(x, xb, w_in, w_out, w_cmp_k, w_cmp_v, pe_k, pe_v, g, b, bsz, seq):
    d = x.shape[1]
    n_cmp = (seq - CMP_BLOCK) // CMP_STRIDE + 1
    n_rows = seq // CMP_STRIDE
    w_heads = jnp.concatenate([w_in[:, :GATE_OFF], w_in[:, GATE_OFF + GATE_DIM:]], axis=1).astype(BF16)
    gate_cols = np.array([[GATE_OFF + c * NSA_HEADS + k * NSA_GROUP + gg for c in range(3) for gg in range(NSA_GROUP)]
                          for k in range(NSA_KV_HEADS)])
    w_gate = jnp.concatenate([_pad_cols(w_in[:, gate_cols[k]], HEAD_DIM) for k in range(NSA_KV_HEADS)], axis=1).astype(BF16)

    cos_t, sin_t = _rope_tables(jnp.arange(seq))
    h = _inproj(xb, w_heads, cos_t, sin_t, seq, tm=1024, tn=512)
    gates = _gate_proj(xb, w_gate, tm=1024)

    def blocks16(col0):
        part = h[:, col0 * HEAD_DIM:(col0 + NSA_KV_HEADS) * HEAD_DIM]
        part = part.reshape(bsz, n_rows, CMP_STRIDE, NSA_KV_HEADS, HEAD_DIM).transpose(0, 3, 1, 2, 4)
        return part.reshape(bsz, NSA_KV_HEADS, n_rows, CMP_STRIDE * HEAD_DIM)

    cmp_end = jnp.arange(n_rows) * CMP_STRIDE + CMP_BLOCK - 1
    cos_c, sin_c = _rope_tables(cmp_end)
    flat_pe = lambda pe: jnp.broadcast_to(pe.reshape(1, CMP_BLOCK * HEAD_DIM), (8, CMP_BLOCK * HEAD_DIM)).astype(BF16)
    flat_w = lambda w: w.reshape(CMP_BLOCK * HEAD_DIM, HEAD_DIM).astype(BF16)
    kc, vct = _compress(blocks16(COL_KC), blocks16(COL_VC), flat_w(w_cmp_k), flat_w(w_cmp_v),
                        flat_pe(pe_k), flat_pe(pe_v), cos_c, sin_c, n_cmp)

    o_a = _nsa(h, kc, vct, gates, _overlap_t(seq), _block_indicator(seq), bsz, seq, tq=256, tk=256)
    o_b = _stick_breaking(h, _later_sum_matrix(256), bsz, seq, tq=256, tk=256, heads=4)
    return _matmul_ln([o_a, o_b], w_out.astype(BF16), x, g, b, tm=512, tk=512, name="out_proj_ln")


def _ffn_layer(x, xb, w_up_all, conv_w, conv_b, w_down_all, layer, g, b, seq, *, tn=512, tk=1024):
    d_ff = w_down_all.shape[1]
    f_pad = -(-d_ff // tk) * tk
    assert f_pad % tn == 0
    split = lambda a: jnp.concatenate([_pad_cols(a[:, :d_ff], f_pad), _pad_cols(a[:, d_ff:], f_pad)], axis=1)
    w_up_p = _stage_ffn_weight(w_up_all, layer, d_ff, f_pad, axis=1)
    conv_w_p = split(conv_w)
    conv_b_p = split(conv_b.reshape(1, -1))
    w_down_p = _stage_ffn_weight(w_down_all, layer, d_ff, f_pad, axis=0)
    gated = _ffn_up(xb, w_up_p, conv_w_p, conv_b_p, seq, tm=1024, tn=tn, chunk=256)
    return _matmul_ln([gated], w_down_p, x, g, b, tm=512, tk=tk, name="ffn_down_ln")


def kernel(x, attn_w_in, attn_w_out, cmp_w_k, cmp_w_v, cmp_pe_k, cmp_pe_v, pool_w, pool_scale, ffn_w_up, ffn_conv_w,
           ffn_conv_b, ffn_w_down, ln_mix_g, ln_mix_b, ln_ffn_g, ln_ffn_b):
    bsz, seq, d = x.shape
    xf = x.reshape(bsz * seq, d)
    xb = xf.astype(BF16)
    vec = lambda a, layer: a[layer].reshape(1, d)
    for layer in range(DEPTH):
        i = layer // 2
        if layer % 2 == 0:
            xf, xb = _attention_layer(xf, xb, attn_w_in[i], attn_w_out[i], cmp_w_k[i], cmp_w_v[i], cmp_pe_k[i],
                                      cmp_pe_v[i], vec(ln_mix_g, layer), vec(ln_mix_b, layer), bsz, seq)
        else:
            xf, xb = _pool_mixer_ln(xf, pool_w[i].astype(BF16), pool_scale[i].reshape(1, d), vec(ln_mix_g, layer),
                                    vec(ln_mix_b, layer), seq, tm=256)
        xf, xb = _ffn_layer(xf, xb, ffn_w_up, ffn_conv_w[layer], ffn_conv_b[layer], ffn_w_down, layer,
                            vec(ln_ffn_g, layer), vec(ln_ffn_b, layer), seq)
    return xf.reshape(bsz, seq, d)
```

```python
import functools

import jax
import jax.numpy as jnp
import numpy as np
from jax import lax
from jax.experimental import pallas as pl
from jax.experimental.pallas import tpu as pltpu

F32 = jnp.float32
BF16 = jnp.bfloat16

HEAD_DIM = 128
NSA_HEADS = 16
NSA_KV_HEADS = 4
NSA_GROUP = 4
SB_HEADS = 16
CMP_BLOCK = 32
CMP_STRIDE = 16
SEL_BLOCK = 64
SEL_TOP_N = 16
WINDOW = 512
ROPE_THETA = 10000.0
POOL_WINDOWS = (2, 4, 8, 16)
CONV_WIDTH = 3
LN_EPS = 1e-5
NEG_INF = -1e30
FORCE_SCORE = 1e9
DEPTH = 2
DEEPNORM_ALPHA = (2 * DEPTH) ** 0.25
SCALE = HEAD_DIM ** -0.5
LOG2E = 1.4426950408889634

COL_Q_A, COL_KC, COL_VC, COL_KS, COL_VS, COL_KW, COL_VW, COL_Q_B, COL_K_B, COL_V_B = 0, 16, 20, 24, 28, 32, 36, 40, 56, 72
N_HEAD_COLS = 88
Q_NSA_DIM = NSA_HEADS * HEAD_DIM
KV_NSA_DIM = NSA_KV_HEADS * HEAD_DIM
GATE_DIM = 3 * NSA_HEADS
GATE_OFF = Q_NSA_DIM + 6 * KV_NSA_DIM

VMEM_LIMIT_BYTES = 56 * 1024 * 1024


def _params(sem, vmem=VMEM_LIMIT_BYTES):
    return pltpu.CompilerParams(dimension_semantics=sem, vmem_limit_bytes=vmem)


def _dot(a, b):
    return jnp.dot(a, b, preferred_element_type=F32)


def _dot_nt(a, b):
    return lax.dot_general(a, b, (((1,), (1,)), ((), ())), preferred_element_type=F32)


def _layer_norm_rows(y, g, b):
    mu = jnp.mean(y, axis=-1, keepdims=True)
    d = y - mu
    var = jnp.mean(d * d, axis=-1, keepdims=True)
    return d * lax.rsqrt(var + LN_EPS) * g + b


def _transpose_bf16(x):
    return x.astype(F32).T.astype(BF16)


def _inproj_kernel(a_ref, b_ref, cos_ref, sin_ref, o_ref, *, heads_per_tile, chunk):
    j = pl.program_id(1)

    def head_scales(head):
        is_rope = (head < COL_KC) | ((head >= COL_KS) & (head < COL_VS)) | ((head >= COL_KW) & (head < COL_VW))
        is_query = (head < COL_KC) | ((head >= COL_Q_B) & (head < COL_K_B))
        q_scale = jnp.where(is_query, SCALE * LOG2E, 1.0).astype(F32)
        return jnp.where(is_rope, q_scale, 0.0), jnp.where(is_rope, 0.0, q_scale)

    scales = [head_scales(j * heads_per_tile + hh) for hh in range(heads_per_tile)]
    for c in range(a_ref.shape[0] // chunk):
        rows = slice(c * chunk, (c + 1) * chunk)
        acc = _dot(a_ref[rows, :], b_ref[...])
        cos_c, sin_c = cos_ref[rows, :], sin_ref[rows, :]
        for hh, (rope_sel, pass_sel) in enumerate(scales):
            blk = acc[:, hh * HEAD_DIM:(hh + 1) * HEAD_DIM]
            rot = pltpu.roll(blk, HEAD_DIM // 2, axis=1)
            out = blk * (cos_c * rope_sel + pass_sel) + rot * (sin_c * rope_sel)
            o_ref[rows, hh * HEAD_DIM:(hh + 1) * HEAD_DIM] = out.astype(o_ref.dtype)


def _inproj(xb, w_heads, cos_t, sin_t, seq, *, tm, tn, chunk=256):
    m, kdim = xb.shape
    n = w_heads.shape[1]
    assert tn % HEAD_DIM == 0 and n % tn == 0 and seq % tm == 0 and tm % chunk == 0
    return pl.pallas_call(
        functools.partial(_inproj_kernel, heads_per_tile=tn // HEAD_DIM, chunk=chunk),
        grid=(m // tm, n // tn),
        in_specs=[
            pl.BlockSpec((tm, kdim), lambda i, j: (i, 0)),
            pl.BlockSpec((kdim, tn), lambda i, j: (0, j)),
            pl.BlockSpec((tm, HEAD_DIM), lambda i, j: (i % (seq // tm), 0)),
            pl.BlockSpec((tm, HEAD_DIM), lambda i, j: (i % (seq // tm), 0)),
        ],
        out_specs=pl.BlockSpec((tm, tn), lambda i, j: (i, j)),
        out_shape=jax.ShapeDtypeStruct((m, n), BF16),
        compiler_params=_params(("parallel", "arbitrary")),
        name="inproj_rope",
    )(xb, w_heads, cos_t, sin_t)


def _gate_kernel(x_ref, b_ref, o_ref, xb_ref):
    xb = x_ref[...].astype(BF16)
    xb_ref[...] = xb
    o_ref[...] = jax.nn.sigmoid(_dot(xb, b_ref[...]))


def _gate_proj(x, w_gate, *, tm):
    m, kdim = x.shape
    n = w_gate.shape[1]
    return pl.pallas_call(
        _gate_kernel,
        grid=(m // tm,),
        in_specs=[pl.BlockSpec((tm, kdim), lambda i: (i, 0)), pl.BlockSpec((kdim, n), lambda i: (0, 0))],
        out_specs=[pl.BlockSpec((tm, n), lambda i: (i, 0)), pl.BlockSpec((tm, kdim), lambda i: (i, 0))],
        out_shape=[jax.ShapeDtypeStruct((m, n), F32), jax.ShapeDtypeStruct((m, kdim), BF16)],
        compiler_params=_params(("parallel",)),
        name="gate_proj",
    )(x, w_gate)


def _compress_kernel(k2_ref, v2_ref, wk_ref, wv_ref, pek_ref, pev_ref, cos_ref, sin_ref, kc_ref, vct_ref, *, n_cmp):
    half = wk_ref.shape[0] // 2
    n_rows = k2_ref.shape[0]
    row = lax.broadcasted_iota(jnp.int32, (n_rows, HEAD_DIM), 0)

    def compress(x2_ref, w_ref, pe_ref):
        x2 = x2_ref[...]
        lo = _dot(x2, w_ref[:half, :])
        hi = _dot(x2, w_ref[half:, :])
        pe_term = _dot(pe_ref[...], w_ref[...])[0:1, :]
        return lo + pltpu.roll(hi, n_rows - 1, axis=0) + pe_term

    kc = compress(k2_ref, wk_ref, pek_ref)
    vc = compress(v2_ref, wv_ref, pev_ref)
    kc = kc * cos_ref[...] + pltpu.roll(kc, HEAD_DIM // 2, axis=1) * sin_ref[...]
    kc_ref[...] = jnp.where(row < n_cmp, kc, 0.0).astype(kc_ref.dtype)
    vct_ref[...] = jnp.where(row < n_cmp, vc, 0.0).T.astype(vct_ref.dtype)


def _compress(k2, v2, wk, wv, pek, pev, cos_c, sin_c, n_cmp):
    bsz, nkv, n_rows, wide = k2.shape
    assert n_rows == HEAD_DIM
    blk4 = pl.BlockSpec((None, None, n_rows, wide), lambda b, k: (b, k, 0, 0))
    full = lambda a: pl.BlockSpec(a.shape, lambda b, k: (0,) * a.ndim)
    out_blk = pl.BlockSpec((None, None, n_rows, HEAD_DIM), lambda b, k: (b, k, 0, 0))
    out_sds = jax.ShapeDtypeStruct((bsz, nkv, n_rows, HEAD_DIM), BF16)
    return pl.pallas_call(
        functools.partial(_compress_kernel, n_cmp=n_cmp),
        grid=(bsz, nkv),
        in_specs=[blk4, blk4, full(wk), full(wv), full(pek), full(pev), full(cos_c), full(sin_c)],
        out_specs=[out_blk, out_blk],
        out_shape=[out_sds, out_sds],
        compiler_params=_params(("parallel", "parallel")),
        name="nsa_compress",
    )(k2, v2, wk, wv, pek, pev, cos_c, sin_c)


def _nsa_kernel(q_ref, kc_ref, vct_ref, ks_ref, vs_ref, kw_ref, vw_ref, g_ref, ovl_ref, eind_ref, o_ref,
                vst_sc, vwt_sc, m_sc, l_sc, acc_sc, *, tq, tk, n_cmp, n_blk):
    qi = pl.program_id(2)
    t0 = qi * tq
    cols = NSA_GROUP * tq

    @pl.when(qi == 0)
    def _():
        for kt in range(vst_sc.shape[0]):
            vst_sc[kt] = _transpose_bf16(vs_ref[kt * tk:(kt + 1) * tk, :])
            vwt_sc[kt] = _transpose_bf16(vw_ref[kt * tk:(kt + 1) * tk, :])

    q = q_ref[...]
    q4 = jnp.concatenate([q[:, g * HEAD_DIM:(g + 1) * HEAD_DIM] for g in range(NSA_GROUP)], axis=0)
    key_minus_query = (lax.broadcasted_iota(jnp.int32, (tk, cols), 0)
                       - (lax.broadcasted_iota(jnp.int32, (tk, cols), 1) & (tq - 1)))

    n_rows = kc_ref.shape[0]
    row = lax.broadcasted_iota(jnp.int32, (n_rows, cols), 0)
    t_abs = t0 + (lax.broadcasted_iota(jnp.int32, (n_rows, cols), 1) & (tq - 1))
    s = _dot_nt(kc_ref[...], q4)
    valid = (row * CMP_STRIDE + (CMP_BLOCK - 1) <= t_abs) & (row < n_cmp)
    s = jnp.where(valid, s, NEG_INF)
    e = jnp.exp2(s - jnp.max(s, axis=0, keepdims=True))
    p = e * (1.0 / jnp.sum(e, axis=0, keepdims=True))
    p = jnp.where(t_abs >= CMP_BLOCK - 1, p, 0.0)
    o_cmp = _dot(vct_ref[...], p.astype(BF16))

    psum = p[:, 0:tq] + p[:, tq:2 * tq] + p[:, 2 * tq:3 * tq] + p[:, 3 * tq:4 * tq]
    imp = jnp.dot(ovl_ref[...], psum, precision=lax.Precision.HIGHEST, preferred_element_type=F32)[0:n_blk, :]
    jrow = lax.broadcasted_iota(jnp.int32, (n_blk, tq), 0)
    t_lane = t0 + lax.broadcasted_iota(jnp.int32, (n_blk, tq), 1)
    cur = t_lane >> 6
    forced = (jrow == 0) | (jrow == cur) | (jrow == cur - 1)
    imp = jnp.where(forced, FORCE_SCORE, jnp.where(jrow * SEL_BLOCK <= t_lane, imp, NEG_INF))
    rank = jnp.zeros((n_blk, tq), F32)
    for jp in range(n_blk):
        other = imp[jp:jp + 1, :]
        ahead = (other > imp) | ((other == imp) & (jrow > jp))
        rank = rank + jnp.where(ahead, 1.0, 0.0)
    bias_t = jnp.where(rank < SEL_TOP_N, 0.0, NEG_INF)
    bias_t = jnp.concatenate([bias_t, jnp.zeros((HEAD_DIM - n_blk, tq), F32)], axis=0)
    bias = bias_t.T.astype(BF16)
    q_aug = jnp.concatenate([q4, jnp.concatenate([bias] * NSA_GROUP, axis=0)], axis=1)

    def win_tile(back):
        kt = qi - back
        kt_c = jnp.maximum(kt, 0)
        sc = _dot_nt(kw_ref[pl.ds(pl.multiple_of(kt_c * tk, tk), tk), :], q4)
        hi, lo = back * tk, back * tk - WINDOW
        if lo >= -(tq - 1):
            lo_dyn = jnp.where(kt >= 0, lo, tk) if back else lo
            sc = jnp.where(key_minus_query > lo_dyn, sc, NEG_INF)
        elif back:
            sc = jnp.where(kt >= 0, sc, NEG_INF)
        if hi < tk - 1:
            sc = jnp.where(key_minus_query <= hi, sc, NEG_INF)
        return sc, vwt_sc[kt_c]

    w_tiles = [win_tile(back) for back in range(WINDOW // tk, -1, -1)]
    m_w = functools.reduce(jnp.maximum, [jnp.max(sc, axis=0, keepdims=True) for sc, _ in w_tiles])
    l_w = jnp.zeros_like(m_w)
    o_win = jnp.zeros((HEAD_DIM, cols), F32)
    for sc, vt in w_tiles:
        pr = jnp.exp2(sc - m_w)
        l_w = l_w + jnp.sum(pr, axis=0, keepdims=True)
        o_win = o_win + _dot(vt, pr.astype(BF16))
    o_win = o_win * (1.0 / l_w)

    def slc_scores(kt):
        s0 = pl.multiple_of(kt * tk, tk)
        k_aug = jnp.concatenate([ks_ref[pl.ds(s0, tk), :], eind_ref[pl.ds(s0, tk), :]], axis=1)
        return _dot_nt(k_aug, q_aug)

    def online_step(sc, vt):
        m_old = m_sc[...]
        m_new = jnp.maximum(m_old, jnp.max(sc, axis=0, keepdims=True))
        pr = jnp.exp2(sc - m_new)
        corr = jnp.exp2(m_old - m_new)
        l_sc[...] = corr * l_sc[...] + jnp.sum(pr, axis=0, keepdims=True)
        acc_sc[...] = corr * acc_sc[...] + _dot(vt, pr.astype(BF16))
        m_sc[...] = m_new

    m_sc[...] = jnp.full(m_sc.shape, NEG_INF, F32)
    l_sc[...] = jnp.zeros(l_sc.shape, F32)
    acc_sc[...] = jnp.zeros(acc_sc.shape, F32)

    def slc_body(kt, carry):
        online_step(slc_scores(kt), vst_sc[kt])
        return carry

    lax.fori_loop(0, qi, slc_body, 0)
    online_step(jnp.where(key_minus_query <= 0, slc_scores(qi), NEG_INF), vst_sc[qi])
    o_slc = acc_sc[...] * (1.0 / l_sc[...])

    gates_t = g_ref[...].T
    for g in range(NSA_GROUP):
        sl = slice(g * tq, (g + 1) * tq)
        o = (gates_t[g:g + 1, :] * o_cmp[:, sl]
             + gates_t[NSA_GROUP + g:NSA_GROUP + g + 1, :] * o_slc[:, sl]
             + gates_t[2 * NSA_GROUP + g:2 * NSA_GROUP + g + 1, :] * o_win[:, sl])
        o_ref[:, g * HEAD_DIM:(g + 1) * HEAD_DIM] = o.T.astype(o_ref.dtype)


def _nsa(h, kc, vct, gates, ovl, eind, bsz, seq, *, tq, tk):
    assert tq == tk and seq % tq == 0 and WINDOW % tk == 0
    m = h.shape[0]
    nq = seq // tq
    n_cmp = (seq - CMP_BLOCK) // CMP_STRIDE + 1
    n_blk = seq // SEL_BLOCK
    qw = NSA_GROUP * HEAD_DIM
    cols = NSA_GROUP * tq
    kv_spec = lambda col0: pl.BlockSpec((seq, HEAD_DIM), lambda b, k, i: (b, col0 + k))
    cmp_spec = pl.BlockSpec((None, None, kc.shape[2], HEAD_DIM), lambda b, k, i: (b, k, 0, 0))
    return pl.pallas_call(
        functools.partial(_nsa_kernel, tq=tq, tk=tk, n_cmp=n_cmp, n_blk=n_blk),
        grid=(bsz, NSA_KV_HEADS, nq),
        in_specs=[
            pl.BlockSpec((tq, qw), lambda b, k, i: (b * nq + i, k)),
            cmp_spec, cmp_spec,
            kv_spec(COL_KS), kv_spec(COL_VS), kv_spec(COL_KW), kv_spec(COL_VW),
            pl.BlockSpec((tq, HEAD_DIM), lambda b, k, i: (b * nq + i, k)),
            pl.BlockSpec(ovl.shape, lambda b, k, i: (0, 0)),
            pl.BlockSpec(eind.shape, lambda b, k, i: (0, 0)),
        ],
        out_specs=pl.BlockSpec((tq, qw), lambda b, k, i: (b * nq + i, k)),
        out_shape=jax.ShapeDtypeStruct((m, Q_NSA_DIM), BF16),
        scratch_shapes=[pltpu.VMEM((seq // tk, HEAD_DIM, tk), BF16), pltpu.VMEM((seq // tk, HEAD_DIM, tk), BF16),
                        pltpu.VMEM((1, cols), F32), pltpu.VMEM((1, cols), F32), pltpu.VMEM((HEAD_DIM, cols), F32)],
        compiler_params=_params(("arbitrary", "arbitrary", "arbitrary")),
        name="nsa_attention",
    )(h, kc, vct, h, h, h, h, gates, ovl, eind)


def _sb_kernel(q_ref, k_ref, v_ref, u_ref, o_ref, carry_sc, acc_sc, *, tq, tk, heads):
    qi = pl.program_id(2)
    query_minus_key = lax.broadcasted_iota(jnp.int32, (tq, tk), 0) - lax.broadcasted_iota(jnp.int32, (tq, tk), 1)
    carry_sc[...] = jnp.zeros(carry_sc.shape, F32)
    acc_sc[...] = jnp.zeros(acc_sc.shape, F32)
    u = u_ref[...]

    def tile(kt, diagonal):
        s0 = pl.multiple_of(kt * tk, tk)
        for hh in range(heads):
            cs = slice(hh * HEAD_DIM, (hh + 1) * HEAD_DIM)
            z = _dot_nt(q_ref[:, cs], k_ref[pl.ds(s0, tk), cs])
            sp = jnp.maximum(z, 0.0) + jnp.log2(1.0 + jnp.exp2(-jnp.abs(z)))
            if diagonal:
                mask = query_minus_key > 0
                sp = jnp.where(mask, sp, 0.0)
            sp_b = sp.astype(BF16)
            sums = _dot(sp_b, u)
            carry = carry_sc[hh]
            later = sums + jnp.concatenate([carry] * (tk // HEAD_DIM), axis=1)
            a = jnp.exp2(z - sp - later)
            if diagonal:
                a = jnp.where(mask, a, 0.0)
            acc_sc[hh] += _dot(a.astype(BF16), v_ref[pl.ds(s0, tk), cs])
            row_total = sums[:, 0:1] + sp_b[:, 0:1].astype(F32)
            carry_sc[hh] = carry + jnp.broadcast_to(row_total, carry.shape)

    tile(qi, True)

    def body(it, c):
        tile(qi - 1 - it, False)
        return c

    lax.fori_loop(0, qi, body, 0)
    for hh in range(heads):
        o_ref[:, hh * HEAD_DIM:(hh + 1) * HEAD_DIM] = acc_sc[hh].astype(o_ref.dtype)


def _stick_breaking(h, u_tri, bsz, seq, *, tq, tk, heads):
    assert seq % tq == 0 and tq == tk and u_tri.shape == (tk, tk)
    assert SB_HEADS % heads == 0 and COL_Q_B % heads == 0 and COL_K_B % heads == 0 and COL_V_B % heads == 0
    m = h.shape[0]
    nq = seq // tq
    wide = heads * HEAD_DIM
    return pl.pallas_call(
        functools.partial(_sb_kernel, tq=tq, tk=tk, heads=heads),
        grid=(bsz, SB_HEADS // heads, nq),
        in_specs=[
            pl.BlockSpec((tq, wide), lambda b, hg, i: (b * nq + i, COL_Q_B // heads + hg)),
            pl.BlockSpec((seq, wide), lambda b, hg, i: (b, COL_K_B // heads + hg)),
            pl.BlockSpec((seq, wide), lambda b, hg, i: (b, COL_V_B // heads + hg)),
            pl.BlockSpec(u_tri.shape, lambda b, hg, i: (0, 0)),
        ],
        out_specs=pl.BlockSpec((tq, wide), lambda b, hg, i: (b * nq + i, hg)),
        out_shape=jax.ShapeDtypeStruct((m, SB_HEADS * HEAD_DIM), BF16),
        scratch_shapes=[pltpu.VMEM((heads, tq, HEAD_DIM), F32), pltpu.VMEM((heads, tq, HEAD_DIM), F32)],
        compiler_params=_params(("parallel", "parallel", "arbitrary")),
        name="stick_breaking",
    )(h, h, h, u_tri)


def _ln_epilogue(y, x_ref, g_ref, b_ref, of_ref, ob_ref, rows=slice(None)):
    out = _layer_norm_rows(DEEPNORM_ALPHA * x_ref[rows, :] + y, g_ref[...], b_ref[...])
    of_ref[rows, :] = out
    ob_ref[rows, :] = out.astype(ob_ref.dtype)


def _mm_ln_kernel(*refs, k_splits, chunk):
    n_a = len(k_splits) - 1
    a_refs = refs[:n_a]
    w_ref, x_ref, g_ref, b_ref, of_ref, ob_ref, acc_sc = refs[n_a:]
    k = pl.program_id(1)
    nk = k_splits[-1]

    for s, a_ref in enumerate(a_refs):
        lo, hi = k_splits[s], k_splits[s + 1]
        first = lo == 0
        last = hi == nk

        if first:
            @pl.when(k == 0)
            def _(a_ref=a_ref):
                acc_sc[...] = _dot(a_ref[...], w_ref[...])

        lo_acc, hi_acc = lo + (1 if first else 0), hi - (1 if last else 0)
        if hi_acc > lo_acc:
            @pl.when((k >= lo_acc) & (k < hi_acc))
            def _(a_ref=a_ref):
                acc_sc[...] += _dot(a_ref[...], w_ref[...])

        if last:
            @pl.when(k == nk - 1)
            def _(a_ref=a_ref):
                for c in range(acc_sc.shape[0] // chunk):
                    rows = slice(c * chunk, (c + 1) * chunk)
                    y = acc_sc[rows, :] + _dot(a_ref[rows, :], w_ref[...])
                    _ln_epilogue(y, x_ref, g_ref, b_ref, of_ref, ob_ref, rows)


def _row_resident_specs(tm, n):
    row = pl.BlockSpec((tm, n), lambda i, k: (i, 0), pipeline_mode=pl.Buffered(1))
    vec = pl.BlockSpec((1, n), lambda i, k: (0, 0))
    return row, vec


def _matmul_ln(a_list, w, x, g, b, *, tm, tk, name, chunk=128):
    m, n = x.shape
    kdim = w.shape[0]
    nk = kdim // tk
    k_splits = [0]
    for a in a_list:
        assert a.shape[1] % tk == 0
        k_splits.append(k_splits[-1] + a.shape[1] // tk)
    assert k_splits[-1] == nk and nk >= 2 and tm % chunk == 0
    row, vec = _row_resident_specs(tm, n)

    def a_spec(lo, hi):
        return pl.BlockSpec((tm, tk), lambda i, k: (i, jnp.clip(k - lo, 0, hi - lo - 1)))

    return pl.pallas_call(
        functools.partial(_mm_ln_kernel, k_splits=tuple(k_splits), chunk=chunk),
        grid=(m // tm, nk),
        in_specs=[a_spec(k_splits[s], k_splits[s + 1]) for s in range(len(a_list))]
        + [pl.BlockSpec((tk, n), lambda i, k: (k, 0)), row, vec, vec],
        out_specs=[row, row],
        out_shape=[jax.ShapeDtypeStruct((m, n), F32), jax.ShapeDtypeStruct((m, n), BF16)],
        scratch_shapes=[pltpu.VMEM((tm, n), F32)],
        compiler_params=_params(("parallel", "arbitrary")),
        name=name,
    )(*a_list, w, x, g, b)


def _stage_tiles_kernel(x_ref, o_ref, *, tiles_in, tiles_out):
    valid = pl.program_id(0) % tiles_out < tiles_in
    o_ref[...] = jnp.where(valid, x_ref[...], 0.0).astype(o_ref.dtype)


def _stage_ffn_weight(w_all, layer, d_ff, f_pad, axis, *, tile=256):
    assert d_ff % tile == 0 and f_pad % tile == 0
    tiles_in, tiles_out = d_ff // tile, f_pad // tile
    shape = list(w_all.shape[1:])
    sections = shape[axis] // d_ff
    src = lambda t: (t // tiles_out) * tiles_in + jnp.minimum(t % tiles_out, tiles_in - 1)
    if axis == 0:
        blk, out_shape = (tile, shape[1]), (sections * f_pad, shape[1])
        in_spec = pl.BlockSpec((None,) + blk, lambda t: (layer, src(t), 0))
        out_spec = pl.BlockSpec(blk, lambda t: (t, 0))
    else:
        blk, out_shape = (shape[0], tile), (shape[0], sections * f_pad)
        in_spec = pl.BlockSpec((None,) + blk, lambda t: (layer, 0, src(t)))
        out_spec = pl.BlockSpec(blk, lambda t: (0, t))
    return pl.pallas_call(
        functools.partial(_stage_tiles_kernel, tiles_in=tiles_in, tiles_out=tiles_out),
        grid=(sections * tiles_out,),
        in_specs=[in_spec],
        out_specs=out_spec,
        out_shape=jax.ShapeDtypeStruct(out_shape, BF16),
        compiler_params=_params(("parallel",)),
        name="stage_ffn_weight",
    )(w_all)


def _ffn_up_kernel(a_ref, halo_ref, wg_ref, wv_ref, cwg_ref, cwv_ref, cbg_ref, cbv_ref, o_ref, head_sc, hg_sc, hv_sc, *, tm, chunk, halo, tiles_per_seq):
    i = pl.program_id(0)

    @pl.when(pl.program_id(1) == 0)
    def _():
        seq_start = i % tiles_per_seq == 0
        head_sc[0:halo, :] = jnp.where(seq_start, jnp.zeros_like(halo_ref[...]), halo_ref[...])
        head_sc[halo:, :] = a_ref[...]

    def conv(h_sc, cw_ref, cb_ref):
        cw = cw_ref[...]
        out = cb_ref[...]
        for kk in range(CONV_WIDTH):
            out = out + h_sc[pl.ds(halo - (CONV_WIDTH - 1) + kk, chunk), :] * cw[kk:kk + 1, :]
        return out

    for c in range(tm // chunk):
        p = c % 2
        rows = slice(c * chunk, (c + 1) * chunk)
        if c == 0:
            lhs = head_sc[0:halo + chunk, :]
            hg_sc[p] = _dot(lhs, wg_ref[...])
            hv_sc[p] = _dot(lhs, wv_ref[...])
        else:
            lhs = head_sc[halo + c * chunk:halo + (c + 1) * chunk, :]
            hg_sc[p, 0:halo, :] = hg_sc[1 - p, chunk:, :]
            hv_sc[p, 0:halo, :] = hv_sc[1 - p, chunk:, :]
            hg_sc[p, halo:, :] = _dot(lhs, wg_ref[...])
            hv_sc[p, halo:, :] = _dot(lhs, wv_ref[...])
        gate = conv(hg_sc.at[p], cwg_ref, cbg_ref)
        val = conv(hv_sc.at[p], cwv_ref, cbv_ref)
        o_ref[rows, :] = (gate * jax.nn.sigmoid(gate) * val).astype(o_ref.dtype)


def _ffn_up(xb, w_up, conv_w, conv_b, seq, *, tm, tn, chunk, halo=16):
    m, kdim = xb.shape
    f_pad = w_up.shape[1] // 2
    nj = f_pad // tn
    assert seq % tm == 0 and tm % chunk == 0 and chunk % halo == 0
    hb = tm // halo
    return pl.pallas_call(
        functools.partial(_ffn_up_kernel, tm=tm, chunk=chunk, halo=halo, tiles_per_seq=seq // tm),
        grid=(m // tm, nj),
        in_specs=[
            pl.BlockSpec((tm, kdim), lambda i, j: (i, 0)),
            pl.BlockSpec((halo, kdim), lambda i, j: (jnp.maximum(i * hb - 1, 0), 0)),
            pl.BlockSpec((kdim, tn), lambda i, j: (0, j)),
            pl.BlockSpec((kdim, tn), lambda i, j: (0, nj + j)),
            pl.BlockSpec((CONV_WIDTH, tn), lambda i, j: (0, j)),
            pl.BlockSpec((CONV_WIDTH, tn), lambda i, j: (0, nj + j)),
            pl.BlockSpec((1, tn), lambda i, j: (0, j)),
            pl.BlockSpec((1, tn), lambda i, j: (0, nj + j)),
        ],
        out_specs=pl.BlockSpec((tm, tn), lambda i, j: (i, j)),
        out_shape=jax.ShapeDtypeStruct((m, f_pad), BF16),
        scratch_shapes=[pltpu.VMEM((halo + tm, kdim), BF16), pltpu.VMEM((2, halo + chunk, tn), F32),
                        pltpu.VMEM((2, halo + chunk, tn), F32)],
        compiler_params=_params(("parallel", "arbitrary")),
        name="ffn_up_conv_gate",
    )(xb, xb, w_up, w_up, conv_w, conv_w, conv_b, conv_b)


def _pool_kernel(x_ref, halo_ref, w_ref, ps_ref, g_ref, b_ref, of_ref, ob_ref, d_sc, *, tm, halo, tiles_per_seq, group_dim):
    i = pl.program_id(0)
    keep = jnp.where(i % tiles_per_seq == 0, 0.0, 1.0)
    t_in_seq = (i % tiles_per_seq) * tm + lax.broadcasted_iota(jnp.int32, (tm, group_dim), 0)
    for gi, win in enumerate(POOL_WINDOWS):
        cs = slice(gi * group_dim, (gi + 1) * group_dim)
        ext = jnp.concatenate([halo_ref[:, cs] * keep, x_ref[:, cs]], axis=0)
        acc = ext[halo:halo + tm, :]
        for back in range(1, win):
            acc = acc + ext[halo - back:halo - back + tm, :]
        count = jnp.minimum(t_in_seq + 1, win).astype(F32)
        d = acc / count - ext[halo:halo + tm, :]
        d_sc[:, cs] = _dot(d.astype(BF16), w_ref[gi])
    y = d_sc[...] * ps_ref[...]
    _ln_epilogue(y, x_ref, g_ref, b_ref, of_ref, ob_ref)


def _pool_mixer_ln(x, w_pool, pool_scale, g, b, seq, *, tm, halo=16):
    m, n = x.shape
    ngrp, group_dim = w_pool.shape[0], w_pool.shape[1]
    assert seq % tm == 0 and tm % halo == 0 and max(POOL_WINDOWS) <= halo
    hb = tm // halo
    row = pl.BlockSpec((tm, n), lambda i: (i, 0))
    vec = pl.BlockSpec((1, n), lambda i: (0, 0))
    return pl.pallas_call(
        functools.partial(_pool_kernel, tm=tm, halo=halo, tiles_per_seq=seq // tm, group_dim=group_dim),
        grid=(m // tm,),
        in_specs=[row, pl.BlockSpec((halo, n), lambda i: (jnp.maximum(i * hb - 1, 0), 0)),
                  pl.BlockSpec(w_pool.shape, lambda i: (0, 0, 0), pipeline_mode=pl.Buffered(1)), vec, vec, vec],
        out_specs=[row, row],
        out_shape=[jax.ShapeDtypeStruct((m, n), F32), jax.ShapeDtypeStruct((m, n), BF16)],
        scratch_shapes=[pltpu.VMEM((tm, n), F32)],
        compiler_params=_params(("parallel",)),
        name="pool_mixer_ln",
    )(x, x, w_pool, pool_scale, g, b)


def _rope_tables(pos):
    inv_freq = 1.0 / (ROPE_THETA ** (jnp.arange(0, HEAD_DIM, 2, dtype=F32) / HEAD_DIM))
    ang = pos.astype(F32)[:, None] * inv_freq[None, :]
    cos, sin = jnp.cos(ang), jnp.sin(ang)
    return jnp.concatenate([cos, cos], axis=-1), jnp.concatenate([-sin, sin], axis=-1)


def _overlap_t(seq):
    n_cmp = (seq - CMP_BLOCK) // CMP_STRIDE + 1
    n_blk = seq // SEL_BLOCK
    c_start = np.arange(n_cmp)[:, None] * CMP_STRIDE
    b_start = np.arange(n_blk)[None, :] * SEL_BLOCK
    ov = np.clip(np.minimum(c_start + CMP_BLOCK, b_start + SEL_BLOCK) - np.maximum(c_start, b_start), 0, None) / CMP_BLOCK
    out = np.zeros((HEAD_DIM, HEAD_DIM), np.float32)
    out[:n_blk, :n_cmp] = ov.T
    return jnp.asarray(out)


def _block_indicator(seq):
    out = np.zeros((seq, HEAD_DIM), np.float32)
    out[np.arange(seq), np.arange(seq) // SEL_BLOCK] = 1.0
    return jnp.asarray(out, BF16)


def _later_sum_matrix(n):
    return jnp.asarray(np.tril(np.ones((n, n), np.float32), -1), BF16)


def _pad_cols(w, n):
    return jnp.pad(w, ((0, 0), (0, n - w.shape[1])))


def _attention_layer(x, w_in, w_out, w_cmp_k, w_cmp_v, pe_k, pe_v, g, b, bsz, seq):
    d = x.shape[1]
    n_cmp = (seq - CMP_BLOCK) // CMP_STRIDE + 1
    n_rows = seq // CMP_STRIDE
    w_heads = jnp.concatenate([w_in[:, :GATE_OFF], w_in[:, GATE_OFF + GATE_DIM:]], axis=1).astype(BF16)
    gate_cols = np.array([[GATE_OFF + c * NSA_HEADS + k * NSA_GROUP + gg for c in range(3) for gg in range(NSA_GROUP)]
                          for k in range(NSA_KV_HEADS)])
    w_gate = jnp.concatenate([_pad_cols(w_in[:, gate_cols[k]], HEAD_DIM) for k in range(NSA_KV_HEADS)], axis=1).astype(BF16)

    cos_t, sin_t = _rope_tables(jnp.arange(seq))
    gates, xb = _gate_proj(x, w_gate, tm=512)
    h = _inproj(xb, w_heads, cos_t, sin_t, seq, tm=1024, tn=1024)

    def blocks16(col0):
        part = h[:, col0 * HEAD_DIM:(col0 + NSA_KV_HEADS) * HEAD_DIM]
        part = part.reshape(bsz, n_rows, CMP_STRIDE, NSA_KV_HEADS, HEAD_DIM).transpose(0, 3, 1, 2, 4)
        return part.reshape(bsz, NSA_KV_HEADS, n_rows, CMP_STRIDE * HEAD_DIM)

    cmp_end = jnp.arange(n_rows) * CMP_STRIDE + CMP_BLOCK - 1
    cos_c, sin_c = _rope_tables(cmp_end)
    flat_pe = lambda pe: jnp.broadcast_to(pe.reshape(1, CMP_BLOCK * HEAD_DIM), (8, CMP_BLOCK * HEAD_DIM)).astype(BF16)
    flat_w = lambda w: w.reshape(CMP_BLOCK * HEAD_DIM, HEAD_DIM).astype(BF16)
    kc, vct = _compress(blocks16(COL_KC), blocks16(COL_VC), flat_w(w_cmp_k), flat_w(w_cmp_v),
                        flat_pe(pe_k), flat_pe(pe_v), cos_c, sin_c, n_cmp)

    o_a = _nsa(h, kc, vct, gates, _overlap_t(seq), _block_indicator(seq), bsz, seq, tq=256, tk=256)
    o_b = _stick_breaking(h, _later_sum_matrix(256), bsz, seq, tq=256, tk=256, heads=8)
    return _matmul_ln([o_a, o_b], w_out.astype(BF16), x, g, b, tm=512, tk=512, name="out_proj_ln")


def _ffn_layer(x, xb, w_up_all, conv_w, conv_b, w_down_all, layer, g, b, seq, *, tn=512, tk=1024):
    d_ff = w_down_all.shape[1]
    f_pad = -(-d_ff // tk) * tk
    assert f_pad % tn == 0
    split = lambda a: jnp.concatenate([_pad_cols(a[:, :d_ff], f_pad), _pad_cols(a[:, d_ff:], f_pad)], axis=1)
    w_up_p = _stage_ffn_weight(w_up_all, layer, d_ff, f_pad, axis=1)
    conv_w_p = split(conv_w)
    conv_b_p = split(conv_b.reshape(1, -1))
    w_down_p = _stage_ffn_weight(w_down_all, layer, d_ff, f_pad, axis=0)
    gated = _ffn_up(xb, w_up_p, conv_w_p, conv_b_p, seq, tm=1024, tn=tn, chunk=256)
    return _matmul_ln([gated], w_down_p, x, g, b, tm=512, tk=tk, name="ffn_down_ln")


def kernel(x, attn_w_in, attn_w_out, cmp_w_k, cmp_w_v, cmp_pe_k, cmp_pe_v, pool_w, pool_scale, ffn_w_up, ffn_conv_w,
           ffn_conv_b, ffn_w_down, ln_mix_g, ln_mix_b, ln_ffn_g, ln_ffn_b):
    bsz, seq, d = x.shape
    xf = x.reshape(bsz * seq, d)
    xb = None
    vec = lambda a, layer: a[layer].reshape(1, d)
    for layer in range(DEPTH):
        i = layer // 2
        if layer % 2 == 0:
            xf, xb = _attention_layer(xf, attn_w_in[i], attn_w_out[i], cmp_w_k[i], cmp_w_v[i], cmp_pe_k[i],
                                      cmp_pe_v[i], vec(ln_mix_g, layer), vec(ln_mix_b, layer), bsz, seq)
        else:
            xf, xb = _pool_mixer_ln(xf, pool_w[i].astype(BF16), pool_scale[i].reshape(1, d), vec(ln_mix_g, layer),
                                    vec(ln_mix_b, layer), seq, tm=256)
        xf, xb = _ffn_layer(xf, xb, ffn_w_up, ffn_conv_w[layer], ffn_conv_b[layer], ffn_w_down, layer,
                            vec(ln_ffn_g, layer), vec(ln_ffn_b, layer), seq)
    return xf.reshape(bsz, seq, d)
```

```python
import functools

import jax
import jax.numpy as jnp
import numpy as np
from jax import lax
from jax.experimental import pallas as pl
from jax.experimental.pallas import tpu as pltpu

F32 = jnp.float32
BF16 = jnp.bfloat16

HEAD_DIM = 128
NSA_HEADS = 16
NSA_KV_HEADS = 4
NSA_GROUP = 4
SB_HEADS = 16
CMP_BLOCK = 32
CMP_STRIDE = 16
SEL_BLOCK = 64
SEL_TOP_N = 16
WINDOW = 512
ROPE_THETA = 10000.0
POOL_WINDOWS = (2, 4, 8, 16)
CONV_WIDTH = 3
LN_EPS = 1e-5
NEG_INF = -1e30
FORCE_SCORE = 1e9
DEPTH = 2
DEEPNORM_ALPHA = (2 * DEPTH) ** 0.25
SCALE = HEAD_DIM ** -0.5
LOG2E = 1.4426950408889634

COL_Q_A, COL_KC, COL_VC, COL_KS, COL_VS, COL_KW, COL_VW, COL_Q_B, COL_K_B, COL_V_B = 0, 16, 20, 24, 28, 32, 36, 40, 56, 72
N_HEAD_COLS = 88
Q_NSA_DIM = NSA_HEADS * HEAD_DIM
KV_NSA_DIM = NSA_KV_HEADS * HEAD_DIM
GATE_DIM = 3 * NSA_HEADS
GATE_OFF = Q_NSA_DIM + 6 * KV_NSA_DIM

VMEM_LIMIT_BYTES = 56 * 1024 * 1024


def _params(sem, vmem=VMEM_LIMIT_BYTES):
    return pltpu.CompilerParams(dimension_semantics=sem, vmem_limit_bytes=vmem)


def _dot(a, b):
    return jnp.dot(a, b, preferred_element_type=F32)


def _dot_nt(a, b):
    return lax.dot_general(a, b, (((1,), (1,)), ((), ())), preferred_element_type=F32)


def _layer_norm_rows(y, g, b):
    mu = jnp.mean(y, axis=-1, keepdims=True)
    d = y - mu
    var = jnp.mean(d * d, axis=-1, keepdims=True)
    return d * lax.rsqrt(var + LN_EPS) * g + b


def _transpose_bf16(x):
    return x.astype(F32).T.astype(BF16)


def _inproj_kernel(a_ref, b_ref, cos_ref, sin_ref, o_ref, *, heads_per_tile, chunk):
    j = pl.program_id(1)

    def head_scales(head):
        is_rope = (head < COL_KC) | ((head >= COL_KS) & (head < COL_VS)) | ((head >= COL_KW) & (head < COL_VW))
        is_query = (head < COL_KC) | ((head >= COL_Q_B) & (head < COL_K_B))
        q_scale = jnp.where(is_query, SCALE * LOG2E, 1.0).astype(F32)
        return jnp.where(is_rope, q_scale, 0.0), jnp.where(is_rope, 0.0, q_scale)

    scales = [head_scales(j * heads_per_tile + hh) for hh in range(heads_per_tile)]
    for c in range(a_ref.shape[0] // chunk):
        rows = slice(c * chunk, (c + 1) * chunk)
        acc = _dot(a_ref[rows, :], b_ref[...])
        cos_c, sin_c = cos_ref[rows, :], sin_ref[rows, :]
        for hh, (rope_sel, pass_sel) in enumerate(scales):
            blk = acc[:, hh * HEAD_DIM:(hh + 1) * HEAD_DIM]
            rot = pltpu.roll(blk, HEAD_DIM // 2, axis=1)
            out = blk * (cos_c * rope_sel + pass_sel) + rot * (sin_c * rope_sel)
            o_ref[rows, hh * HEAD_DIM:(hh + 1) * HEAD_DIM] = out.astype(o_ref.dtype)


def _inproj(xb, w_heads, cos_t, sin_t, seq, *, tm, tn, chunk=256):
    m, kdim = xb.shape
    n = w_heads.shape[1]
    assert tn % HEAD_DIM == 0 and n % tn == 0 and seq % tm == 0 and tm % chunk == 0
    return pl.pallas_call(
        functools.partial(_inproj_kernel, heads_per_tile=tn // HEAD_DIM, chunk=chunk),
        grid=(m // tm, n // tn),
        in_specs=[
            pl.BlockSpec((tm, kdim), lambda i, j: (i, 0)),
            pl.BlockSpec((kdim, tn), lambda i, j: (0, j)),
            pl.BlockSpec((tm, HEAD_DIM), lambda i, j: (i % (seq // tm), 0)),
            pl.BlockSpec((tm, HEAD_DIM), lambda i, j: (i % (seq // tm), 0)),
        ],
        out_specs=pl.BlockSpec((tm, tn), lambda i, j: (i, j)),
        out_shape=jax.ShapeDtypeStruct((m, n), BF16),
        compiler_params=_params(("parallel", "arbitrary")),
        name="inproj_rope",
    )(xb, w_heads, cos_t, sin_t)


def _gate_kernel(x_ref, b_ref, o_ref, xb_ref):
    xb = x_ref[...].astype(BF16)
    xb_ref[...] = xb
    o_ref[...] = jax.nn.sigmoid(_dot(xb, b_ref[...]))


def _gate_proj(x, w_gate, *, tm):
    m, kdim = x.shape
    n = w_gate.shape[1]
    return pl.pallas_call(
        _gate_kernel,
        grid=(m // tm,),
        in_specs=[pl.BlockSpec((tm, kdim), lambda i: (i, 0)), pl.BlockSpec((kdim, n), lambda i: (0, 0))],
        out_specs=[pl.BlockSpec((tm, n), lambda i: (i, 0)), pl.BlockSpec((tm, kdim), lambda i: (i, 0))],
        out_shape=[jax.ShapeDtypeStruct((m, n), F32), jax.ShapeDtypeStruct((m, kdim), BF16)],
        compiler_params=_params(("parallel",)),
        name="gate_proj",
    )(x, w_gate)


def _compress_kernel(k2_ref, v2_ref, wk_ref, wv_ref, pek_ref, pev_ref, cos_ref, sin_ref, kc_ref, vct_ref, *, n_cmp):
    half = wk_ref.shape[0] // 2
    n_rows = k2_ref.shape[0]
    row = lax.broadcasted_iota(jnp.int32, (n_rows, HEAD_DIM), 0)

    def compress(x2_ref, w_ref, pe_ref):
        x2 = x2_ref[...]
        lo = _dot(x2, w_ref[:half, :])
        hi = _dot(x2, w_ref[half:, :])
        pe_term = _dot(pe_ref[...], w_ref[...])[0:1, :]
        return lo + pltpu.roll(hi, n_rows - 1, axis=0) + pe_term

    kc = compress(k2_ref, wk_ref, pek_ref)
    vc = compress(v2_ref, wv_ref, pev_ref)
    kc = kc * cos_ref[...] + pltpu.roll(kc, HEAD_DIM // 2, axis=1) * sin_ref[...]
    kc_ref[...] = jnp.where(row < n_cmp, kc, 0.0).astype(kc_ref.dtype)
    vct_ref[...] = jnp.where(row < n_cmp, vc, 0.0).T.astype(vct_ref.dtype)


def _compress(k2, v2, wk, wv, pek, pev, cos_c, sin_c, n_cmp):
    bsz, nkv, n_rows, wide = k2.shape
    assert n_rows == HEAD_DIM
    blk4 = pl.BlockSpec((None, None, n_rows, wide), lambda b, k: (b, k, 0, 0))
    full = lambda a: pl.BlockSpec(a.shape, lambda b, k: (0,) * a.ndim)
    out_blk = pl.BlockSpec((None, None, n_rows, HEAD_DIM), lambda b, k: (b, k, 0, 0))
    out_sds = jax.ShapeDtypeStruct((bsz, nkv, n_rows, HEAD_DIM), BF16)
    return pl.pallas_call(
        functools.partial(_compress_kernel, n_cmp=n_cmp),
        grid=(bsz, nkv),
        in_specs=[blk4, blk4, full(wk), full(wv), full(pek), full(pev), full(cos_c), full(sin_c)],
        out_specs=[out_blk, out_blk],
        out_shape=[out_sds, out_sds],
        compiler_params=_params(("parallel", "parallel")),
        name="nsa_compress",
    )(k2, v2, wk, wv, pek, pev, cos_c, sin_c)


def _nsa_kernel(q_ref, kc_ref, vct_ref, ks_ref, vs_ref, kw_ref, vw_ref, g_ref, ovl_ref, eind_ref, o_ref,
                vst_sc, vwt_sc, m_sc, l_sc, acc_sc, *, tq, tk, n_cmp, n_blk):
    qi = pl.program_id(2)
    t0 = qi * tq
    cols = NSA_GROUP * tq

    @pl.when(qi == 0)
    def _():
        for kt in range(vst_sc.shape[0]):
            vst_sc[kt] = _transpose_bf16(vs_ref[kt * tk:(kt + 1) * tk, :])
            vwt_sc[kt] = _transpose_bf16(vw_ref[kt * tk:(kt + 1) * tk, :])

    q = q_ref[...]
    q4 = jnp.concatenate([q[:, g * HEAD_DIM:(g + 1) * HEAD_DIM] for g in range(NSA_GROUP)], axis=0)
    key_minus_query = (lax.broadcasted_iota(jnp.int32, (tk, cols), 0)
                       - (lax.broadcasted_iota(jnp.int32, (tk, cols), 1) & (tq - 1)))

    n_rows = kc_ref.shape[0]
    row = lax.broadcasted_iota(jnp.int32, (n_rows, cols), 0)
    t_abs = t0 + (lax.broadcasted_iota(jnp.int32, (n_rows, cols), 1) & (tq - 1))
    s = _dot_nt(kc_ref[...], q4)
    valid = (row * CMP_STRIDE + (CMP_BLOCK - 1) <= t_abs) & (row < n_cmp)
    s = jnp.where(valid, s, NEG_INF)
    e = jnp.exp2(s - jnp.max(s, axis=0, keepdims=True))
    p = e * (1.0 / jnp.sum(e, axis=0, keepdims=True))
    p = jnp.where(t_abs >= CMP_BLOCK - 1, p, 0.0)
    o_cmp = _dot(vct_ref[...], p.astype(BF16))

    psum = p[:, 0:tq] + p[:, tq:2 * tq] + p[:, 2 * tq:3 * tq] + p[:, 3 * tq:4 * tq]
    imp = jnp.dot(ovl_ref[...], psum, precision=lax.Precision.HIGHEST, preferred_element_type=F32)[0:n_blk, :]
    jrow = lax.broadcasted_iota(jnp.int32, (n_blk, tq), 0)
    t_lane = t0 + lax.broadcasted_iota(jnp.int32, (n_blk, tq), 1)
    cur = t_lane >> 6
    forced = (jrow == 0) | (jrow == cur) | (jrow == cur - 1)
    imp = jnp.where(forced, FORCE_SCORE, jnp.where(jrow * SEL_BLOCK <= t_lane, imp, NEG_INF))
    rank = jnp.zeros((n_blk, tq), F32)
    for jp in range(n_blk):
        other = imp[jp:jp + 1, :]
        ahead = (other > imp) | ((other == imp) & (jrow > jp))
        rank = rank + jnp.where(ahead, 1.0, 0.0)
    bias_t = jnp.where(rank < SEL_TOP_N, 0.0, NEG_INF)
    bias_t = jnp.concatenate([bias_t, jnp.zeros((HEAD_DIM - n_blk, tq), F32)], axis=0)
    bias = bias_t.T.astype(BF16)
    q_aug = jnp.concatenate([q4, jnp.concatenate([bias] * NSA_GROUP, axis=0)], axis=1)

    def win_tile(back):
        kt = qi - back
        kt_c = jnp.maximum(kt, 0)
        sc = _dot_nt(kw_ref[pl.ds(pl.multiple_of(kt_c * tk, tk), tk), :], q4)
        hi, lo = back * tk, back * tk - WINDOW
        if lo >= -(tq - 1):
            lo_dyn = jnp.where(kt >= 0, lo, tk) if back else lo
            sc = jnp.where(key_minus_query > lo_dyn, sc, NEG_INF)
        elif back:
            sc = jnp.where(kt >= 0, sc, NEG_INF)
        if hi < tk - 1:
            sc = jnp.where(key_minus_query <= hi, sc, NEG_INF)
        return sc, vwt_sc[kt_c]

    w_tiles = [win_tile(back) for back in range(WINDOW // tk, -1, -1)]
    m_w = functools.reduce(jnp.maximum, [jnp.max(sc, axis=0, keepdims=True) for sc, _ in w_tiles])
    l_w = jnp.zeros_like(m_w)
    o_win = jnp.zeros((HEAD_DIM, cols), F32)
    for sc, vt in w_tiles:
        pr = jnp.exp2(sc - m_w)
        l_w = l_w + jnp.sum(pr, axis=0, keepdims=True)
        o_win = o_win + _dot(vt, pr.astype(BF16))
    o_win = o_win * (1.0 / l_w)

    def slc_scores(kt):
        s0 = pl.multiple_of(kt * tk, tk)
        k_aug = jnp.concatenate([ks_ref[pl.ds(s0, tk), :], eind_ref[pl.ds(s0, tk), :]], axis=1)
        return _dot_nt(k_aug, q_aug)

    def online_step(sc, vt):
        m_old = m_sc[...]
        m_new = jnp.maximum(m_old, jnp.max(sc, axis=0, keepdims=True))
        pr = jnp.exp2(sc - m_new)
        corr = jnp.exp2(m_old - m_new)
        l_sc[...] = corr * l_sc[...] + jnp.sum(pr, axis=0, keepdims=True)
        acc_sc[...] = corr * acc_sc[...] + _dot(vt, pr.astype(BF16))
        m_sc[...] = m_new

    m_sc[...] = jnp.full(m_sc.shape, NEG_INF, F32)
    l_sc[...] = jnp.zeros(l_sc.shape, F32)
    acc_sc[...] = jnp.zeros(acc_sc.shape, F32)

    def slc_body(kt, carry):
        online_step(slc_scores(kt), vst_sc[kt])
        return carry

    lax.fori_loop(0, qi, slc_body, 0)
    online_step(jnp.where(key_minus_query <= 0, slc_scores(qi), NEG_INF), vst_sc[qi])
    o_slc = acc_sc[...] * (1.0 / l_sc[...])

    gates_t = g_ref[...].T
    for g in range(NSA_GROUP):
        sl = slice(g * tq, (g + 1) * tq)
        o = (gates_t[g:g + 1, :] * o_cmp[:, sl]
             + gates_t[NSA_GROUP + g:NSA_GROUP + g + 1, :] * o_slc[:, sl]
             + gates_t[2 * NSA_GROUP + g:2 * NSA_GROUP + g + 1, :] * o_win[:, sl])
        o_ref[:, g * HEAD_DIM:(g + 1) * HEAD_DIM] = o.T.astype(o_ref.dtype)


def _nsa(h, kc, vct, gates, ovl, eind, bsz, seq, *, tq, tk):
    assert tq == tk and seq % tq == 0 and WINDOW % tk == 0
    m = h.shape[0]
    nq = seq // tq
    n_cmp = (seq - CMP_BLOCK) // CMP_STRIDE + 1
    n_blk = seq // SEL_BLOCK
    qw = NSA_GROUP * HEAD_DIM
    cols = NSA_GROUP * tq
    kv_spec = lambda col0: pl.BlockSpec((seq, HEAD_DIM), lambda b, k, i: (b, col0 + k))
    cmp_spec = pl.BlockSpec((None, None, kc.shape[2], HEAD_DIM), lambda b, k, i: (b, k, 0, 0))
    return pl.pallas_call(
        functools.partial(_nsa_kernel, tq=tq, tk=tk, n_cmp=n_cmp, n_blk=n_blk),
        grid=(bsz, NSA_KV_HEADS, nq),
        in_specs=[
            pl.BlockSpec((tq, qw), lambda b, k, i: (b * nq + i, k)),
            cmp_spec, cmp_spec,
            kv_spec(COL_KS), kv_spec(COL_VS), kv_spec(COL_KW), kv_spec(COL_VW),
            pl.BlockSpec((tq, HEAD_DIM), lambda b, k, i: (b * nq + i, k)),
            pl.BlockSpec(ovl.shape, lambda b, k, i: (0, 0)),
            pl.BlockSpec(eind.shape, lambda b, k, i: (0, 0)),
        ],
        out_specs=pl.BlockSpec((tq, qw), lambda b, k, i: (b * nq + i, k)),
        out_shape=jax.ShapeDtypeStruct((m, Q_NSA_DIM), BF16),
        scratch_shapes=[pltpu.VMEM((seq // tk, HEAD_DIM, tk), BF16), pltpu.VMEM((seq // tk, HEAD_DIM, tk), BF16),
                        pltpu.VMEM((1, cols), F32), pltpu.VMEM((1, cols), F32), pltpu.VMEM((HEAD_DIM, cols), F32)],
        compiler_params=_params(("arbitrary", "arbitrary", "arbitrary")),
        name="nsa_attention",
    )(h, kc, vct, h, h, h, h, gates, ovl, eind)


def _sb_kernel(q_ref, k_ref, v_ref, u_ref, o_ref, carry_sc, acc_sc, *, tq, tk, heads):
    qi = pl.program_id(2)
    query_minus_key = lax.broadcasted_iota(jnp.int32, (tq, tk), 0) - lax.broadcasted_iota(jnp.int32, (tq, tk), 1)
    carry_sc[...] = jnp.zeros(carry_sc.shape, F32)
    acc_sc[...] = jnp.zeros(acc_sc.shape, F32)
    u = u_ref[...]

    def tile(kt, diagonal):
        s0 = pl.multiple_of(kt * tk, tk)
        for hh in range(heads):
            cs = slice(hh * HEAD_DIM, (hh + 1) * HEAD_DIM)
            z = _dot_nt(q_ref[:, cs], k_ref[pl.ds(s0, tk), cs])
            sp = jnp.maximum(z, 0.0) + jnp.log2(1.0 + jnp.exp2(-jnp.abs(z)))
            if diagonal:
                mask = query_minus_key > 0
                sp = jnp.where(mask, sp, 0.0)
            sp_b = sp.astype(BF16)
            sums = _dot(sp_b, u)
            carry = carry_sc[hh]
            later = sums + jnp.concatenate([carry] * (tk // HEAD_DIM), axis=1)
            a = jnp.exp2(z - sp - later)
            if diagonal:
                a = jnp.where(mask, a, 0.0)
            acc_sc[hh] += _dot(a.astype(BF16), v_ref[pl.ds(s0, tk), cs])
            row_total = sums[:, 0:1] + sp_b[:, 0:1].astype(F32)
            carry_sc[hh] = carry + jnp.broadcast_to(row_total, carry.shape)

    tile(qi, True)

    def body(it, c):
        tile(qi - 1 - it, False)
        return c

    lax.fori_loop(0, qi, body, 0)
    for hh in range(heads):
        o_ref[:, hh * HEAD_DIM:(hh + 1) * HEAD_DIM] = acc_sc[hh].astype(o_ref.dtype)


def _stick_breaking(h, u_tri, bsz, seq, *, tq, tk, heads):
    assert seq % tq == 0 and tq == tk and u_tri.shape == (tk, tk)
    assert SB_HEADS % heads == 0 and COL_Q_B % heads == 0 and COL_K_B % heads == 0 and COL_V_B % heads == 0
    m = h.shape[0]
    nq = seq // tq
    wide = heads * HEAD_DIM
    return pl.pallas_call(
        functools.partial(_sb_kernel, tq=tq, tk=tk, heads=heads),
        grid=(bsz, SB_HEADS // heads, nq),
        in_specs=[
            pl.BlockSpec((tq, wide), lambda b, hg, i: (b * nq + i, COL_Q_B // heads + hg)),
            pl.BlockSpec((seq, wide), lambda b, hg, i: (b, COL_K_B // heads + hg)),
            pl.BlockSpec((seq, wide), lambda b, hg, i: (b, COL_V_B // heads + hg)),
            pl.BlockSpec(u_tri.shape, lambda b, hg, i: (0, 0)),
        ],
        out_specs=pl.BlockSpec((tq, wide), lambda b, hg, i: (b * nq + i, hg)),
        out_shape=jax.ShapeDtypeStruct((m, SB_HEADS * HEAD_DIM), BF16),
        scratch_shapes=[pltpu.VMEM((heads, tq, HEAD_DIM), F32), pltpu.VMEM((heads, tq, HEAD_DIM), F32)],
        compiler_params=_params(("parallel", "parallel", "arbitrary")),
        name="stick_breaking",
    )(h, h, h, u_tri)


def _ln_epilogue(y, x_ref, g_ref, b_ref, of_ref, ob_ref, rows=slice(None)):
    out = _layer_norm_rows(DEEPNORM_ALPHA * x_ref[rows, :] + y, g_ref[...], b_ref[...])
    of_ref[rows, :] = out
    ob_ref[rows, :] = out.astype(ob_ref.dtype)


def _mm_ln_kernel(*refs, k_splits, chunk):
    n_a = len(k_splits) - 1
    a_refs = refs[:n_a]
    w_ref, x_hbm, g_ref, b_ref, of_hbm, ob_hbm, acc_sc, x_sc, of_sc, ob_sc, sems = refs[n_a:]
    i, k = pl.program_id(0), pl.program_id(1)
    n_tiles, nk = pl.num_programs(0), k_splits[-1]
    tm = acc_sc.shape[0]

    def tile_rows(tile):
        return pl.ds(pl.multiple_of(tile * tm, tm), tm)

    def residual_copy():
        return pltpu.make_async_copy(x_hbm.at[tile_rows(i)], x_sc, sems.at[0])

    def output_copies(tile):
        return (pltpu.make_async_copy(of_sc, of_hbm.at[tile_rows(tile)], sems.at[1]),
                pltpu.make_async_copy(ob_sc, ob_hbm.at[tile_rows(tile)], sems.at[2]))

    @pl.when(k == 0)
    def _():
        residual_copy().start()

    for s, a_ref in enumerate(a_refs):
        lo, hi = k_splits[s], k_splits[s + 1]
        first = lo == 0
        last = hi == nk

        if first:
            @pl.when(k == 0)
            def _(a_ref=a_ref):
                acc_sc[...] = _dot(a_ref[...], w_ref[...])

        lo_acc, hi_acc = lo + (1 if first else 0), hi - (1 if last else 0)
        if hi_acc > lo_acc:
            @pl.when((k >= lo_acc) & (k < hi_acc))
            def _(a_ref=a_ref):
                acc_sc[...] += _dot(a_ref[...], w_ref[...])

        if last:
            @pl.when(k == nk - 1)
            def _(a_ref=a_ref):
                residual_copy().wait()

                @pl.when(i > 0)
                def _():
                    for cp in output_copies(i - 1):
                        cp.wait()

                for c in range(tm // chunk):
                    rows = slice(c * chunk, (c + 1) * chunk)
                    y = acc_sc[rows, :] + _dot(a_ref[rows, :], w_ref[...])
                    _ln_epilogue(y, x_sc, g_ref, b_ref, of_sc, ob_sc, rows)

                for cp in output_copies(i):
                    cp.start()

                @pl.when(i == n_tiles - 1)
                def _():
                    for cp in output_copies(i):
                        cp.wait()


def _matmul_ln(a_list, w, x, g, b, *, tm, tk, name, chunk=128):
    m, n = x.shape
    kdim = w.shape[0]
    nk = kdim // tk
    k_splits = [0]
    for a in a_list:
        assert a.shape[1] % tk == 0
        k_splits.append(k_splits[-1] + a.shape[1] // tk)
    assert k_splits[-1] == nk and nk >= 2 and tm % chunk == 0 and m % tm == 0
    vec = pl.BlockSpec((1, n), lambda i, k: (0, 0))
    hbm = pl.BlockSpec(memory_space=pl.ANY)

    def a_spec(lo, hi):
        return pl.BlockSpec((tm, tk), lambda i, k: (i, jnp.clip(k - lo, 0, hi - lo - 1)))

    return pl.pallas_call(
        functools.partial(_mm_ln_kernel, k_splits=tuple(k_splits), chunk=chunk),
        grid=(m // tm, nk),
        in_specs=[a_spec(k_splits[s], k_splits[s + 1]) for s in range(len(a_list))]
        + [pl.BlockSpec((tk, n), lambda i, k: (k, 0)), hbm, vec, vec],
        out_specs=[hbm, hbm],
        out_shape=[jax.ShapeDtypeStruct((m, n), F32), jax.ShapeDtypeStruct((m, n), BF16)],
        scratch_shapes=[pltpu.VMEM((tm, n), F32), pltpu.VMEM((tm, n), F32), pltpu.VMEM((tm, n), F32),
                        pltpu.VMEM((tm, n), BF16), pltpu.SemaphoreType.DMA((3,))],
        compiler_params=_params(("arbitrary", "arbitrary"), vmem=VMEM_LIMIT_BYTES + 4 * 1024 * 1024),
        name=name,
    )(*a_list, w, x, g, b)


def _stage_tiles_kernel(x_ref, o_ref, *, tiles_in, tiles_out):
    valid = pl.program_id(0) % tiles_out < tiles_in
    o_ref[...] = jnp.where(valid, x_ref[...], 0.0).astype(o_ref.dtype)


def _stage_ffn_weight(w_all, layer, d_ff, f_pad, axis, *, tile=256):
    assert d_ff % tile == 0 and f_pad % tile == 0
    tiles_in, tiles_out = d_ff // tile, f_pad // tile
    shape = list(w_all.shape[1:])
    sections = shape[axis] // d_ff
    src = lambda t: (t // tiles_out) * tiles_in + jnp.minimum(t % tiles_out, tiles_in - 1)
    if axis == 0:
        blk, out_shape = (tile, shape[1]), (sections * f_pad, shape[1])
        in_spec = pl.BlockSpec((None,) + blk, lambda t: (layer, src(t), 0))
        out_spec = pl.BlockSpec(blk, lambda t: (t, 0))
    else:
        blk, out_shape = (shape[0], tile), (shape[0], sections * f_pad)
        in_spec = pl.BlockSpec((None,) + blk, lambda t: (layer, 0, src(t)))
        out_spec = pl.BlockSpec(blk, lambda t: (0, t))
    return pl.pallas_call(
        functools.partial(_stage_tiles_kernel, tiles_in=tiles_in, tiles_out=tiles_out),
        grid=(sections * tiles_out,),
        in_specs=[in_spec],
        out_specs=out_spec,
        out_shape=jax.ShapeDtypeStruct(out_shape, BF16),
        compiler_params=_params(("parallel",)),
        name="stage_ffn_weight",
    )(w_all)


def _ffn_up_kernel(a_ref, halo_ref, wg_ref, wv_ref, cwg_ref, cwv_ref, cbg_ref, cbv_ref, o_ref, head_sc, hg_sc, hv_sc, *, tm, chunk, halo, tiles_per_seq):
    i = pl.program_id(0)

    @pl.when(pl.program_id(1) == 0)
    def _():
        seq_start = i % tiles_per_seq == 0
        head_sc[0:halo, :] = jnp.where(seq_start, jnp.zeros_like(halo_ref[...]), halo_ref[...])
        head_sc[halo:, :] = a_ref[...]

    def conv(h_sc, cw_ref, cb_ref):
        cw = cw_ref[...]
        out = cb_ref[...]
        for kk in range(CONV_WIDTH):
            out = out + h_sc[pl.ds(halo - (CONV_WIDTH - 1) + kk, chunk), :] * cw[kk:kk + 1, :]
        return out

    for c in range(tm // chunk):
        p = c % 2
        rows = slice(c * chunk, (c + 1) * chunk)
        if c == 0:
            lhs = head_sc[0:halo + chunk, :]
            hg_sc[p] = _dot(lhs, wg_ref[...])
            hv_sc[p] = _dot(lhs, wv_ref[...])
        else:
            lhs = head_sc[halo + c * chunk:halo + (c + 1) * chunk, :]
            hg_sc[p, 0:halo, :] = hg_sc[1 - p, chunk:, :]
            hv_sc[p, 0:halo, :] = hv_sc[1 - p, chunk:, :]
            hg_sc[p, halo:, :] = _dot(lhs, wg_ref[...])
            hv_sc[p, halo:, :] = _dot(lhs, wv_ref[...])
        gate = conv(hg_sc.at[p], cwg_ref, cbg_ref)
        val = conv(hv_sc.at[p], cwv_ref, cbv_ref)
        o_ref[rows, :] = (gate * jax.nn.sigmoid(gate) * val).astype(o_ref.dtype)


def _ffn_up(xb, w_up, conv_w, conv_b, seq, *, tm, tn, chunk, halo=16):
    m, kdim = xb.shape
    f_pad = w_up.shape[1] // 2
    nj = f_pad // tn
    assert seq % tm == 0 and tm % chunk == 0 and chunk % halo == 0
    hb = tm // halo
    return pl.pallas_call(
        functools.partial(_ffn_up_kernel, tm=tm, chunk=chunk, halo=halo, tiles_per_seq=seq // tm),
        grid=(m // tm, nj),
        in_specs=[
            pl.BlockSpec((tm, kdim), lambda i, j: (i, 0)),
            pl.BlockSpec((halo, kdim), lambda i, j: (jnp.maximum(i * hb - 1, 0), 0)),
            pl.BlockSpec((kdim, tn), lambda i, j: (0, j)),
            pl.BlockSpec((kdim, tn), lambda i, j: (0, nj + j)),
            pl.BlockSpec((CONV_WIDTH, tn), lambda i, j: (0, j)),
            pl.BlockSpec((CONV_WIDTH, tn), lambda i, j: (0, nj + j)),
            pl.BlockSpec((1, tn), lambda i, j: (0, j)),
            pl.BlockSpec((1, tn), lambda i, j: (0, nj + j)),
        ],
        out_specs=pl.BlockSpec((tm, tn), lambda i, j: (i, j)),
        out_shape=jax.ShapeDtypeStruct((m, f_pad), BF16),
        scratch_shapes=[pltpu.VMEM((halo + tm, kdim), BF16), pltpu.VMEM((2, halo + chunk, tn), F32),
                        pltpu.VMEM((2, halo + chunk, tn), F32)],
        compiler_params=_params(("parallel", "arbitrary")),
        name="ffn_up_conv_gate",
    )(xb, xb, w_up, w_up, conv_w, conv_w, conv_b, conv_b)


def _pool_kernel(x_ref, halo_ref, w_ref, ps_ref, g_ref, b_ref, of_ref, ob_ref, d_sc, *, tm, halo, tiles_per_seq, group_dim):
    i = pl.program_id(0)
    keep = jnp.where(i % tiles_per_seq == 0, 0.0, 1.0)
    t_in_seq = (i % tiles_per_seq) * tm + lax.broadcasted_iota(jnp.int32, (tm, group_dim), 0)
    for gi, win in enumerate(POOL_WINDOWS):
        cs = slice(gi * group_dim, (gi + 1) * group_dim)
        ext = jnp.concatenate([halo_ref[:, cs] * keep, x_ref[:, cs]], axis=0)
        acc = ext[halo:halo + tm, :]
        for back in range(1, win):
            acc = acc + ext[halo - back:halo - back + tm, :]
        count = jnp.minimum(t_in_seq + 1, win).astype(F32)
        d = acc / count - ext[halo:halo + tm, :]
        d_sc[:, cs] = _dot(d.astype(BF16), w_ref[gi])
    y = d_sc[...] * ps_ref[...]
    _ln_epilogue(y, x_ref, g_ref, b_ref, of_ref, ob_ref)


def _pool_mixer_ln(x, w_pool, pool_scale, g, b, seq, *, tm, halo=16):
    m, n = x.shape
    ngrp, group_dim = w_pool.shape[0], w_pool.shape[1]
    assert seq % tm == 0 and tm % halo == 0 and max(POOL_WINDOWS) <= halo
    hb = tm // halo
    row = pl.BlockSpec((tm, n), lambda i: (i, 0))
    vec = pl.BlockSpec((1, n), lambda i: (0, 0))
    return pl.pallas_call(
        functools.partial(_pool_kernel, tm=tm, halo=halo, tiles_per_seq=seq // tm, group_dim=group_dim),
        grid=(m // tm,),
        in_specs=[row, pl.BlockSpec((halo, n), lambda i: (jnp.maximum(i * hb - 1, 0), 0)),
                  pl.BlockSpec(w_pool.shape, lambda i: (0, 0, 0), pipeline_mode=pl.Buffered(1)), vec, vec, vec],
        out_specs=[row, row],
        out_shape=[jax.ShapeDtypeStruct((m, n), F32), jax.ShapeDtypeStruct((m, n), BF16)],
        scratch_shapes=[pltpu.VMEM((tm, n), F32)],
        compiler_params=_params(("parallel",)),
        name="pool_mixer_ln",
    )(x, x, w_pool, pool_scale, g, b)


def _rope_tables(pos):
    inv_freq = 1.0 / (ROPE_THETA ** (jnp.arange(0, HEAD_DIM, 2, dtype=F32) / HEAD_DIM))
    ang = pos.astype(F32)[:, None] * inv_freq[None, :]
    cos, sin = jnp.cos(ang), jnp.sin(ang)
    return jnp.concatenate([cos, cos], axis=-1), jnp.concatenate([-sin, sin], axis=-1)


def _overlap_t(seq):
    n_cmp = (seq - CMP_BLOCK) // CMP_STRIDE + 1
    n_blk = seq // SEL_BLOCK
    c_start = np.arange(n_cmp)[:, None] * CMP_STRIDE
    b_start = np.arange(n_blk)[None, :] * SEL_BLOCK
    ov = np.clip(np.minimum(c_start + CMP_BLOCK, b_start + SEL_BLOCK) - np.maximum(c_start, b_start), 0, None) / CMP_BLOCK
    out = np.zeros((HEAD_DIM, HEAD_DIM), np.float32)
    out[:n_blk, :n_cmp] = ov.T
    return jnp.asarray(out)


def _block_indicator(seq):
    out = np.zeros((seq, HEAD_DIM), np.float32)
    out[np.arange(seq), np.arange(seq) // SEL_BLOCK] = 1.0
    return jnp.asarray(out, BF16)


def _later_sum_matrix(n):
    return jnp.asarray(np.tril(np.ones((n, n), np.float32), -1), BF16)


def _pad_cols(w, n):
    return jnp.pad(w, ((0, 0), (0, n - w.shape[1])))


def _attention_layer(x, w_in, w_out, w_cmp_k, w_cmp_v, pe_k, pe_v, g, b, bsz, seq):
    d = x.shape[1]
    n_cmp = (seq - CMP_BLOCK) // CMP_STRIDE + 1
    n_rows = seq // CMP_STRIDE
    w_heads = jnp.concatenate([w_in[:, :GATE_OFF], w_in[:, GATE_OFF + GATE_DIM:]], axis=1).astype(BF16)
    gate_cols = np.array([[GATE_OFF + c * NSA_HEADS + k * NSA_GROUP + gg for c in range(3) for gg in range(NSA_GROUP)]
                          for k in range(NSA_KV_HEADS)])
    w_gate = jnp.concatenate([_pad_cols(w_in[:, gate_cols[k]], HEAD_DIM) for k in range(NSA_KV_HEADS)], axis=1).astype(BF16)

    cos_t, sin_t = _rope_tables(jnp.arange(seq))
    gates, xb = _gate_proj(x, w_gate, tm=512)
    h = _inproj(xb, w_heads, cos_t, sin_t, seq, tm=1024, tn=1024)

    def blocks16(col0):
        part = h[:, col0 * HEAD_DIM:(col0 + NSA_KV_HEADS) * HEAD_DIM]
        part = part.reshape(bsz, n_rows, CMP_STRIDE, NSA_KV_HEADS, HEAD_DIM).transpose(0, 3, 1, 2, 4)
        return part.reshape(bsz, NSA_KV_HEADS, n_rows, CMP_STRIDE * HEAD_DIM)

    cmp_end = jnp.arange(n_rows) * CMP_STRIDE + CMP_BLOCK - 1
    cos_c, sin_c = _rope_tables(cmp_end)
    flat_pe = lambda pe: jnp.broadcast_to(pe.reshape(1, CMP_BLOCK * HEAD_DIM), (8, CMP_BLOCK * HEAD_DIM)).astype(BF16)
    flat_w = lambda w: w.reshape(CMP_BLOCK * HEAD_DIM, HEAD_DIM).astype(BF16)
    kc, vct = _compress(blocks16(COL_KC), blocks16(COL_VC), flat_w(w_cmp_k), flat_w(w_cmp_v),
                        flat_pe(pe_k), flat_pe(pe_v), cos_c, sin_c, n_cmp)

    o_a = _nsa(h, kc, vct, gates, _overlap_t(seq), _block_indicator(seq), bsz, seq, tq=256, tk=256)
    o_b = _stick_breaking(h, _later_sum_matrix(256), bsz, seq, tq=256, tk=256, heads=8)
    return _matmul_ln([o_a, o_b], w_out.astype(BF16), x, g, b, tm=512, tk=512, name="out_proj_ln")


def _ffn_layer(x, xb, w_up_all, conv_w, conv_b, w_down_all, layer, g, b, seq, *, tn=512, tk=1024):
    d_ff = w_down_all.shape[1]
    f_pad = -(-d_ff // tk) * tk
    assert f_pad % tn == 0
    split = lambda a: jnp.concatenate([_pad_cols(a[:, :d_ff], f_pad), _pad_cols(a[:, d_ff:], f_pad)], axis=1)
    w_up_p = _stage_ffn_weight(w_up_all, layer, d_ff, f_pad, axis=1)
    conv_w_p = split(conv_w)
    conv_b_p = split(conv_b.reshape(1, -1))
    w_down_p = _stage_ffn_weight(w_down_all, layer, d_ff, f_pad, axis=0)
    gated = _ffn_up(xb, w_up_p, conv_w_p, conv_b_p, seq, tm=1024, tn=tn, chunk=256)
    return _matmul_ln([gated], w_down_p, x, g, b, tm=512, tk=tk, name="ffn_down_ln")


def kernel(x, attn_w_in, attn_w_out, cmp_w_k, cmp_w_v, cmp_pe_k, cmp_pe_v, pool_w, pool_scale, ffn_w_up, ffn_conv_w,
           ffn_conv_b, ffn_w_down, ln_mix_g, ln_mix_b, ln_ffn_g, ln_ffn_b):
    bsz, seq, d = x.shape
    xf = x.reshape(bsz * seq, d)
    xb = None
    vec = lambda a, layer: a[layer].reshape(1, d)
    for layer in range(DEPTH):
        i = layer // 2
        if layer % 2 == 0:
            xf, xb = _attention_layer(xf, attn_w_in[i], attn_w_out[i], cmp_w_k[i], cmp_w_v[i], cmp_pe_k[i],
                                      cmp_pe_v[i], vec(ln_mix_g, layer), vec(ln_mix_b, layer), bsz, seq)
        else:
            xf, xb = _pool_mixer_ln(xf, pool_w[i].astype(BF16), pool_scale[i].reshape(1, d), vec(ln_mix_g, layer),
                                    vec(ln_mix_b, layer), seq, tm=256)
        xf, xb = _ffn_layer(xf, xb, ffn_w_up, ffn_conv_w[layer], ffn_conv_b[layer], ffn_w_down, layer,
                            vec(ln_ffn_g, layer), vec(ln_ffn_b, layer), seq)
    return xf.reshape(bsz, seq, d)
```

```python
import functools

import jax
import jax.numpy as jnp
import numpy as np
from jax import lax
from jax.experimental import pallas as pl
from jax.experimental.pallas import tpu as pltpu

F32 = jnp.float32
BF16 = jnp.bfloat16

HEAD_DIM = 128
NSA_HEADS = 16
NSA_KV_HEADS = 4
NSA_GROUP = 4
SB_HEADS = 16
CMP_BLOCK = 32
CMP_STRIDE = 16
SEL_BLOCK = 64
SEL_TOP_N = 16
WINDOW = 512
ROPE_THETA = 10000.0
POOL_WINDOWS = (2, 4, 8, 16)
CONV_WIDTH = 3
LN_EPS = 1e-5
NEG_INF = -1e30
FORCE_SCORE = 1e9
DEPTH = 2
DEEPNORM_ALPHA = (2 * DEPTH) ** 0.25
SCALE = HEAD_DIM ** -0.5
LOG2E = 1.4426950408889634

COL_Q_A, COL_KC, COL_VC, COL_KS, COL_VS, COL_KW, COL_VW, COL_Q_B, COL_K_B, COL_V_B = 0, 16, 20, 24, 28, 32, 36, 40, 56, 72
N_HEAD_COLS = 88
Q_NSA_DIM = NSA_HEADS * HEAD_DIM
KV_NSA_DIM = NSA_KV_HEADS * HEAD_DIM
GATE_DIM = 3 * NSA_HEADS
GATE_OFF = Q_NSA_DIM + 6 * KV_NSA_DIM

VMEM_LIMIT_BYTES = 56 * 1024 * 1024


def _params(sem, vmem=VMEM_LIMIT_BYTES):
    return pltpu.CompilerParams(dimension_semantics=sem, vmem_limit_bytes=vmem)


def _dot(a, b):
    return jnp.dot(a, b, preferred_element_type=F32)


def _dot_nt(a, b):
    return lax.dot_general(a, b, (((1,), (1,)), ((), ())), preferred_element_type=F32)


def _layer_norm_rows(y, g, b):
    mu = jnp.mean(y, axis=-1, keepdims=True)
    d = y - mu
    var = jnp.mean(d * d, axis=-1, keepdims=True)
    return d * lax.rsqrt(var + LN_EPS) * g + b


def _transpose_bf16(x):
    return x.astype(F32).T.astype(BF16)


def _inproj_kernel(a_ref, b_ref, cos_ref, sin_ref, o_ref, *, heads_per_tile, chunk):
    j = pl.program_id(1)

    def head_scales(head):
        is_rope = (head < COL_KC) | ((head >= COL_KS) & (head < COL_VS)) | ((head >= COL_KW) & (head < COL_VW))
        is_query = (head < COL_KC) | ((head >= COL_Q_B) & (head < COL_K_B))
        q_scale = jnp.where(is_query, SCALE * LOG2E, 1.0).astype(F32)
        return jnp.where(is_rope, q_scale, 0.0), jnp.where(is_rope, 0.0, q_scale)

    scales = [head_scales(j * heads_per_tile + hh) for hh in range(heads_per_tile)]
    for c in range(a_ref.shape[0] // chunk):
        rows = slice(c * chunk, (c + 1) * chunk)
        acc = _dot(a_ref[rows, :], b_ref[...])
        cos_c, sin_c = cos_ref[rows, :], sin_ref[rows, :]
        for hh, (rope_sel, pass_sel) in enumerate(scales):
            blk = acc[:, hh * HEAD_DIM:(hh + 1) * HEAD_DIM]
            rot = pltpu.roll(blk, HEAD_DIM // 2, axis=1)
            out = blk * (cos_c * rope_sel + pass_sel) + rot * (sin_c * rope_sel)
            o_ref[rows, hh * HEAD_DIM:(hh + 1) * HEAD_DIM] = out.astype(o_ref.dtype)


def _inproj(xb, w_heads, cos_t, sin_t, seq, *, tm, tn, chunk=256):
    m, kdim = xb.shape
    n = w_heads.shape[1]
    assert tn % HEAD_DIM == 0 and n % tn == 0 and seq % tm == 0 and tm % chunk == 0
    return pl.pallas_call(
        functools.partial(_inproj_kernel, heads_per_tile=tn // HEAD_DIM, chunk=chunk),
        grid=(m // tm, n // tn),
        in_specs=[
            pl.BlockSpec((tm, kdim), lambda i, j: (i, 0)),
            pl.BlockSpec((kdim, tn), lambda i, j: (0, j)),
            pl.BlockSpec((tm, HEAD_DIM), lambda i, j: (i % (seq // tm), 0)),
            pl.BlockSpec((tm, HEAD_DIM), lambda i, j: (i % (seq // tm), 0)),
        ],
        out_specs=pl.BlockSpec((tm, tn), lambda i, j: (i, j)),
        out_shape=jax.ShapeDtypeStruct((m, n), BF16),
        compiler_params=_params(("parallel", "arbitrary")),
        name="inproj_rope",
    )(xb, w_heads, cos_t, sin_t)


def _gate_kernel(x_ref, b_ref, o_ref, xb_ref):
    xb = x_ref[...].astype(BF16)
    xb_ref[...] = xb
    o_ref[...] = jax.nn.sigmoid(_dot(xb, b_ref[...]))


def _gate_proj(x, w_gate, *, tm):
    m, kdim = x.shape
    n = w_gate.shape[1]
    return pl.pallas_call(
        _gate_kernel,
        grid=(m // tm,),
        in_specs=[pl.BlockSpec((tm, kdim), lambda i: (i, 0)), pl.BlockSpec((kdim, n), lambda i: (0, 0))],
        out_specs=[pl.BlockSpec((tm, n), lambda i: (i, 0)), pl.BlockSpec((tm, kdim), lambda i: (i, 0))],
        out_shape=[jax.ShapeDtypeStruct((m, n), F32), jax.ShapeDtypeStruct((m, kdim), BF16)],
        compiler_params=_params(("parallel",)),
        name="gate_proj",
    )(x, w_gate)


def _compress_kernel(k2_ref, v2_ref, wk_ref, wv_ref, pek_ref, pev_ref, cos_ref, sin_ref, kc_ref, vct_ref, *, n_cmp):
    half = wk_ref.shape[0] // 2
    n_rows = k2_ref.shape[0]
    row = lax.broadcasted_iota(jnp.int32, (n_rows, HEAD_DIM), 0)

    def compress(x2_ref, w_ref, pe_ref):
        x2 = x2_ref[...]
        lo = _dot(x2, w_ref[:half, :])
        hi = _dot(x2, w_ref[half:, :])
        pe_term = _dot(pe_ref[...], w_ref[...])[0:1, :]
        return lo + pltpu.roll(hi, n_rows - 1, axis=0) + pe_term

    kc = compress(k2_ref, wk_ref, pek_ref)
    vc = compress(v2_ref, wv_ref, pev_ref)
    kc = kc * cos_ref[...] + pltpu.roll(kc, HEAD_DIM // 2, axis=1) * sin_ref[...]
    kc_ref[...] = jnp.where(row < n_cmp, kc, 0.0).astype(kc_ref.dtype)
    vct_ref[...] = jnp.where(row < n_cmp, vc, 0.0).T.astype(vct_ref.dtype)


def _compress(k2, v2, wk, wv, pek, pev, cos_c, sin_c, n_cmp):
    bsz, nkv, n_rows, wide = k2.shape
    assert n_rows == HEAD_DIM
    blk4 = pl.BlockSpec((None, None, n_rows, wide), lambda b, k: (b, k, 0, 0))
    full = lambda a: pl.BlockSpec(a.shape, lambda b, k: (0,) * a.ndim)
    out_blk = pl.BlockSpec((None, None, n_rows, HEAD_DIM), lambda b, k: (b, k, 0, 0))
    out_sds = jax.ShapeDtypeStruct((bsz, nkv, n_rows, HEAD_DIM), BF16)
    return pl.pallas_call(
        functools.partial(_compress_kernel, n_cmp=n_cmp),
        grid=(bsz, nkv),
        in_specs=[blk4, blk4, full(wk), full(wv), full(pek), full(pev), full(cos_c), full(sin_c)],
        out_specs=[out_blk, out_blk],
        out_shape=[out_sds, out_sds],
        compiler_params=_params(("parallel", "parallel")),
        name="nsa_compress",
    )(k2, v2, wk, wv, pek, pev, cos_c, sin_c)


def _nsa_kernel(q_ref, kc_ref, vct_ref, ks_ref, vs_ref, kw_ref, vw_ref, g_ref, ovl_ref, eind_ref, o_ref,
                vst_sc, vwt_sc, m_sc, l_sc, acc_sc, *, tq, tk, n_cmp, n_blk):
    qi = pl.program_id(2)
    t0 = qi * tq
    cols = NSA_GROUP * tq

    @pl.when(qi == 0)
    def _():
        for kt in range(vst_sc.shape[0]):
            vst_sc[kt] = _transpose_bf16(vs_ref[kt * tk:(kt + 1) * tk, :])
            vwt_sc[kt] = _transpose_bf16(vw_ref[kt * tk:(kt + 1) * tk, :])

    q = q_ref[...]
    q4 = jnp.concatenate([q[:, g * HEAD_DIM:(g + 1) * HEAD_DIM] for g in range(NSA_GROUP)], axis=0)
    key_minus_query = (lax.broadcasted_iota(jnp.int32, (tk, cols), 0)
                       - (lax.broadcasted_iota(jnp.int32, (tk, cols), 1) & (tq - 1)))

    n_rows = kc_ref.shape[0]
    row = lax.broadcasted_iota(jnp.int32, (n_rows, cols), 0)
    t_abs = t0 + (lax.broadcasted_iota(jnp.int32, (n_rows, cols), 1) & (tq - 1))
    s = _dot_nt(kc_ref[...], q4)
    valid = (row * CMP_STRIDE + (CMP_BLOCK - 1) <= t_abs) & (row < n_cmp)
    s = jnp.where(valid, s, NEG_INF)
    e = jnp.exp2(s - jnp.max(s, axis=0, keepdims=True))
    p = e * (1.0 / jnp.sum(e, axis=0, keepdims=True))
    p = jnp.where(t_abs >= CMP_BLOCK - 1, p, 0.0)
    o_cmp = _dot(vct_ref[...], p.astype(BF16))

    psum = p[:, 0:tq] + p[:, tq:2 * tq] + p[:, 2 * tq:3 * tq] + p[:, 3 * tq:4 * tq]
    imp = jnp.dot(ovl_ref[...], psum, precision=lax.Precision.HIGHEST, preferred_element_type=F32)[0:n_blk, :]
    jrow = lax.broadcasted_iota(jnp.int32, (n_blk, tq), 0)
    t_lane = t0 + lax.broadcasted_iota(jnp.int32, (n_blk, tq), 1)
    cur = t_lane >> 6
    forced = (jrow == 0) | (jrow == cur) | (jrow == cur - 1)
    imp = jnp.where(forced, FORCE_SCORE, jnp.where(jrow * SEL_BLOCK <= t_lane, imp, NEG_INF))
    rank = jnp.zeros((n_blk, tq), F32)
    for jp in range(n_blk):
        other = imp[jp:jp + 1, :]
        ahead = (other > imp) | ((other == imp) & (jrow > jp))
        rank = rank + jnp.where(ahead, 1.0, 0.0)
    bias_t = jnp.where(rank < SEL_TOP_N, 0.0, NEG_INF)
    bias_t = jnp.concatenate([bias_t, jnp.zeros((HEAD_DIM - n_blk, tq), F32)], axis=0)
    bias = bias_t.T.astype(BF16)
    q_aug = jnp.concatenate([q4, jnp.concatenate([bias] * NSA_GROUP, axis=0)], axis=1)

    def win_tile(back):
        kt = qi - back
        kt_c = jnp.maximum(kt, 0)
        sc = _dot_nt(kw_ref[pl.ds(pl.multiple_of(kt_c * tk, tk), tk), :], q4)
        hi, lo = back * tk, back * tk - WINDOW
        if lo >= -(tq - 1):
            lo_dyn = jnp.where(kt >= 0, lo, tk) if back else lo
            sc = jnp.where(key_minus_query > lo_dyn, sc, NEG_INF)
        elif back:
            sc = jnp.where(kt >= 0, sc, NEG_INF)
        if hi < tk - 1:
            sc = jnp.where(key_minus_query <= hi, sc, NEG_INF)
        return sc, vwt_sc[kt_c]

    w_tiles = [win_tile(back) for back in range(WINDOW // tk, -1, -1)]
    m_w = functools.reduce(jnp.maximum, [jnp.max(sc, axis=0, keepdims=True) for sc, _ in w_tiles])
    l_w = jnp.zeros_like(m_w)
    o_win = jnp.zeros((HEAD_DIM, cols), F32)
    for sc, vt in w_tiles:
        pr = jnp.exp2(sc - m_w)
        l_w = l_w + jnp.sum(pr, axis=0, keepdims=True)
        o_win = o_win + _dot(vt, pr.astype(BF16))
    o_win = o_win * (1.0 / l_w)

    def slc_scores(kt):
        s0 = pl.multiple_of(kt * tk, tk)
        k_aug = jnp.concatenate([ks_ref[pl.ds(s0, tk), :], eind_ref[pl.ds(s0, tk), :]], axis=1)
        return _dot_nt(k_aug, q_aug)

    def online_step(sc, vt):
        m_old = m_sc[...]
        m_new = jnp.maximum(m_old, jnp.max(sc, axis=0, keepdims=True))
        pr = jnp.exp2(sc - m_new)
        corr = jnp.exp2(m_old - m_new)
        l_sc[...] = corr * l_sc[...] + jnp.sum(pr, axis=0, keepdims=True)
        acc_sc[...] = corr * acc_sc[...] + _dot(vt, pr.astype(BF16))
        m_sc[...] = m_new

    m_sc[...] = jnp.full(m_sc.shape, NEG_INF, F32)
    l_sc[...] = jnp.zeros(l_sc.shape, F32)
    acc_sc[...] = jnp.zeros(acc_sc.shape, F32)

    def slc_body(kt, carry):
        online_step(slc_scores(kt), vst_sc[kt])
        return carry

    lax.fori_loop(0, qi, slc_body, 0)
    online_step(jnp.where(key_minus_query <= 0, slc_scores(qi), NEG_INF), vst_sc[qi])
    o_slc = acc_sc[...] * (1.0 / l_sc[...])

    gates_t = g_ref[...].T
    for g in range(NSA_GROUP):
        sl = slice(g * tq, (g + 1) * tq)
        o = (gates_t[g:g + 1, :] * o_cmp[:, sl]
             + gates_t[NSA_GROUP + g:NSA_GROUP + g + 1, :] * o_slc[:, sl]
             + gates_t[2 * NSA_GROUP + g:2 * NSA_GROUP + g + 1, :] * o_win[:, sl])
        o_ref[:, g * HEAD_DIM:(g + 1) * HEAD_DIM] = o.T.astype(o_ref.dtype)


def _nsa(h, kc, vct, gates, ovl, eind, bsz, seq, *, tq, tk):
    assert tq == tk and seq % tq == 0 and WINDOW % tk == 0
    m = h.shape[0]
    nq = seq // tq
    n_cmp = (seq - CMP_BLOCK) // CMP_STRIDE + 1
    n_blk = seq // SEL_BLOCK
    qw = NSA_GROUP * HEAD_DIM
    cols = NSA_GROUP * tq
    kv_spec = lambda col0: pl.BlockSpec((seq, HEAD_DIM), lambda b, k, i: (b, col0 + k))
    cmp_spec = pl.BlockSpec((None, None, kc.shape[2], HEAD_DIM), lambda b, k, i: (b, k, 0, 0))
    return pl.pallas_call(
        functools.partial(_nsa_kernel, tq=tq, tk=tk, n_cmp=n_cmp, n_blk=n_blk),
        grid=(bsz, NSA_KV_HEADS, nq),
        in_specs=[
            pl.BlockSpec((tq, qw), lambda b, k, i: (b * nq + i, k)),
            cmp_spec, cmp_spec,
            kv_spec(COL_KS), kv_spec(COL_VS), kv_spec(COL_KW), kv_spec(COL_VW),
            pl.BlockSpec((tq, HEAD_DIM), lambda b, k, i: (b * nq + i, k)),
            pl.BlockSpec(ovl.shape, lambda b, k, i: (0, 0)),
            pl.BlockSpec(eind.shape, lambda b, k, i: (0, 0)),
        ],
        out_specs=pl.BlockSpec((tq, qw), lambda b, k, i: (b * nq + i, k)),
        out_shape=jax.ShapeDtypeStruct((m, Q_NSA_DIM), BF16),
        scratch_shapes=[pltpu.VMEM((seq // tk, HEAD_DIM, tk), BF16), pltpu.VMEM((seq // tk, HEAD_DIM, tk), BF16),
                        pltpu.VMEM((1, cols), F32), pltpu.VMEM((1, cols), F32), pltpu.VMEM((HEAD_DIM, cols), F32)],
        compiler_params=_params(("arbitrary", "arbitrary", "arbitrary")),
        name="nsa_attention",
    )(h, kc, vct, h, h, h, h, gates, ovl, eind)


def _sb_kernel(q_ref, k_ref, v_ref, u_ref, o_ref, carry_sc, acc_sc, *, tq, tk, heads):
    qi = pl.program_id(2)
    query_minus_key = lax.broadcasted_iota(jnp.int32, (tq, tk), 0) - lax.broadcasted_iota(jnp.int32, (tq, tk), 1)
    carry_sc[...] = jnp.zeros(carry_sc.shape, F32)
    acc_sc[...] = jnp.zeros(acc_sc.shape, F32)
    u = u_ref[...]

    def tile(kt, diagonal):
        s0 = pl.multiple_of(kt * tk, tk)
        for hh in range(heads):
            cs = slice(hh * HEAD_DIM, (hh + 1) * HEAD_DIM)
            z = _dot_nt(q_ref[:, cs], k_ref[pl.ds(s0, tk), cs])
            sp = jnp.maximum(z, 0.0) + jnp.log2(1.0 + jnp.exp2(-jnp.abs(z)))
            if diagonal:
                mask = query_minus_key > 0
                sp = jnp.where(mask, sp, 0.0)
            sp_b = sp.astype(BF16)
            sums = _dot(sp_b, u)
            carry = carry_sc[hh]
            later = sums + jnp.concatenate([carry] * (tk // HEAD_DIM), axis=1)
            a = jnp.exp2(z - sp - later)
            if diagonal:
                a = jnp.where(mask, a, 0.0)
            acc_sc[hh] += _dot(a.astype(BF16), v_ref[pl.ds(s0, tk), cs])
            row_total = sums[:, 0:1] + sp_b[:, 0:1].astype(F32)
            carry_sc[hh] = carry + jnp.broadcast_to(row_total, carry.shape)

    tile(qi, True)

    def body(it, c):
        tile(qi - 1 - it, False)
        return c

    lax.fori_loop(0, qi, body, 0)
    for hh in range(heads):
        o_ref[:, hh * HEAD_DIM:(hh + 1) * HEAD_DIM] = acc_sc[hh].astype(o_ref.dtype)


def _stick_breaking(h, u_tri, bsz, seq, *, tq, tk, heads):
    assert seq % tq == 0 and tq == tk and u_tri.shape == (tk, tk)
    assert SB_HEADS % heads == 0 and COL_Q_B % heads == 0 and COL_K_B % heads == 0 and COL_V_B % heads == 0
    m = h.shape[0]
    nq = seq // tq
    wide = heads * HEAD_DIM
    return pl.pallas_call(
        functools.partial(_sb_kernel, tq=tq, tk=tk, heads=heads),
        grid=(bsz, SB_HEADS // heads, nq),
        in_specs=[
            pl.BlockSpec((tq, wide), lambda b, hg, i: (b * nq + i, COL_Q_B // heads + hg)),
            pl.BlockSpec((seq, wide), lambda b, hg, i: (b, COL_K_B // heads + hg)),
            pl.BlockSpec((seq, wide), lambda b, hg, i: (b, COL_V_B // heads + hg)),
            pl.BlockSpec(u_tri.shape, lambda b, hg, i: (0, 0)),
        ],
        out_specs=pl.BlockSpec((tq, wide), lambda b, hg, i: (b * nq + i, hg)),
        out_shape=jax.ShapeDtypeStruct((m, SB_HEADS * HEAD_DIM), BF16),
        scratch_shapes=[pltpu.VMEM((heads, tq, HEAD_DIM), F32), pltpu.VMEM((heads, tq, HEAD_DIM), F32)],
        compiler_params=_params(("parallel", "parallel", "arbitrary")),
        name="stick_breaking",
    )(h, h, h, u_tri)


def _ln_epilogue(y, x_ref, g_ref, b_ref, of_ref, ob_ref, rows=slice(None)):
    out = _layer_norm_rows(DEEPNORM_ALPHA * x_ref[rows, :] + y, g_ref[...], b_ref[...])
    of_ref[rows, :] = out
    ob_ref[rows, :] = out.astype(ob_ref.dtype)


def _mm_ln_kernel(*refs, k_splits, chunk):
    n_a = len(k_splits) - 1
    a_refs = refs[:n_a]
    w_ref, x_hbm, g_ref, b_ref, of_hbm, ob_hbm, acc_sc, x_sc, of_sc, ob_sc, sems = refs[n_a:]
    i, k = pl.program_id(0), pl.program_id(1)
    n_tiles, nk = pl.num_programs(0), k_splits[-1]
    tm = acc_sc.shape[0]

    def tile_rows(tile):
        return pl.ds(pl.multiple_of(tile * tm, tm), tm)

    def residual_copy():
        return pltpu.make_async_copy(x_hbm.at[tile_rows(i)], x_sc, sems.at[0])

    def output_copies(tile):
        return (pltpu.make_async_copy(of_sc, of_hbm.at[tile_rows(tile)], sems.at[1]),
                pltpu.make_async_copy(ob_sc, ob_hbm.at[tile_rows(tile)], sems.at[2]))

    @pl.when(k == 0)
    def _():
        residual_copy().start()

    for s, a_ref in enumerate(a_refs):
        lo, hi = k_splits[s], k_splits[s + 1]
        first = lo == 0
        last = hi == nk

        if first:
            @pl.when(k == 0)
            def _(a_ref=a_ref):
                acc_sc[...] = _dot(a_ref[...], w_ref[...])

        lo_acc, hi_acc = lo + (1 if first else 0), hi - (1 if last else 0)
        if hi_acc > lo_acc:
            @pl.when((k >= lo_acc) & (k < hi_acc))
            def _(a_ref=a_ref):
                acc_sc[...] += _dot(a_ref[...], w_ref[...])

        if last:
            @pl.when(k == nk - 1)
            def _(a_ref=a_ref):
                residual_copy().wait()

                @pl.when(i > 0)
                def _():
                    for cp in output_copies(i - 1):
                        cp.wait()

                for c in range(tm // chunk):
                    rows = slice(c * chunk, (c + 1) * chunk)
                    y = acc_sc[rows, :] + _dot(a_ref[rows, :], w_ref[...])
                    _ln_epilogue(y, x_sc, g_ref, b_ref, of_sc, ob_sc, rows)

                for cp in output_copies(i):
                    cp.start()

                @pl.when(i == n_tiles - 1)
                def _():
                    for cp in output_copies(i):
                        cp.wait()


def _matmul_ln(a_list, w, x, g, b, *, tm, tk, name, chunk=128):
    m, n = x.shape
    kdim = w.shape[0]
    nk = kdim // tk
    k_splits = [0]
    for a in a_list:
        assert a.shape[1] % tk == 0
        k_splits.append(k_splits[-1] + a.shape[1] // tk)
    assert k_splits[-1] == nk and nk >= 2 and tm % chunk == 0 and m % tm == 0
    vec = pl.BlockSpec((1, n), lambda i, k: (0, 0))
    hbm = pl.BlockSpec(memory_space=pl.ANY)

    def a_spec(lo, hi):
        return pl.BlockSpec((tm, tk), lambda i, k: (i, jnp.clip(k - lo, 0, hi - lo - 1)))

    return pl.pallas_call(
        functools.partial(_mm_ln_kernel, k_splits=tuple(k_splits), chunk=chunk),
        grid=(m // tm, nk),
        in_specs=[a_spec(k_splits[s], k_splits[s + 1]) for s in range(len(a_list))]
        + [pl.BlockSpec((tk, n), lambda i, k: (k, 0)), hbm, vec, vec],
        out_specs=[hbm, hbm],
        out_shape=[jax.ShapeDtypeStruct((m, n), F32), jax.ShapeDtypeStruct((m, n), BF16)],
        scratch_shapes=[pltpu.VMEM((tm, n), F32), pltpu.VMEM((tm, n), F32), pltpu.VMEM((tm, n), F32),
                        pltpu.VMEM((tm, n), BF16), pltpu.SemaphoreType.DMA((3,))],
        compiler_params=_params(("arbitrary", "arbitrary"), vmem=VMEM_LIMIT_BYTES + 4 * 1024 * 1024),
        name=name,
    )(*a_list, w, x, g, b)


def _stage_tiles_kernel(x_ref, o_ref, *, tiles_in, tiles_out):
    valid = pl.program_id(0) % tiles_out < tiles_in
    o_ref[...] = jnp.where(valid, x_ref[...], 0.0).astype(o_ref.dtype)


def _stage_ffn_weight(w_all, layer, d_ff, f_pad, axis, *, tile=256):
    assert d_ff % tile == 0 and f_pad % tile == 0
    tiles_in, tiles_out = d_ff // tile, f_pad // tile
    shape = list(w_all.shape[1:])
    sections = shape[axis] // d_ff
    src = lambda t: (t // tiles_out) * tiles_in + jnp.minimum(t % tiles_out, tiles_in - 1)
    if axis == 0:
        blk, out_shape = (tile, shape[1]), (sections * f_pad, shape[1])
        in_spec = pl.BlockSpec((None,) + blk, lambda t: (layer, src(t), 0))
        out_spec = pl.BlockSpec(blk, lambda t: (t, 0))
    else:
        blk, out_shape = (shape[0], tile), (shape[0], sections * f_pad)
        in_spec = pl.BlockSpec((None,) + blk, lambda t: (layer, 0, src(t)))
        out_spec = pl.BlockSpec(blk, lambda t: (0, t))
    return pl.pallas_call(
        functools.partial(_stage_tiles_kernel, tiles_in=tiles_in, tiles_out=tiles_out),
        grid=(sections * tiles_out,),
        in_specs=[in_spec],
        out_specs=out_spec,
        out_shape=jax.ShapeDtypeStruct(out_shape, BF16),
        compiler_params=_params(("parallel",)),
        name="stage_ffn_weight",
    )(w_all)


def _ffn_up_kernel(a_ref, halo_ref, wg_ref, wv_ref, cwg_ref, cwv_ref, cbg_ref, cbv_ref, o_ref, head_sc, hg_sc, hv_sc, *, tm, chunk, halo, tiles_per_seq):
    i = pl.program_id(0)

    @pl.when(pl.program_id(1) == 0)
    def _():
        seq_start = i % tiles_per_seq == 0
        head_sc[0:halo, :] = jnp.where(seq_start, jnp.zeros_like(halo_ref[...]), halo_ref[...])
        head_sc[halo:, :] = a_ref[...]

    def conv(h_sc, cw_ref, cb_ref):
        cw = cw_ref[...]
        out = cb_ref[...]
        for kk in range(CONV_WIDTH):
            out = out + h_sc[pl.ds(halo - (CONV_WIDTH - 1) + kk, chunk), :] * cw[kk:kk + 1, :]
        return out

    for c in range(tm // chunk):
        p = c % 2
        rows = slice(c * chunk, (c + 1) * chunk)
        if c == 0:
            lhs = head_sc[0:halo + chunk, :]
            hg_sc[p] = _dot(lhs, wg_ref[...])
            hv_sc[p] = _dot(lhs, wv_ref[...])
        else:
            lhs = head_sc[halo + c * chunk:halo + (c + 1) * chunk, :]
            hg_sc[p, 0:halo, :] = hg_sc[1 - p, chunk:, :]
            hv_sc[p, 0:halo, :] = hv_sc[1 - p, chunk:, :]
            hg_sc[p, halo:, :] = _dot(lhs, wg_ref[...])
            hv_sc[p, halo:, :] = _dot(lhs, wv_ref[...])
        gate = conv(hg_sc.at[p], cwg_ref, cbg_ref)
        val = conv(hv_sc.at[p], cwv_ref, cbv_ref)
        o_ref[rows, :] = (gate * jax.nn.sigmoid(gate) * val).astype(o_ref.dtype)


def _ffn_up(xb, w_up, conv_w, conv_b, seq, *, tm, tn, chunk, halo=16):
    m, kdim = xb.shape
    f_pad = w_up.shape[1] // 2
    nj = f_pad // tn
    assert seq % tm == 0 and tm % chunk == 0 and chunk % halo == 0
    hb = tm // halo
    return pl.pallas_call(
        functools.partial(_ffn_up_kernel, tm=tm, chunk=chunk, halo=halo, tiles_per_seq=seq // tm),
        grid=(m // tm, nj),
        in_specs=[
            pl.BlockSpec((tm, kdim), lambda i, j: (i, 0)),
            pl.BlockSpec((halo, kdim), lambda i, j: (jnp.maximum(i * hb - 1, 0), 0)),
            pl.BlockSpec((kdim, tn), lambda i, j: (0, j)),
            pl.BlockSpec((kdim, tn), lambda i, j: (0, nj + j)),
            pl.BlockSpec((CONV_WIDTH, tn), lambda i, j: (0, j)),
            pl.BlockSpec((CONV_WIDTH, tn), lambda i, j: (0, nj + j)),
            pl.BlockSpec((1, tn), lambda i, j: (0, j)),
            pl.BlockSpec((1, tn), lambda i, j: (0, nj + j)),
        ],
        out_specs=pl.BlockSpec((tm, tn), lambda i, j: (i, j)),
        out_shape=jax.ShapeDtypeStruct((m, f_pad), BF16),
        scratch_shapes=[pltpu.VMEM((halo + tm, kdim), BF16), pltpu.VMEM((2, halo + chunk, tn), F32),
                        pltpu.VMEM((2, halo + chunk, tn), F32)],
        compiler_params=_params(("parallel", "arbitrary")),
        name="ffn_up_conv_gate",
    )(xb, xb, w_up, w_up, conv_w, conv_w, conv_b, conv_b)


def _pool_kernel(x_ref, halo_ref, w_ref, ps_ref, g_ref, b_ref, of_ref, ob_ref, d_sc, *, tm, halo, tiles_per_seq, group_dim):
    i = pl.program_id(0)
    keep = jnp.where(i % tiles_per_seq == 0, 0.0, 1.0)
    t_in_seq = (i % tiles_per_seq) * tm + lax.broadcasted_iota(jnp.int32, (tm, group_dim), 0)
    for gi, win in enumerate(POOL_WINDOWS):
        cs = slice(gi * group_dim, (gi + 1) * group_dim)
        ext = jnp.concatenate([halo_ref[:, cs] * keep, x_ref[:, cs]], axis=0)
        acc, span = ext, 1
        while span < win:
            acc = acc + pltpu.roll(acc, span, axis=0)
            span *= 2
        count = jnp.minimum(t_in_seq + 1, win).astype(F32)
        d = acc[halo:, :] / count - ext[halo:, :]
        d_sc[:, cs] = _dot(d.astype(BF16), w_ref[gi])
    y = d_sc[...] * ps_ref[...]
    _ln_epilogue(y, x_ref, g_ref, b_ref, of_ref, ob_ref)


def _pool_mixer_ln(x, w_pool, pool_scale, g, b, seq, *, tm, halo=16):
    m, n = x.shape
    ngrp, group_dim = w_pool.shape[0], w_pool.shape[1]
    assert seq % tm == 0 and tm % halo == 0 and max(POOL_WINDOWS) <= halo
    hb = tm // halo
    row = pl.BlockSpec((tm, n), lambda i: (i, 0))
    vec = pl.BlockSpec((1, n), lambda i: (0, 0))
    return pl.pallas_call(
        functools.partial(_pool_kernel, tm=tm, halo=halo, tiles_per_seq=seq // tm, group_dim=group_dim),
        grid=(m // tm,),
        in_specs=[row, pl.BlockSpec((halo, n), lambda i: (jnp.maximum(i * hb - 1, 0), 0)),
                  pl.BlockSpec(w_pool.shape, lambda i: (0, 0, 0), pipeline_mode=pl.Buffered(1)), vec, vec, vec],
        out_specs=[row, row],
        out_shape=[jax.ShapeDtypeStruct((m, n), F32), jax.ShapeDtypeStruct((m, n), BF16)],
        scratch_shapes=[pltpu.VMEM((tm, n), F32)],
        compiler_params=_params(("parallel",)),
        name="pool_mixer_ln",
    )(x, x, w_pool, pool_scale, g, b)


def _rope_tables(pos):
    inv_freq = 1.0 / (ROPE_THETA ** (jnp.arange(0, HEAD_DIM, 2, dtype=F32) / HEAD_DIM))
    ang = pos.astype(F32)[:, None] * inv_freq[None, :]
    cos, sin = jnp.cos(ang), jnp.sin(ang)
    return jnp.concatenate([cos, cos], axis=-1), jnp.concatenate([-sin, sin], axis=-1)


def _overlap_t(seq):
    n_cmp = (seq - CMP_BLOCK) // CMP_STRIDE + 1
    n_blk = seq // SEL_BLOCK
    c_start = np.arange(n_cmp)[:, None] * CMP_STRIDE
    b_start = np.arange(n_blk)[None, :] * SEL_BLOCK
    ov = np.clip(np.minimum(c_start + CMP_BLOCK, b_start + SEL_BLOCK) - np.maximum(c_start, b_start), 0, None) / CMP_BLOCK
    out = np.zeros((HEAD_DIM, HEAD_DIM), np.float32)
    out[:n_blk, :n_cmp] = ov.T
    return jnp.asarray(out)


def _block_indicator(seq):
    out = np.zeros((seq, HEAD_DIM), np.float32)
    out[np.arange(seq), np.arange(seq) // SEL_BLOCK] = 1.0
    return jnp.asarray(out, BF16)


def _later_sum_matrix(n):
    return jnp.asarray(np.tril(np.ones((n, n), np.float32), -1), BF16)


def _pad_cols(w, n):
    return jnp.pad(w, ((0, 0), (0, n - w.shape[1])))


def _attention_layer(x, w_in, w_out, w_cmp_k, w_cmp_v, pe_k, pe_v, g, b, bsz, seq):
    d = x.shape[1]
    n_cmp = (seq - CMP_BLOCK) // CMP_STRIDE + 1
    n_rows = seq // CMP_STRIDE
    w_heads = jnp.concatenate([w_in[:, :GATE_OFF], w_in[:, GATE_OFF + GATE_DIM:]], axis=1).astype(BF16)
    gate_cols = np.array([[GATE_OFF + c * NSA_HEADS + k * NSA_GROUP + gg for c in range(3) for gg in range(NSA_GROUP)]
                          for k in range(NSA_KV_HEADS)])
    w_gate = jnp.concatenate([_pad_cols(w_in[:, gate_cols[k]], HEAD_DIM) for k in range(NSA_KV_HEADS)], axis=1).astype(BF16)

    cos_t, sin_t = _rope_tables(jnp.arange(seq))
    gates, xb = _gate_proj(x, w_gate, tm=512)
    h = _inproj(xb, w_heads, cos_t, sin_t, seq, tm=1024, tn=1024)

    def blocks16(col0):
        part = h[:, col0 * HEAD_DIM:(col0 + NSA_KV_HEADS) * HEAD_DIM]
        part = part.reshape(bsz, n_rows, CMP_STRIDE, NSA_KV_HEADS, HEAD_DIM).transpose(0, 3, 1, 2, 4)
        return part.reshape(bsz, NSA_KV_HEADS, n_rows, CMP_STRIDE * HEAD_DIM)

    cmp_end = jnp.arange(n_rows) * CMP_STRIDE + CMP_BLOCK - 1
    cos_c, sin_c = _rope_tables(cmp_end)
    flat_pe = lambda pe: jnp.broadcast_to(pe.reshape(1, CMP_BLOCK * HEAD_DIM), (8, CMP_BLOCK * HEAD_DIM)).astype(BF16)
    flat_w = lambda w: w.reshape(CMP_BLOCK * HEAD_DIM, HEAD_DIM).astype(BF16)
    kc, vct = _compress(blocks16(COL_KC), blocks16(COL_VC), flat_w(w_cmp_k), flat_w(w_cmp_v),
                        flat_pe(pe_k), flat_pe(pe_v), cos_c, sin_c, n_cmp)

    o_a = _nsa(h, kc, vct, gates, _overlap_t(seq), _block_indicator(seq), bsz, seq, tq=256, tk=256)
    o_b = _stick_breaking(h, _later_sum_matrix(256), bsz, seq, tq=256, tk=256, heads=8)
    return _matmul_ln([o_a, o_b], w_out.astype(BF16), x, g, b, tm=512, tk=1024, name="out_proj_ln")


def _ffn_layer(x, xb, w_up_all, conv_w, conv_b, w_down_all, layer, g, b, seq, *, tn=512, tk=1024):
    d_ff = w_down_all.shape[1]
    f_pad = -(-d_ff // tk) * tk
    assert f_pad % tn == 0
    split = lambda a: jnp.concatenate([_pad_cols(a[:, :d_ff], f_pad), _pad_cols(a[:, d_ff:], f_pad)], axis=1)
    w_up_p = _stage_ffn_weight(w_up_all, layer, d_ff, f_pad, axis=1)
    conv_w_p = split(conv_w)
    conv_b_p = split(conv_b.reshape(1, -1))
    w_down_p = _stage_ffn_weight(w_down_all, layer, d_ff, f_pad, axis=0)
    gated = _ffn_up(xb, w_up_p, conv_w_p, conv_b_p, seq, tm=1024, tn=tn, chunk=256)
    return _matmul_ln([gated], w_down_p, x, g, b, tm=512, tk=tk, name="ffn_down_ln")


def kernel(x, attn_w_in, attn_w_out, cmp_w_k, cmp_w_v, cmp_pe_k, cmp_pe_v, pool_w, pool_scale, ffn_w_up, ffn_conv_w,
           ffn_conv_b, ffn_w_down, ln_mix_g, ln_mix_b, ln_ffn_g, ln_ffn_b):
    bsz, seq, d = x.shape
    xf = x.reshape(bsz * seq, d)
    xb = None
    vec = lambda a, layer: a[layer].reshape(1, d)
    for layer in range(DEPTH):
        i = layer // 2
        if layer % 2 == 0:
            xf, xb = _attention_layer(xf, attn_w_in[i], attn_w_out[i], cmp_w_k[i], cmp_w_v[i], cmp_pe_k[i],
                                      cmp_pe_v[i], vec(ln_mix_g, layer), vec(ln_mix_b, layer), bsz, seq)
        else:
            xf, xb = _pool_mixer_ln(xf, pool_w[i].astype(BF16), pool_scale[i].reshape(1, d), vec(ln_mix_g, layer),
                                    vec(ln_mix_b, layer), seq, tm=256)
        xf, xb = _ffn_layer(xf, xb, ffn_w_up, ffn_conv_w[layer], ffn_conv_b[layer], ffn_w_down, layer,
                            vec(ln_ffn_g, layer), vec(ln_ffn_b, layer), seq)
    return xf.reshape(bsz, seq, d)
```

```python
import functools

import jax
import jax.numpy as jnp
import numpy as np
from jax import lax
from jax.experimental import pallas as pl
from jax.experimental.pallas import tpu as pltpu

F32 = jnp.float32
BF16 = jnp.bfloat16

HEAD_DIM = 128
NSA_HEADS = 16
NSA_KV_HEADS = 4
NSA_GROUP = 4
SB_HEADS = 16
CMP_BLOCK = 32
CMP_STRIDE = 16
SEL_BLOCK = 64
SEL_TOP_N = 16
WINDOW = 512
ROPE_THETA = 10000.0
POOL_WINDOWS = (2, 4, 8, 16)
CONV_WIDTH = 3
LN_EPS = 1e-5
NEG_INF = -1e30
FORCE_SCORE = 1e9
DEPTH = 2
DEEPNORM_ALPHA = (2 * DEPTH) ** 0.25
SCALE = HEAD_DIM ** -0.5
LOG2E = 1.4426950408889634

COL_Q_A, COL_KC, COL_VC, COL_KS, COL_VS, COL_KW, COL_VW, COL_Q_B, COL_K_B, COL_V_B = 0, 16, 20, 24, 28, 32, 36, 40, 56, 72
N_HEAD_COLS = 88
Q_NSA_DIM = NSA_HEADS * HEAD_DIM
KV_NSA_DIM = NSA_KV_HEADS * HEAD_DIM
GATE_DIM = 3 * NSA_HEADS
GATE_OFF = Q_NSA_DIM + 6 * KV_NSA_DIM

VMEM_LIMIT_BYTES = 56 * 1024 * 1024


def _params(sem, vmem=VMEM_LIMIT_BYTES):
    return pltpu.CompilerParams(dimension_semantics=sem, vmem_limit_bytes=vmem)


def _dot(a, b):
    return jnp.dot(a, b, preferred_element_type=F32)


def _dot_nt(a, b):
    return lax.dot_general(a, b, (((1,), (1,)), ((), ())), preferred_element_type=F32)


def _layer_norm_rows(y, g, b):
    mu = jnp.mean(y, axis=-1, keepdims=True)
    d = y - mu
    var = jnp.mean(d * d, axis=-1, keepdims=True)
    return d * lax.rsqrt(var + LN_EPS) * g + b


def _transpose_bf16(x):
    return x.astype(F32).T.astype(BF16)


def _inproj_kernel(a_ref, b_ref, cos_ref, sin_ref, o_ref, *, heads_per_tile, chunk):
    j = pl.program_id(1)

    def head_scales(head):
        is_rope = (head < COL_KC) | ((head >= COL_KS) & (head < COL_VS)) | ((head >= COL_KW) & (head < COL_VW))
        is_query = (head < COL_KC) | ((head >= COL_Q_B) & (head < COL_K_B))
        q_scale = jnp.where(is_query, SCALE * LOG2E, 1.0).astype(F32)
        return jnp.where(is_rope, q_scale, 0.0), jnp.where(is_rope, 0.0, q_scale)

    scales = [head_scales(j * heads_per_tile + hh) for hh in range(heads_per_tile)]
    for c in range(a_ref.shape[0] // chunk):
        rows = slice(c * chunk, (c + 1) * chunk)
        acc = _dot(a_ref[rows, :], b_ref[...])
        cos_c, sin_c = cos_ref[rows, :], sin_ref[rows, :]
        for hh, (rope_sel, pass_sel) in enumerate(scales):
            blk = acc[:, hh * HEAD_DIM:(hh + 1) * HEAD_DIM]
            rot = pltpu.roll(blk, HEAD_DIM // 2, axis=1)
            out = blk * (cos_c * rope_sel + pass_sel) + rot * (sin_c * rope_sel)
            o_ref[rows, hh * HEAD_DIM:(hh + 1) * HEAD_DIM] = out.astype(o_ref.dtype)


def _inproj(xb, w_heads, cos_t, sin_t, seq, *, tm, tn, chunk=256):
    m, kdim = xb.shape
    n = w_heads.shape[1]
    assert tn % HEAD_DIM == 0 and n % tn == 0 and seq % tm == 0 and tm % chunk == 0
    return pl.pallas_call(
        functools.partial(_inproj_kernel, heads_per_tile=tn // HEAD_DIM, chunk=chunk),
        grid=(m // tm, n // tn),
        in_specs=[
            pl.BlockSpec((tm, kdim), lambda i, j: (i, 0)),
            pl.BlockSpec((kdim, tn), lambda i, j: (0, j)),
            pl.BlockSpec((tm, HEAD_DIM), lambda i, j: (i % (seq // tm), 0)),
            pl.BlockSpec((tm, HEAD_DIM), lambda i, j: (i % (seq // tm), 0)),
        ],
        out_specs=pl.BlockSpec((tm, tn), lambda i, j: (i, j)),
        out_shape=jax.ShapeDtypeStruct((m, n), BF16),
        compiler_params=_params(("parallel", "arbitrary")),
        name="inproj_rope",
    )(xb, w_heads, cos_t, sin_t)


def _gate_kernel(x_ref, b_ref, o_ref, xb_ref):
    xb = x_ref[...].astype(BF16)
    xb_ref[...] = xb
    o_ref[...] = jax.nn.sigmoid(_dot(xb, b_ref[...]))


def _gate_proj(x, w_gate, *, tm):
    m, kdim = x.shape
    n = w_gate.shape[1]
    return pl.pallas_call(
        _gate_kernel,
        grid=(m // tm,),
        in_specs=[pl.BlockSpec((tm, kdim), lambda i: (i, 0)), pl.BlockSpec((kdim, n), lambda i: (0, 0))],
        out_specs=[pl.BlockSpec((tm, n), lambda i: (i, 0)), pl.BlockSpec((tm, kdim), lambda i: (i, 0))],
        out_shape=[jax.ShapeDtypeStruct((m, n), F32), jax.ShapeDtypeStruct((m, kdim), BF16)],
        compiler_params=_params(("parallel",)),
        name="gate_proj",
    )(x, w_gate)


def _compress_kernel(k2_ref, v2_ref, wk_ref, wv_ref, pek_ref, pev_ref, cos_ref, sin_ref, kc_ref, vct_ref, *, n_cmp):
    half = wk_ref.shape[0] // 2
    n_rows = k2_ref.shape[0]
    row = lax.broadcasted_iota(jnp.int32, (n_rows, HEAD_DIM), 0)

    def compress(x2_ref, w_ref, pe_ref):
        x2 = x2_ref[...]
        lo = _dot(x2, w_ref[:half, :])
        hi = _dot(x2, w_ref[half:, :])
        pe_term = _dot(pe_ref[...], w_ref[...])[0:1, :]
        return lo + pltpu.roll(hi, n_rows - 1, axis=0) + pe_term

    kc = compress(k2_ref, wk_ref, pek_ref)
    vc = compress(v2_ref, wv_ref, pev_ref)
    kc = kc * cos_ref[...] + pltpu.roll(kc, HEAD_DIM // 2, axis=1) * sin_ref[...]
    kc_ref[...] = jnp.where(row < n_cmp, kc, 0.0).astype(kc_ref.dtype)
    vct_ref[...] = jnp.where(row < n_cmp, vc, 0.0).T.astype(vct_ref.dtype)


def _compress(k2, v2, wk, wv, pek, pev, cos_c, sin_c, n_cmp):
    bsz, nkv, n_rows, wide = k2.shape
    assert n_rows == HEAD_DIM
    blk4 = pl.BlockSpec((None, None, n_rows, wide), lambda b, k: (b, k, 0, 0))
    full = lambda a: pl.BlockSpec(a.shape, lambda b, k: (0,) * a.ndim)
    out_blk = pl.BlockSpec((None, None, n_rows, HEAD_DIM), lambda b, k: (b, k, 0, 0))
    out_sds = jax.ShapeDtypeStruct((bsz, nkv, n_rows, HEAD_DIM), BF16)
    return pl.pallas_call(
        functools.partial(_compress_kernel, n_cmp=n_cmp),
        grid=(bsz, nkv),
        in_specs=[blk4, blk4, full(wk), full(wv), full(pek), full(pev), full(cos_c), full(sin_c)],
        out_specs=[out_blk, out_blk],
        out_shape=[out_sds, out_sds],
        compiler_params=_params(("parallel", "parallel")),
        name="nsa_compress",
    )(k2, v2, wk, wv, pek, pev, cos_c, sin_c)


def _nsa_kernel(q_ref, kc_ref, vct_ref, ks_ref, vs_ref, kw_ref, vw_ref, g_ref, ovl_ref, eind_ref, o_ref,
                vst_sc, vwt_sc, m_sc, l_sc, acc_sc, *, tq, tk, n_cmp, n_blk):
    qi = pl.program_id(2)
    t0 = qi * tq
    cols = NSA_GROUP * tq

    @pl.when(qi == 0)
    def _():
        for kt in range(vst_sc.shape[0]):
            vst_sc[kt] = _transpose_bf16(vs_ref[kt * tk:(kt + 1) * tk, :])
            vwt_sc[kt] = _transpose_bf16(vw_ref[kt * tk:(kt + 1) * tk, :])

    q = q_ref[...]
    q4 = jnp.concatenate([q[:, g * HEAD_DIM:(g + 1) * HEAD_DIM] for g in range(NSA_GROUP)], axis=0)
    key_minus_query = (lax.broadcasted_iota(jnp.int32, (tk, cols), 0)
                       - (lax.broadcasted_iota(jnp.int32, (tk, cols), 1) & (tq - 1)))

    n_rows = kc_ref.shape[0]
    row = lax.broadcasted_iota(jnp.int32, (n_rows, cols), 0)
    t_abs = t0 + (lax.broadcasted_iota(jnp.int32, (n_rows, cols), 1) & (tq - 1))
    s = _dot_nt(kc_ref[...], q4)
    valid = (row * CMP_STRIDE + (CMP_BLOCK - 1) <= t_abs) & (row < n_cmp)
    s = jnp.where(valid, s, NEG_INF)
    e = jnp.exp2(s - jnp.max(s, axis=0, keepdims=True))
    p = e * (1.0 / jnp.sum(e, axis=0, keepdims=True))
    p = jnp.where(t_abs >= CMP_BLOCK - 1, p, 0.0)
    o_cmp = _dot(vct_ref[...], p.astype(BF16))

    psum = p[:, 0:tq] + p[:, tq:2 * tq] + p[:, 2 * tq:3 * tq] + p[:, 3 * tq:4 * tq]
    imp = jnp.dot(ovl_ref[...], psum, precision=lax.Precision.HIGHEST, preferred_element_type=F32)[0:n_blk, :]
    jrow = lax.broadcasted_iota(jnp.int32, (n_blk, tq), 0)
    t_lane = t0 + lax.broadcasted_iota(jnp.int32, (n_blk, tq), 1)
    cur = t_lane >> 6
    forced = (jrow == 0) | (jrow == cur) | (jrow == cur - 1)
    imp = jnp.where(forced, FORCE_SCORE, jnp.where(jrow * SEL_BLOCK <= t_lane, imp, NEG_INF))
    rank = jnp.zeros((n_blk, tq), F32)
    for jp in range(n_blk):
        other = imp[jp:jp + 1, :]
        ahead = (other > imp) | ((other == imp) & (jrow > jp))
        rank = rank + jnp.where(ahead, 1.0, 0.0)
    bias_t = jnp.where(rank < SEL_TOP_N, 0.0, NEG_INF)
    bias_t = jnp.concatenate([bias_t, jnp.zeros((HEAD_DIM - n_blk, tq), F32)], axis=0)
    bias = bias_t.T.astype(BF16)
    q_aug = jnp.concatenate([q4, jnp.concatenate([bias] * NSA_GROUP, axis=0)], axis=1)

    def win_tile(back):
        kt = qi - back
        kt_c = jnp.maximum(kt, 0)
        sc = _dot_nt(kw_ref[pl.ds(pl.multiple_of(kt_c * tk, tk), tk), :], q4)
        hi, lo = back * tk, back * tk - WINDOW
        if lo >= -(tq - 1):
            lo_dyn = jnp.where(kt >= 0, lo, tk) if back else lo
            sc = jnp.where(key_minus_query > lo_dyn, sc, NEG_INF)
        elif back:
            sc = jnp.where(kt >= 0, sc, NEG_INF)
        if hi < tk - 1:
            sc = jnp.where(key_minus_query <= hi, sc, NEG_INF)
        return sc, vwt_sc[kt_c]

    w_tiles = [win_tile(back) for back in range(WINDOW // tk, -1, -1)]
    m_w = functools.reduce(jnp.maximum, [jnp.max(sc, axis=0, keepdims=True) for sc, _ in w_tiles])
    l_w = jnp.zeros_like(m_w)
    o_win = jnp.zeros((HEAD_DIM, cols), F32)
    for sc, vt in w_tiles:
        pr = jnp.exp2(sc - m_w)
        l_w = l_w + jnp.sum(pr, axis=0, keepdims=True)
        o_win = o_win + _dot(vt, pr.astype(BF16))
    o_win = o_win * (1.0 / l_w)

    def slc_scores(kt, n_tiles=1):
        s0 = pl.multiple_of(kt * tk, tk)
        k_aug = jnp.concatenate([ks_ref[pl.ds(s0, n_tiles * tk), :], eind_ref[pl.ds(s0, n_tiles * tk), :]], axis=1)
        return _dot_nt(k_aug, q_aug)

    def online_step(sc, vt):
        m_old = m_sc[...]
        m_new = jnp.maximum(m_old, jnp.max(sc, axis=0, keepdims=True))
        pr = jnp.exp2(sc - m_new)
        corr = jnp.exp2(m_old - m_new)
        l_sc[...] = corr * l_sc[...] + jnp.sum(pr, axis=0, keepdims=True)
        acc_sc[...] = corr * acc_sc[...] + _dot(vt, pr.astype(BF16))
        m_sc[...] = m_new

    m_sc[...] = jnp.full(m_sc.shape, NEG_INF, F32)
    l_sc[...] = jnp.zeros(l_sc.shape, F32)
    acc_sc[...] = jnp.zeros(acc_sc.shape, F32)

    def slc_pair(pt, carry):
        kt = 2 * pt
        online_step(slc_scores(kt, 2), jnp.concatenate([vst_sc[kt], vst_sc[kt + 1]], axis=1))
        return carry

    lax.fori_loop(0, qi // 2, slc_pair, 0)

    @pl.when(qi % 2 == 1)
    def _():
        online_step(slc_scores(qi - 1), vst_sc[qi - 1])

    online_step(jnp.where(key_minus_query <= 0, slc_scores(qi), NEG_INF), vst_sc[qi])
    o_slc = acc_sc[...] * (1.0 / l_sc[...])

    gates_t = g_ref[...].T
    for g in range(NSA_GROUP):
        sl = slice(g * tq, (g + 1) * tq)
        o = (gates_t[g:g + 1, :] * o_cmp[:, sl]
             + gates_t[NSA_GROUP + g:NSA_GROUP + g + 1, :] * o_slc[:, sl]
             + gates_t[2 * NSA_GROUP + g:2 * NSA_GROUP + g + 1, :] * o_win[:, sl])
        o_ref[:, g * HEAD_DIM:(g + 1) * HEAD_DIM] = o.T.astype(o_ref.dtype)


def _nsa(h, kc, vct, gates, ovl, eind, bsz, seq, *, tq, tk):
    assert tq == tk and seq % tq == 0 and WINDOW % tk == 0
    m = h.shape[0]
    nq = seq // tq
    n_cmp = (seq - CMP_BLOCK) // CMP_STRIDE + 1
    n_blk = seq // SEL_BLOCK
    qw = NSA_GROUP * HEAD_DIM
    cols = NSA_GROUP * tq
    kv_spec = lambda col0: pl.BlockSpec((seq, HEAD_DIM), lambda b, k, i: (b, col0 + k))
    cmp_spec = pl.BlockSpec((None, None, kc.shape[2], HEAD_DIM), lambda b, k, i: (b, k, 0, 0))
    return pl.pallas_call(
        functools.partial(_nsa_kernel, tq=tq, tk=tk, n_cmp=n_cmp, n_blk=n_blk),
        grid=(bsz, NSA_KV_HEADS, nq),
        in_specs=[
            pl.BlockSpec((tq, qw), lambda b, k, i: (b * nq + i, k)),
            cmp_spec, cmp_spec,
            kv_spec(COL_KS), kv_spec(COL_VS), kv_spec(COL_KW), kv_spec(COL_VW),
            pl.BlockSpec((tq, HEAD_DIM), lambda b, k, i: (b * nq + i, k)),
            pl.BlockSpec(ovl.shape, lambda b, k, i: (0, 0)),
            pl.BlockSpec(eind.shape, lambda b, k, i: (0, 0)),
        ],
        out_specs=pl.BlockSpec((tq, qw), lambda b, k, i: (b * nq + i, k)),
        out_shape=jax.ShapeDtypeStruct((m, Q_NSA_DIM), BF16),
        scratch_shapes=[pltpu.VMEM((seq // tk, HEAD_DIM, tk), BF16), pltpu.VMEM((seq // tk, HEAD_DIM, tk), BF16),
                        pltpu.VMEM((1, cols), F32), pltpu.VMEM((1, cols), F32), pltpu.VMEM((HEAD_DIM, cols), F32)],
        compiler_params=_params(("arbitrary", "arbitrary", "arbitrary")),
        name="nsa_attention",
    )(h, kc, vct, h, h, h, h, gates, ovl, eind)


def _sb_kernel(q_ref, k_ref, v_ref, u_ref, o_ref, carry_sc, acc_sc, *, tq, tk, heads):
    qi = pl.program_id(2)
    query_minus_key = lax.broadcasted_iota(jnp.int32, (tq, tk), 0) - lax.broadcasted_iota(jnp.int32, (tq, tk), 1)
    carry_sc[...] = jnp.zeros(carry_sc.shape, F32)
    acc_sc[...] = jnp.zeros(acc_sc.shape, F32)
    u = u_ref[...]

    def tile(kt, diagonal):
        s0 = pl.multiple_of(kt * tk, tk)
        for hh in range(heads):
            cs = slice(hh * HEAD_DIM, (hh + 1) * HEAD_DIM)
            z = _dot_nt(q_ref[:, cs], k_ref[pl.ds(s0, tk), cs])
            sp = jnp.maximum(z, 0.0) + jnp.log2(1.0 + jnp.exp2(-jnp.abs(z)))
            if diagonal:
                mask = query_minus_key > 0
                sp = jnp.where(mask, sp, 0.0)
            sp_b = sp.astype(BF16)
            sums = _dot(sp_b, u)
            carry = carry_sc[hh]
            later = sums + jnp.concatenate([carry] * (tk // HEAD_DIM), axis=1)
            a = jnp.exp2(z - sp - later)
            if diagonal:
                a = jnp.where(mask, a, 0.0)
            acc_sc[hh] += _dot(a.astype(BF16), v_ref[pl.ds(s0, tk), cs])
            row_total = sums[:, 0:1] + sp_b[:, 0:1].astype(F32)
            carry_sc[hh] = carry + jnp.broadcast_to(row_total, carry.shape)

    tile(qi, True)

    def body(it, c):
        tile(qi - 1 - it, False)
        return c

    lax.fori_loop(0, qi, body, 0)
    for hh in range(heads):
        o_ref[:, hh * HEAD_DIM:(hh + 1) * HEAD_DIM] = acc_sc[hh].astype(o_ref.dtype)


def _stick_breaking(h, u_tri, bsz, seq, *, tq, tk, heads):
    assert seq % tq == 0 and tq == tk and u_tri.shape == (tk, tk)
    assert SB_HEADS % heads == 0 and COL_Q_B % heads == 0 and COL_K_B % heads == 0 and COL_V_B % heads == 0
    m = h.shape[0]
    nq = seq // tq
    wide = heads * HEAD_DIM
    return pl.pallas_call(
        functools.partial(_sb_kernel, tq=tq, tk=tk, heads=heads),
        grid=(bsz, SB_HEADS // heads, nq),
        in_specs=[
            pl.BlockSpec((tq, wide), lambda b, hg, i: (b * nq + i, COL_Q_B // heads + hg)),
            pl.BlockSpec((seq, wide), lambda b, hg, i: (b, COL_K_B // heads + hg)),
            pl.BlockSpec((seq, wide), lambda b, hg, i: (b, COL_V_B // heads + hg)),
            pl.BlockSpec(u_tri.shape, lambda b, hg, i: (0, 0)),
        ],
        out_specs=pl.BlockSpec((tq, wide), lambda b, hg, i: (b * nq + i, hg)),
        out_shape=jax.ShapeDtypeStruct((m, SB_HEADS * HEAD_DIM), BF16),
        scratch_shapes=[pltpu.VMEM((heads, tq, HEAD_DIM), F32), pltpu.VMEM((heads, tq, HEAD_DIM), F32)],
        compiler_params=_params(("parallel", "parallel", "arbitrary")),
        name="stick_breaking",
    )(h, h, h, u_tri)


def _ln_epilogue(y, x_ref, g_ref, b_ref, of_ref, ob_ref, rows=slice(None)):
    out = _layer_norm_rows(DEEPNORM_ALPHA * x_ref[rows, :] + y, g_ref[...], b_ref[...])
    of_ref[rows, :] = out
    ob_ref[rows, :] = out.astype(ob_ref.dtype)


def _mm_ln_kernel(*refs, k_splits, chunk, last_k):
    n_a = len(k_splits) - 1
    a_refs = refs[:n_a]
    w_ref, x_hbm, g_ref, b_ref, of_hbm, ob_hbm, acc_sc, x_sc, of_sc, ob_sc, sems = refs[n_a:]
    i, k = pl.program_id(0), pl.program_id(1)
    n_tiles, nk = pl.num_programs(0), k_splits[-1]
    tm = acc_sc.shape[0]

    def tile_rows(tile):
        return pl.ds(pl.multiple_of(tile * tm, tm), tm)

    def residual_copy():
        return pltpu.make_async_copy(x_hbm.at[tile_rows(i)], x_sc, sems.at[0])

    def output_copies(tile):
        return (pltpu.make_async_copy(of_sc, of_hbm.at[tile_rows(tile)], sems.at[1]),
                pltpu.make_async_copy(ob_sc, ob_hbm.at[tile_rows(tile)], sems.at[2]))

    @pl.when(k == 0)
    def _():
        residual_copy().start()

    for s, a_ref in enumerate(a_refs):
        lo, hi = k_splits[s], k_splits[s + 1]
        first = lo == 0
        last = hi == nk

        if first:
            @pl.when(k == 0)
            def _(a_ref=a_ref):
                acc_sc[...] = _dot(a_ref[...], w_ref[...])

        lo_acc, hi_acc = lo + (1 if first else 0), hi - (1 if last else 0)
        if hi_acc > lo_acc:
            @pl.when((k >= lo_acc) & (k < hi_acc))
            def _(a_ref=a_ref):
                acc_sc[...] += _dot(a_ref[...], w_ref[...])

        if last:
            @pl.when(k == nk - 1)
            def _(a_ref=a_ref):
                residual_copy().wait()

                @pl.when(i > 0)
                def _():
                    for cp in output_copies(i - 1):
                        cp.wait()

                for c in range(tm // chunk):
                    rows = slice(c * chunk, (c + 1) * chunk)
                    y = acc_sc[rows, :] + _dot(a_ref[rows, :last_k], w_ref[:last_k, :])
                    _ln_epilogue(y, x_sc, g_ref, b_ref, of_sc, ob_sc, rows)

                for cp in output_copies(i):
                    cp.start()

                @pl.when(i == n_tiles - 1)
                def _():
                    for cp in output_copies(i):
                        cp.wait()


def _matmul_ln(a_list, w, x, g, b, *, tm, tk, name, chunk=128, k_valid=None):
    m, n = x.shape
    kdim = w.shape[0]
    nk = kdim // tk
    k_splits = [0]
    for a in a_list:
        assert a.shape[1] % tk == 0
        k_splits.append(k_splits[-1] + a.shape[1] // tk)
    assert k_splits[-1] == nk and nk >= 2 and tm % chunk == 0 and m % tm == 0
    last_k = tk if k_valid is None else k_valid - (nk - 1) * tk
    assert 0 < last_k <= tk and last_k % 256 == 0
    vec = pl.BlockSpec((1, n), lambda i, k: (0, 0))
    hbm = pl.BlockSpec(memory_space=pl.ANY)

    def a_spec(lo, hi):
        return pl.BlockSpec((tm, tk), lambda i, k: (i, jnp.clip(k - lo, 0, hi - lo - 1)))

    return pl.pallas_call(
        functools.partial(_mm_ln_kernel, k_splits=tuple(k_splits), chunk=chunk, last_k=last_k),
        grid=(m // tm, nk),
        in_specs=[a_spec(k_splits[s], k_splits[s + 1]) for s in range(len(a_list))]
        + [pl.BlockSpec((tk, n), lambda i, k: (k, 0)), hbm, vec, vec],
        out_specs=[hbm, hbm],
        out_shape=[jax.ShapeDtypeStruct((m, n), F32), jax.ShapeDtypeStruct((m, n), BF16)],
        scratch_shapes=[pltpu.VMEM((tm, n), F32), pltpu.VMEM((tm, n), F32), pltpu.VMEM((tm, n), F32),
                        pltpu.VMEM((tm, n), BF16), pltpu.SemaphoreType.DMA((3,))],
        compiler_params=_params(("arbitrary", "arbitrary"), vmem=VMEM_LIMIT_BYTES + 4 * 1024 * 1024),
        name=name,
    )(*a_list, w, x, g, b)


def _stage_tiles_kernel(x_ref, o_ref, *, tiles_in, tiles_out):
    valid = pl.program_id(0) % tiles_out < tiles_in
    o_ref[...] = jnp.where(valid, x_ref[...], 0.0).astype(o_ref.dtype)


def _stage_ffn_weight(w_all, layer, d_ff, f_pad, axis, *, tile=256):
    assert d_ff % tile == 0 and f_pad % tile == 0
    tiles_in, tiles_out = d_ff // tile, f_pad // tile
    shape = list(w_all.shape[1:])
    sections = shape[axis] // d_ff
    src = lambda t: (t // tiles_out) * tiles_in + jnp.minimum(t % tiles_out, tiles_in - 1)
    if axis == 0:
        blk, out_shape = (tile, shape[1]), (sections * f_pad, shape[1])
        in_spec = pl.BlockSpec((None,) + blk, lambda t: (layer, src(t), 0))
        out_spec = pl.BlockSpec(blk, lambda t: (t, 0))
    else:
        blk, out_shape = (shape[0], tile), (shape[0], sections * f_pad)
        in_spec = pl.BlockSpec((None,) + blk, lambda t: (layer, 0, src(t)))
        out_spec = pl.BlockSpec(blk, lambda t: (0, t))
    return pl.pallas_call(
        functools.partial(_stage_tiles_kernel, tiles_in=tiles_in, tiles_out=tiles_out),
        grid=(sections * tiles_out,),
        in_specs=[in_spec],
        out_specs=out_spec,
        out_shape=jax.ShapeDtypeStruct(out_shape, BF16),
        compiler_params=_params(("parallel",)),
        name="stage_ffn_weight",
    )(w_all)


def _ffn_up_kernel(a_ref, halo_ref, wg_ref, wv_ref, cwg_ref, cwv_ref, cbg_ref, cbv_ref, o_ref, head_sc, hg_sc, hv_sc, *,
                   tm, chunk, halo, tiles_per_seq, last_cols):
    i, j = pl.program_id(0), pl.program_id(1)
    tn = o_ref.shape[1]

    @pl.when(j == 0)
    def _():
        seq_start = i % tiles_per_seq == 0
        head_sc[0:halo, :] = jnp.where(seq_start, jnp.zeros_like(halo_ref[...]), halo_ref[...])
        head_sc[halo:, :] = a_ref[...]

    def run(cols):
        def conv(h_sc, cw_ref, cb_ref):
            cw = cw_ref[:, :cols]
            out = cb_ref[:, :cols]
            for kk in range(CONV_WIDTH):
                out = out + h_sc[pl.ds(halo - (CONV_WIDTH - 1) + kk, chunk), :cols] * cw[kk:kk + 1, :]
            return out

        for c in range(tm // chunk):
            p = c % 2
            rows = slice(c * chunk, (c + 1) * chunk)
            if c == 0:
                lhs = head_sc[0:halo + chunk, :]
                hg_sc[p, :, :cols] = _dot(lhs, wg_ref[:, :cols])
                hv_sc[p, :, :cols] = _dot(lhs, wv_ref[:, :cols])
            else:
                lhs = head_sc[halo + c * chunk:halo + (c + 1) * chunk, :]
                hg_sc[p, 0:halo, :cols] = hg_sc[1 - p, chunk:, :cols]
                hv_sc[p, 0:halo, :cols] = hv_sc[1 - p, chunk:, :cols]
                hg_sc[p, halo:, :cols] = _dot(lhs, wg_ref[:, :cols])
                hv_sc[p, halo:, :cols] = _dot(lhs, wv_ref[:, :cols])
            gate = conv(hg_sc.at[p], cwg_ref, cbg_ref)
            val = conv(hv_sc.at[p], cwv_ref, cbv_ref)
            o_ref[rows, :cols] = (gate * jax.nn.sigmoid(gate) * val).astype(o_ref.dtype)
        if cols < tn:
            o_ref[:, cols:] = jnp.zeros((tm, tn - cols), o_ref.dtype)

    if last_cols == tn:
        run(tn)
    else:
        last = pl.num_programs(1) - 1
        pl.when(j < last)(lambda: run(tn))
        pl.when(j == last)(lambda: run(last_cols))


def _ffn_up(xb, w_up, conv_w, conv_b, seq, d_ff, *, tm, tn, chunk, halo=16):
    m, kdim = xb.shape
    f_pad = w_up.shape[1] // 2
    nj = f_pad // tn
    assert seq % tm == 0 and tm % chunk == 0 and chunk % halo == 0
    hb = tm // halo
    last_cols = d_ff - (nj - 1) * tn
    assert 0 < last_cols <= tn and last_cols % HEAD_DIM == 0
    return pl.pallas_call(
        functools.partial(_ffn_up_kernel, tm=tm, chunk=chunk, halo=halo, tiles_per_seq=seq // tm, last_cols=last_cols),
        grid=(m // tm, nj),
        in_specs=[
            pl.BlockSpec((tm, kdim), lambda i, j: (i, 0)),
            pl.BlockSpec((halo, kdim), lambda i, j: (jnp.maximum(i * hb - 1, 0), 0)),
            pl.BlockSpec((kdim, tn), lambda i, j: (0, j)),
            pl.BlockSpec((kdim, tn), lambda i, j: (0, nj + j)),
            pl.BlockSpec((CONV_WIDTH, tn), lambda i, j: (0, j)),
            pl.BlockSpec((CONV_WIDTH, tn), lambda i, j: (0, nj + j)),
            pl.BlockSpec((1, tn), lambda i, j: (0, j)),
            pl.BlockSpec((1, tn), lambda i, j: (0, nj + j)),
        ],
        out_specs=pl.BlockSpec((tm, tn), lambda i, j: (i, j)),
        out_shape=jax.ShapeDtypeStruct((m, f_pad), BF16),
        scratch_shapes=[pltpu.VMEM((halo + tm, kdim), BF16), pltpu.VMEM((2, halo + chunk, tn), F32),
                        pltpu.VMEM((2, halo + chunk, tn), F32)],
        compiler_params=_params(("parallel", "arbitrary")),
        name="ffn_up_conv_gate",
    )(xb, xb, w_up, w_up, conv_w, conv_w, conv_b, conv_b)


def _pool_kernel(x_ref, halo_ref, w_ref, ps_ref, g_ref, b_ref, of_ref, ob_ref, d_sc, *, tm, halo, tiles_per_seq, group_dim):
    i = pl.program_id(0)
    keep = jnp.where(i % tiles_per_seq == 0, 0.0, 1.0)
    t_in_seq = (i % tiles_per_seq) * tm + lax.broadcasted_iota(jnp.int32, (tm, group_dim), 0)
    for gi, win in enumerate(POOL_WINDOWS):
        cs = slice(gi * group_dim, (gi + 1) * group_dim)
        ext = jnp.concatenate([halo_ref[:, cs] * keep, x_ref[:, cs]], axis=0)
        acc, span = ext, 1
        while span < win:
            acc = acc + pltpu.roll(acc, span, axis=0)
            span *= 2
        count = jnp.minimum(t_in_seq + 1, win).astype(F32)
        d = acc[halo:, :] / count - ext[halo:, :]
        d_sc[:, cs] = _dot(d.astype(BF16), w_ref[gi])
    y = d_sc[...] * ps_ref[...]
    _ln_epilogue(y, x_ref, g_ref, b_ref, of_ref, ob_ref)


def _pool_mixer_ln(x, w_pool, pool_scale, g, b, seq, *, tm, halo=16):
    m, n = x.shape
    ngrp, group_dim = w_pool.shape[0], w_pool.shape[1]
    assert seq % tm == 0 and tm % halo == 0 and max(POOL_WINDOWS) <= halo
    hb = tm // halo
    row = pl.BlockSpec((tm, n), lambda i: (i, 0))
    vec = pl.BlockSpec((1, n), lambda i: (0, 0))
    return pl.pallas_call(
        functools.partial(_pool_kernel, tm=tm, halo=halo, tiles_per_seq=seq // tm, group_dim=group_dim),
        grid=(m // tm,),
        in_specs=[row, pl.BlockSpec((halo, n), lambda i: (jnp.maximum(i * hb - 1, 0), 0)),
                  pl.BlockSpec(w_pool.shape, lambda i: (0, 0, 0), pipeline_mode=pl.Buffered(1)), vec, vec, vec],
        out_specs=[row, row],
        out_shape=[jax.ShapeDtypeStruct((m, n), F32), jax.ShapeDtypeStruct((m, n), BF16)],
        scratch_shapes=[pltpu.VMEM((tm, n), F32)],
        compiler_params=_params(("parallel",)),
        name="pool_mixer_ln",
    )(x, x, w_pool, pool_scale, g, b)


def _rope_tables(pos):
    inv_freq = 1.0 / (ROPE_THETA ** (jnp.arange(0, HEAD_DIM, 2, dtype=F32) / HEAD_DIM))
    ang = pos.astype(F32)[:, None] * inv_freq[None, :]
    cos, sin = jnp.cos(ang), jnp.sin(ang)
    return jnp.concatenate([cos, cos], axis=-1), jnp.concatenate([-sin, sin], axis=-1)


def _overlap_t(seq):
    n_cmp = (seq - CMP_BLOCK) // CMP_STRIDE + 1
    n_blk = seq // SEL_BLOCK
    c_start = np.arange(n_cmp)[:, None] * CMP_STRIDE
    b_start = np.arange(n_blk)[None, :] * SEL_BLOCK
    ov = np.clip(np.minimum(c_start + CMP_BLOCK, b_start + SEL_BLOCK) - np.maximum(c_start, b_start), 0, None) / CMP_BLOCK
    out = np.zeros((HEAD_DIM, HEAD_DIM), np.float32)
    out[:n_blk, :n_cmp] = ov.T
    return jnp.asarray(out)


def _block_indicator(seq):
    out = np.zeros((seq, HEAD_DIM), np.float32)
    out[np.arange(seq), np.arange(seq) // SEL_BLOCK] = 1.0
    return jnp.asarray(out, BF16)


def _later_sum_matrix(n):
    return jnp.asarray(np.tril(np.ones((n, n), np.float32), -1), BF16)


def _pad_cols(w, n):
    return jnp.pad(w, ((0, 0), (0, n - w.shape[1])))


def _attention_layer(x, w_in, w_out, w_cmp_k, w_cmp_v, pe_k, pe_v, g, b, bsz, seq):
    d = x.shape[1]
    n_cmp = (seq - CMP_BLOCK) // CMP_STRIDE + 1
    n_rows = seq // CMP_STRIDE
    w_heads = jnp.concatenate([w_in[:, :GATE_OFF], w_in[:, GATE_OFF + GATE_DIM:]], axis=1).astype(BF16)
    gate_cols = np.array([[GATE_OFF + c * NSA_HEADS + k * NSA_GROUP + gg for c in range(3) for gg in range(NSA_GROUP)]
                          for k in range(NSA_KV_HEADS)])
    w_gate = jnp.concatenate([_pad_cols(w_in[:, gate_cols[k]], HEAD_DIM) for k in range(NSA_KV_HEADS)], axis=1).astype(BF16)

    cos_t, sin_t = _rope_tables(jnp.arange(seq))
    gates, xb = _gate_proj(x, w_gate, tm=512)
    h = _inproj(xb, w_heads, cos_t, sin_t, seq, tm=1024, tn=1024)

    def blocks16(col0):
        part = h[:, col0 * HEAD_DIM:(col0 + NSA_KV_HEADS) * HEAD_DIM]
        part = part.reshape(bsz, n_rows, CMP_STRIDE, NSA_KV_HEADS, HEAD_DIM).transpose(0, 3, 1, 2, 4)
        return part.reshape(bsz, NSA_KV_HEADS, n_rows, CMP_STRIDE * HEAD_DIM)

    cmp_end = jnp.arange(n_rows) * CMP_STRIDE + CMP_BLOCK - 1
    cos_c, sin_c = _rope_tables(cmp_end)
    flat_pe = lambda pe: jnp.broadcast_to(pe.reshape(1, CMP_BLOCK * HEAD_DIM), (8, CMP_BLOCK * HEAD_DIM)).astype(BF16)
    flat_w = lambda w: w.reshape(CMP_BLOCK * HEAD_DIM, HEAD_DIM).astype(BF16)
    kc, vct = _compress(blocks16(COL_KC), blocks16(COL_VC), flat_w(w_cmp_k), flat_w(w_cmp_v),
                        flat_pe(pe_k), flat_pe(pe_v), cos_c, sin_c, n_cmp)

    o_a = _nsa(h, kc, vct, gates, _overlap_t(seq), _block_indicator(seq), bsz, seq, tq=256, tk=256)
    o_b = _stick_breaking(h, _later_sum_matrix(256), bsz, seq, tq=256, tk=256, heads=8)
    return _matmul_ln([o_a, o_b], w_out.astype(BF16), x, g, b, tm=512, tk=1024, name="out_proj_ln")


def _ffn_layer(x, xb, w_up_all, conv_w, conv_b, w_down_all, layer, g, b, seq, *, tn=512, tk=1024):
    d_ff = w_down_all.shape[1]
    f_pad = -(-d_ff // tk) * tk
    assert f_pad % tn == 0
    split = lambda a: jnp.concatenate([_pad_cols(a[:, :d_ff], f_pad), _pad_cols(a[:, d_ff:], f_pad)], axis=1)
    w_up_p = _stage_ffn_weight(w_up_all, layer, d_ff, f_pad, axis=1)
    conv_w_p = split(conv_w)
    conv_b_p = split(conv_b.reshape(1, -1))
    w_down_p = _stage_ffn_weight(w_down_all, layer, d_ff, f_pad, axis=0)
    gated = _ffn_up(xb, w_up_p, conv_w_p, conv_b_p, seq, d_ff, tm=1024, tn=tn, chunk=256)
    return _matmul_ln([gated], w_down_p, x, g, b, tm=512, tk=tk, name="ffn_down_ln", k_valid=d_ff)


def kernel(x, attn_w_in, attn_w_out, cmp_w_k, cmp_w_v, cmp_pe_k, cmp_pe_v, pool_w, pool_scale, ffn_w_up, ffn_conv_w,
           ffn_conv_b, ffn_w_down, ln_mix_g, ln_mix_b, ln_ffn_g, ln_ffn_b):
    bsz, seq, d = x.shape
    xf = x.reshape(bsz * seq, d)
    xb = None
    vec = lambda a, layer: a[layer].reshape(1, d)
    for layer in range(DEPTH):
        i = layer // 2
        if layer % 2 == 0:
            xf, xb = _attention_layer(xf, attn_w_in[i], attn_w_out[i], cmp_w_k[i], cmp_w_v[i], cmp_pe_k[i],
                                      cmp_pe_v[i], vec(ln_mix_g, layer), vec(ln_mix_b, layer), bsz, seq)
        else:
            xf, xb = _pool_mixer_ln(xf, pool_w[i].astype(BF16), pool_scale[i].reshape(1, d), vec(ln_mix_g, layer),
                                    vec(ln_mix_b, layer), seq, tm=256)
        xf, xb = _ffn_layer(xf, xb, ffn_w_up, ffn_conv_w[layer], ffn_conv_b[layer], ffn_w_down, layer,
                            vec(ln_ffn_g, layer), vec(ln_ffn_b, layer), seq)
    return xf.reshape(bsz, seq, d)
```

```python
import functools

import jax
import jax.numpy as jnp
import numpy as np
from jax import lax
from jax.experimental import pallas as pl
from jax.experimental.pallas import tpu as pltpu

F32 = jnp.float32
BF16 = jnp.bfloat16

HEAD_DIM = 128
NSA_HEADS = 16
NSA_KV_HEADS = 4
NSA_GROUP = 4
SB_HEADS = 16
CMP_BLOCK = 32
CMP_STRIDE = 16
SEL_BLOCK = 64
SEL_TOP_N = 16
WINDOW = 512
ROPE_THETA = 10000.0
POOL_WINDOWS = (2, 4, 8, 16)
CONV_WIDTH = 3
LN_EPS = 1e-5
NEG_INF = -1e30
FORCE_SCORE = 1e9
DEPTH = 2
DEEPNORM_ALPHA = (2 * DEPTH) ** 0.25
SCALE = HEAD_DIM ** -0.5
LOG2E = 1.4426950408889634

COL_Q_A, COL_KC, COL_VC, COL_KS, COL_VS, COL_KW, COL_VW, COL_Q_B, COL_K_B, COL_V_B = 0, 16, 20, 24, 28, 32, 36, 40, 56, 72
N_HEAD_COLS = 88
Q_NSA_DIM = NSA_HEADS * HEAD_DIM
KV_NSA_DIM = NSA_KV_HEADS * HEAD_DIM
GATE_DIM = 3 * NSA_HEADS
GATE_OFF = Q_NSA_DIM + 6 * KV_NSA_DIM

VMEM_LIMIT_BYTES = 56 * 1024 * 1024


def _params(sem, vmem=VMEM_LIMIT_BYTES):
    return pltpu.CompilerParams(dimension_semantics=sem, vmem_limit_bytes=vmem)


def _dot(a, b):
    return jnp.dot(a, b, preferred_element_type=F32)


def _dot_nt(a, b):
    return lax.dot_general(a, b, (((1,), (1,)), ((), ())), preferred_element_type=F32)


def _layer_norm_rows(y, g, b):
    mu = jnp.mean(y, axis=-1, keepdims=True)
    d = y - mu
    var = jnp.mean(d * d, axis=-1, keepdims=True)
    return d * lax.rsqrt(var + LN_EPS) * g + b


def _transpose_bf16(x):
    return x.astype(F32).T.astype(BF16)


def _inproj_kernel(a_ref, b_ref, cos_ref, sin_ref, o_ref, *, heads_per_tile, chunk):
    j = pl.program_id(1)

    def head_scales(head):
        is_rope = (head < COL_KC) | ((head >= COL_KS) & (head < COL_VS)) | ((head >= COL_KW) & (head < COL_VW))
        is_query = (head < COL_KC) | ((head >= COL_Q_B) & (head < COL_K_B))
        q_scale = jnp.where(is_query, SCALE * LOG2E, 1.0).astype(F32)
        return jnp.where(is_rope, q_scale, 0.0), jnp.where(is_rope, 0.0, q_scale)

    scales = [head_scales(j * heads_per_tile + hh) for hh in range(heads_per_tile)]
    for c in range(a_ref.shape[0] // chunk):
        rows = slice(c * chunk, (c + 1) * chunk)
        acc = _dot(a_ref[rows, :], b_ref[...])
        cos_c, sin_c = cos_ref[rows, :], sin_ref[rows, :]
        for hh, (rope_sel, pass_sel) in enumerate(scales):
            blk = acc[:, hh * HEAD_DIM:(hh + 1) * HEAD_DIM]
            rot = pltpu.roll(blk, HEAD_DIM // 2, axis=1)
            out = blk * (cos_c * rope_sel + pass_sel) + rot * (sin_c * rope_sel)
            o_ref[rows, hh * HEAD_DIM:(hh + 1) * HEAD_DIM] = out.astype(o_ref.dtype)


def _inproj(xb, w_heads, cos_t, sin_t, seq, *, tm, tn, chunk=256):
    m, kdim = xb.shape
    n = w_heads.shape[1]
    assert tn % HEAD_DIM == 0 and n % tn == 0 and seq % tm == 0 and tm % chunk == 0
    return pl.pallas_call(
        functools.partial(_inproj_kernel, heads_per_tile=tn // HEAD_DIM, chunk=chunk),
        grid=(m // tm, n // tn),
        in_specs=[
            pl.BlockSpec((tm, kdim), lambda i, j: (i, 0)),
            pl.BlockSpec((kdim, tn), lambda i, j: (0, j)),
            pl.BlockSpec((tm, HEAD_DIM), lambda i, j: (i % (seq // tm), 0)),
            pl.BlockSpec((tm, HEAD_DIM), lambda i, j: (i % (seq // tm), 0)),
        ],
        out_specs=pl.BlockSpec((tm, tn), lambda i, j: (i, j)),
        out_shape=jax.ShapeDtypeStruct((m, n), BF16),
        compiler_params=_params(("parallel", "arbitrary")),
        name="inproj_rope",
    )(xb, w_heads, cos_t, sin_t)


def _gate_kernel(x_ref, b_ref, o_ref, xb_ref):
    xb = x_ref[...].astype(BF16)
    xb_ref[...] = xb
    o_ref[...] = jax.nn.sigmoid(_dot(xb, b_ref[...]))


def _gate_proj(x, w_gate, *, tm):
    m, kdim = x.shape
    n = w_gate.shape[1]
    return pl.pallas_call(
        _gate_kernel,
        grid=(m // tm,),
        in_specs=[pl.BlockSpec((tm, kdim), lambda i: (i, 0)), pl.BlockSpec((kdim, n), lambda i: (0, 0))],
        out_specs=[pl.BlockSpec((tm, n), lambda i: (i, 0)), pl.BlockSpec((tm, kdim), lambda i: (i, 0))],
        out_shape=[jax.ShapeDtypeStruct((m, n), F32), jax.ShapeDtypeStruct((m, kdim), BF16)],
        compiler_params=_params(("parallel",)),
        name="gate_proj",
    )(x, w_gate)


def _compress_kernel(k2_ref, v2_ref, wk_ref, wv_ref, pek_ref, pev_ref, cos_ref, sin_ref, kc_ref, vct_ref, *, n_cmp):
    half = wk_ref.shape[0] // 2
    n_rows = k2_ref.shape[0]
    row = lax.broadcasted_iota(jnp.int32, (n_rows, HEAD_DIM), 0)

    def compress(x2_ref, w_ref, pe_ref):
        x2 = x2_ref[...]
        lo = _dot(x2, w_ref[:half, :])
        hi = _dot(x2, w_ref[half:, :])
        pe_term = _dot(pe_ref[...], w_ref[...])[0:1, :]
        return lo + pltpu.roll(hi, n_rows - 1, axis=0) + pe_term

    kc = compress(k2_ref, wk_ref, pek_ref)
    vc = compress(v2_ref, wv_ref, pev_ref)
    kc = kc * cos_ref[...] + pltpu.roll(kc, HEAD_DIM // 2, axis=1) * sin_ref[...]
    kc_ref[...] = jnp.where(row < n_cmp, kc, 0.0).astype(kc_ref.dtype)
    vct_ref[...] = jnp.where(row < n_cmp, vc, 0.0).T.astype(vct_ref.dtype)


def _compress(k2, v2, wk, wv, pek, pev, cos_c, sin_c, n_cmp):
    bsz, nkv, n_rows, wide = k2.shape
    assert n_rows == HEAD_DIM
    blk4 = pl.BlockSpec((None, None, n_rows, wide), lambda b, k: (b, k, 0, 0))
    full = lambda a: pl.BlockSpec(a.shape, lambda b, k: (0,) * a.ndim)
    out_blk = pl.BlockSpec((None, None, n_rows, HEAD_DIM), lambda b, k: (b, k, 0, 0))
    out_sds = jax.ShapeDtypeStruct((bsz, nkv, n_rows, HEAD_DIM), BF16)
    return pl.pallas_call(
        functools.partial(_compress_kernel, n_cmp=n_cmp),
        grid=(bsz, nkv),
        in_specs=[blk4, blk4, full(wk), full(wv), full(pek), full(pev), full(cos_c), full(sin_c)],
        out_specs=[out_blk, out_blk],
        out_shape=[out_sds, out_sds],
        compiler_params=_params(("parallel", "parallel")),
        name="nsa_compress",
    )(k2, v2, wk, wv, pek, pev, cos_c, sin_c)


def _nsa_kernel(q_ref, kc_ref, vct_ref, ks_ref, vs_ref, kw_ref, vw_ref, g_ref, ovl_ref, eind_ref, o_ref,
                vst_sc, vwt_sc, m_sc, l_sc, acc_sc, *, tq, tk, n_cmp, n_blk):
    qi = pl.program_id(2)
    t0 = qi * tq
    cols = NSA_GROUP * tq

    @pl.when(qi == 0)
    def _():
        for kt in range(vst_sc.shape[0]):
            vst_sc[kt] = _transpose_bf16(vs_ref[kt * tk:(kt + 1) * tk, :])
            vwt_sc[kt] = _transpose_bf16(vw_ref[kt * tk:(kt + 1) * tk, :])

    q = q_ref[...]
    q4 = jnp.concatenate([q[:, g * HEAD_DIM:(g + 1) * HEAD_DIM] for g in range(NSA_GROUP)], axis=0)
    key_minus_query = (lax.broadcasted_iota(jnp.int32, (tk, cols), 0)
                       - (lax.broadcasted_iota(jnp.int32, (tk, cols), 1) & (tq - 1)))

    n_rows = kc_ref.shape[0]
    row = lax.broadcasted_iota(jnp.int32, (n_rows, cols), 0)
    t_abs = t0 + (lax.broadcasted_iota(jnp.int32, (n_rows, cols), 1) & (tq - 1))
    s = _dot_nt(kc_ref[...], q4)
    valid = (row * CMP_STRIDE + (CMP_BLOCK - 1) <= t_abs) & (row < n_cmp)
    s = jnp.where(valid, s, NEG_INF)
    e = jnp.exp2(s - jnp.max(s, axis=0, keepdims=True))
    p = e * (1.0 / jnp.sum(e, axis=0, keepdims=True))
    p = jnp.where(t_abs >= CMP_BLOCK - 1, p, 0.0)
    o_cmp = _dot(vct_ref[...], p.astype(BF16))

    psum = p[:, 0:tq] + p[:, tq:2 * tq] + p[:, 2 * tq:3 * tq] + p[:, 3 * tq:4 * tq]
    imp = jnp.dot(ovl_ref[...], psum, precision=lax.Precision.HIGHEST, preferred_element_type=F32)[0:n_blk, :]
    jrow = lax.broadcasted_iota(jnp.int32, (n_blk, tq), 0)
    t_lane = t0 + lax.broadcasted_iota(jnp.int32, (n_blk, tq), 1)
    cur = t_lane >> 6
    forced = (jrow == 0) | (jrow == cur) | (jrow == cur - 1)
    imp = jnp.where(forced, FORCE_SCORE, jnp.where(jrow * SEL_BLOCK <= t_lane, imp, NEG_INF))
    rank = jnp.zeros((n_blk, tq), F32)
    for jp in range(n_blk):
        other = imp[jp:jp + 1, :]
        ahead = (other > imp) | ((other == imp) & (jrow > jp))
        rank = rank + jnp.where(ahead, 1.0, 0.0)
    bias_t = jnp.where(rank < SEL_TOP_N, 0.0, NEG_INF)
    bias_t = jnp.concatenate([bias_t, jnp.zeros((HEAD_DIM - n_blk, tq), F32)], axis=0)
    bias = bias_t.T.astype(BF16)
    q_aug = jnp.concatenate([q4, jnp.concatenate([bias] * NSA_GROUP, axis=0)], axis=1)

    def win_tile(back):
        kt = qi - back
        kt_c = jnp.maximum(kt, 0)
        sc = _dot_nt(kw_ref[pl.ds(pl.multiple_of(kt_c * tk, tk), tk), :], q4)
        hi, lo = back * tk, back * tk - WINDOW
        if lo >= -(tq - 1):
            lo_dyn = jnp.where(kt >= 0, lo, tk) if back else lo
            sc = jnp.where(key_minus_query > lo_dyn, sc, NEG_INF)
        elif back:
            sc = jnp.where(kt >= 0, sc, NEG_INF)
        if hi < tk - 1:
            sc = jnp.where(key_minus_query <= hi, sc, NEG_INF)
        return sc, vwt_sc[kt_c]

    w_tiles = [win_tile(back) for back in range(WINDOW // tk, -1, -1)]
    m_w = functools.reduce(jnp.maximum, [jnp.max(sc, axis=0, keepdims=True) for sc, _ in w_tiles])
    l_w = jnp.zeros_like(m_w)
    o_win = jnp.zeros((HEAD_DIM, cols), F32)
    for sc, vt in w_tiles:
        pr = jnp.exp2(sc - m_w)
        l_w = l_w + jnp.sum(pr, axis=0, keepdims=True)
        o_win = o_win + _dot(vt, pr.astype(BF16))
    o_win = o_win * (1.0 / l_w)

    def slc_scores(kt, n_tiles=1):
        s0 = pl.multiple_of(kt * tk, tk)
        k_aug = jnp.concatenate([ks_ref[pl.ds(s0, n_tiles * tk), :], eind_ref[pl.ds(s0, n_tiles * tk), :]], axis=1)
        return _dot_nt(k_aug, q_aug)

    def online_step(sc, vt):
        m_old = m_sc[...]
        m_new = jnp.maximum(m_old, jnp.max(sc, axis=0, keepdims=True))
        pr = jnp.exp2(sc - m_new)
        corr = jnp.exp2(m_old - m_new)
        l_sc[...] = corr * l_sc[...] + jnp.sum(pr, axis=0, keepdims=True)
        acc_sc[...] = corr * acc_sc[...] + _dot(vt, pr.astype(BF16))
        m_sc[...] = m_new

    m_sc[...] = jnp.full(m_sc.shape, NEG_INF, F32)
    l_sc[...] = jnp.zeros(l_sc.shape, F32)
    acc_sc[...] = jnp.zeros(acc_sc.shape, F32)

    def slc_pair(pt, carry):
        kt = 2 * pt
        online_step(slc_scores(kt, 2), jnp.concatenate([vst_sc[kt], vst_sc[kt + 1]], axis=1))
        return carry

    lax.fori_loop(0, qi // 2, slc_pair, 0)

    @pl.when(qi % 2 == 1)
    def _():
        online_step(slc_scores(qi - 1), vst_sc[qi - 1])

    online_step(jnp.where(key_minus_query <= 0, slc_scores(qi), NEG_INF), vst_sc[qi])
    o_slc = acc_sc[...] * (1.0 / l_sc[...])

    gates_t = g_ref[...].T
    for g in range(NSA_GROUP):
        sl = slice(g * tq, (g + 1) * tq)
        o = (gates_t[g:g + 1, :] * o_cmp[:, sl]
             + gates_t[NSA_GROUP + g:NSA_GROUP + g + 1, :] * o_slc[:, sl]
             + gates_t[2 * NSA_GROUP + g:2 * NSA_GROUP + g + 1, :] * o_win[:, sl])
        o_ref[:, g * HEAD_DIM:(g + 1) * HEAD_DIM] = o.T.astype(o_ref.dtype)


def _nsa(h, kc, vct, gates, ovl, eind, bsz, seq, *, tq, tk):
    assert tq == tk and seq % tq == 0 and WINDOW % tk == 0
    m = h.shape[0]
    nq = seq // tq
    n_cmp = (seq - CMP_BLOCK) // CMP_STRIDE + 1
    n_blk = seq // SEL_BLOCK
    qw = NSA_GROUP * HEAD_DIM
    cols = NSA_GROUP * tq
    kv_spec = lambda col0: pl.BlockSpec((seq, HEAD_DIM), lambda b, k, i: (b, col0 + k))
    cmp_spec = pl.BlockSpec((None, None, kc.shape[2], HEAD_DIM), lambda b, k, i: (b, k, 0, 0))
    return pl.pallas_call(
        functools.partial(_nsa_kernel, tq=tq, tk=tk, n_cmp=n_cmp, n_blk=n_blk),
        grid=(bsz, NSA_KV_HEADS, nq),
        in_specs=[
            pl.BlockSpec((tq, qw), lambda b, k, i: (b * nq + i, k)),
            cmp_spec, cmp_spec,
            kv_spec(COL_KS), kv_spec(COL_VS), kv_spec(COL_KW), kv_spec(COL_VW),
            pl.BlockSpec((tq, HEAD_DIM), lambda b, k, i: (b * nq + i, k)),
            pl.BlockSpec(ovl.shape, lambda b, k, i: (0, 0)),
            pl.BlockSpec(eind.shape, lambda b, k, i: (0, 0)),
        ],
        out_specs=pl.BlockSpec((tq, qw), lambda b, k, i: (b * nq + i, k)),
        out_shape=jax.ShapeDtypeStruct((m, Q_NSA_DIM), BF16),
        scratch_shapes=[pltpu.VMEM((seq // tk, HEAD_DIM, tk), BF16), pltpu.VMEM((seq // tk, HEAD_DIM, tk), BF16),
                        pltpu.VMEM((1, cols), F32), pltpu.VMEM((1, cols), F32), pltpu.VMEM((HEAD_DIM, cols), F32)],
        compiler_params=_params(("arbitrary", "arbitrary", "arbitrary")),
        name="nsa_attention",
    )(h, kc, vct, h, h, h, h, gates, ovl, eind)


def _sb_kernel(q_ref, k_ref, v_ref, u_ref, o_ref, carry_sc, acc_sc, *, tq, tk, heads):
    qi = pl.program_id(2)
    query_minus_key = lax.broadcasted_iota(jnp.int32, (tq, tk), 0) - lax.broadcasted_iota(jnp.int32, (tq, tk), 1)
    carry_sc[...] = jnp.zeros(carry_sc.shape, F32)
    acc_sc[...] = jnp.zeros(acc_sc.shape, F32)
    u = u_ref[...]

    def tile(kt, diagonal):
        s0 = pl.multiple_of(kt * tk, tk)
        for hh in range(heads):
            cs = slice(hh * HEAD_DIM, (hh + 1) * HEAD_DIM)
            z = _dot_nt(q_ref[:, cs], k_ref[pl.ds(s0, tk), cs])
            sp = jnp.maximum(z, 0.0) + jnp.log2(1.0 + jnp.exp2(-jnp.abs(z)))
            if diagonal:
                mask = query_minus_key > 0
                sp = jnp.where(mask, sp, 0.0)
            sp_b = sp.astype(BF16)
            sums = _dot(sp_b, u)
            carry = carry_sc[hh]
            later = sums + jnp.concatenate([carry] * (tk // HEAD_DIM), axis=1)
            a = jnp.exp2(z - sp - later)
            if diagonal:
                a = jnp.where(mask, a, 0.0)
            acc_sc[hh] += _dot(a.astype(BF16), v_ref[pl.ds(s0, tk), cs])
            row_total = sums[:, 0:1] + sp_b[:, 0:1].astype(F32)
            carry_sc[hh] = carry + jnp.broadcast_to(row_total, carry.shape)

    tile(qi, True)

    def body(it, c):
        tile(qi - 1 - it, False)
        return c

    lax.fori_loop(0, qi, body, 0)
    for hh in range(heads):
        o_ref[:, hh * HEAD_DIM:(hh + 1) * HEAD_DIM] = acc_sc[hh].astype(o_ref.dtype)


def _stick_breaking(h, u_tri, bsz, seq, *, tq, tk, heads):
    assert seq % tq == 0 and tq == tk and u_tri.shape == (tk, tk)
    assert SB_HEADS % heads == 0 and COL_Q_B % heads == 0 and COL_K_B % heads == 0 and COL_V_B % heads == 0
    m = h.shape[0]
    nq = seq // tq
    wide = heads * HEAD_DIM
    return pl.pallas_call(
        functools.partial(_sb_kernel, tq=tq, tk=tk, heads=heads),
        grid=(bsz, SB_HEADS // heads, nq),
        in_specs=[
            pl.BlockSpec((tq, wide), lambda b, hg, i: (b * nq + i, COL_Q_B // heads + hg)),
            pl.BlockSpec((seq, wide), lambda b, hg, i: (b, COL_K_B // heads + hg)),
            pl.BlockSpec((seq, wide), lambda b, hg, i: (b, COL_V_B // heads + hg)),
            pl.BlockSpec(u_tri.shape, lambda b, hg, i: (0, 0)),
        ],
        out_specs=pl.BlockSpec((tq, wide), lambda b, hg, i: (b * nq + i, hg)),
        out_shape=jax.ShapeDtypeStruct((m, SB_HEADS * HEAD_DIM), BF16),
        scratch_shapes=[pltpu.VMEM((heads, tq, HEAD_DIM), F32), pltpu.VMEM((heads, tq, HEAD_DIM), F32)],
        compiler_params=_params(("parallel", "parallel", "arbitrary")),
        name="stick_breaking",
    )(h, h, h, u_tri)


def _ln_epilogue(y, x_ref, g_ref, b_ref, of_ref, ob_ref, rows=slice(None)):
    out = _layer_norm_rows(DEEPNORM_ALPHA * x_ref[rows, :] + y, g_ref[...], b_ref[...])
    of_ref[rows, :] = out
    ob_ref[rows, :] = out.astype(ob_ref.dtype)


def _mm_ln_kernel(*refs, k_splits, chunk, last_k):
    n_a = len(k_splits) - 1
    a_refs = refs[:n_a]
    w_ref, x_hbm, g_ref, b_ref, of_hbm, ob_hbm, acc_sc, xo_sc, ob_sc, sems = refs[n_a:]
    i, k = pl.program_id(0), pl.program_id(1)
    n_tiles, nk = pl.num_programs(0), k_splits[-1]
    tm = acc_sc.shape[0]

    def tile_rows(tile):
        return pl.ds(pl.multiple_of(tile * tm, tm), tm)

    def residual_copy():
        return pltpu.make_async_copy(x_hbm.at[tile_rows(i)], xo_sc, sems.at[0])

    def output_copies(tile):
        return (pltpu.make_async_copy(xo_sc, of_hbm.at[tile_rows(tile)], sems.at[1]),
                pltpu.make_async_copy(ob_sc, ob_hbm.at[tile_rows(tile)], sems.at[2]))

    @pl.when(k == 1)
    def _():
        @pl.when(i > 0)
        def _():
            for cp in output_copies(i - 1):
                cp.wait()

        residual_copy().start()

    for s, a_ref in enumerate(a_refs):
        lo, hi = k_splits[s], k_splits[s + 1]
        first = lo == 0
        last = hi == nk

        if first:
            @pl.when(k == 0)
            def _(a_ref=a_ref):
                acc_sc[...] = _dot(a_ref[...], w_ref[...])

        lo_acc, hi_acc = lo + (1 if first else 0), hi - (1 if last else 0)
        if hi_acc > lo_acc:
            @pl.when((k >= lo_acc) & (k < hi_acc))
            def _(a_ref=a_ref):
                acc_sc[...] += _dot(a_ref[...], w_ref[...])

        if last:
            @pl.when(k == nk - 1)
            def _(a_ref=a_ref):
                residual_copy().wait()
                for c in range(tm // chunk):
                    rows = slice(c * chunk, (c + 1) * chunk)
                    y = acc_sc[rows, :] + _dot(a_ref[rows, :last_k], w_ref[:last_k, :])
                    _ln_epilogue(y, xo_sc, g_ref, b_ref, xo_sc, ob_sc, rows)

                for cp in output_copies(i):
                    cp.start()

                @pl.when(i == n_tiles - 1)
                def _():
                    for cp in output_copies(i):
                        cp.wait()


def _matmul_ln(a_list, w, x, g, b, *, tm, tk, name, chunk=128, k_valid=None):
    m, n = x.shape
    kdim = w.shape[0]
    nk = kdim // tk
    k_splits = [0]
    for a in a_list:
        assert a.shape[1] % tk == 0
        k_splits.append(k_splits[-1] + a.shape[1] // tk)
    assert k_splits[-1] == nk and nk >= 3 and tm % chunk == 0 and m % tm == 0
    last_k = tk if k_valid is None else k_valid - (nk - 1) * tk
    assert 0 < last_k <= tk and last_k % HEAD_DIM == 0
    vec = pl.BlockSpec((1, n), lambda i, k: (0, 0))
    hbm = pl.BlockSpec(memory_space=pl.ANY)

    def a_spec(lo, hi):
        return pl.BlockSpec((tm, tk), lambda i, k: (i, jnp.clip(k - lo, 0, hi - lo - 1)))

    return pl.pallas_call(
        functools.partial(_mm_ln_kernel, k_splits=tuple(k_splits), chunk=chunk, last_k=last_k),
        grid=(m // tm, nk),
        in_specs=[a_spec(k_splits[s], k_splits[s + 1]) for s in range(len(a_list))]
        + [pl.BlockSpec((tk, n), lambda i, k: (k, 0)), hbm, vec, vec],
        out_specs=[hbm, hbm],
        out_shape=[jax.ShapeDtypeStruct((m, n), F32), jax.ShapeDtypeStruct((m, n), BF16)],
        scratch_shapes=[pltpu.VMEM((tm, n), F32), pltpu.VMEM((tm, n), F32), pltpu.VMEM((tm, n), BF16),
                        pltpu.SemaphoreType.DMA((3,))],
        compiler_params=_params(("arbitrary", "arbitrary")),
        name=name,
    )(*a_list, w, x, g, b)


def _stage_tiles_kernel(x_ref, o_ref, *, tiles_in, tiles_out):
    valid = pl.program_id(0) % tiles_out < tiles_in
    o_ref[...] = jnp.where(valid, x_ref[...], 0.0).astype(o_ref.dtype)


def _stage_ffn_weight(w_all, layer, d_ff, f_pad, axis, *, tile=256):
    assert d_ff % tile == 0 and f_pad % tile == 0
    tiles_in, tiles_out = d_ff // tile, f_pad // tile
    shape = list(w_all.shape[1:])
    sections = shape[axis] // d_ff
    src = lambda t: (t // tiles_out) * tiles_in + jnp.minimum(t % tiles_out, tiles_in - 1)
    if axis == 0:
        blk, out_shape = (tile, shape[1]), (sections * f_pad, shape[1])
        in_spec = pl.BlockSpec((None,) + blk, lambda t: (layer, src(t), 0))
        out_spec = pl.BlockSpec(blk, lambda t: (t, 0))
    else:
        blk, out_shape = (shape[0], tile), (shape[0], sections * f_pad)
        in_spec = pl.BlockSpec((None,) + blk, lambda t: (layer, 0, src(t)))
        out_spec = pl.BlockSpec(blk, lambda t: (0, t))
    return pl.pallas_call(
        functools.partial(_stage_tiles_kernel, tiles_in=tiles_in, tiles_out=tiles_out),
        grid=(sections * tiles_out,),
        in_specs=[in_spec],
        out_specs=out_spec,
        out_shape=jax.ShapeDtypeStruct(out_shape, BF16),
        compiler_params=_params(("parallel",)),
        name="stage_ffn_weight",
    )(w_all)


def _ffn_up_kernel(a_ref, halo_ref, wg_ref, wv_ref, cwg_ref, cwv_ref, cbg_ref, cbv_ref, o_ref, head_sc, hg_sc, hv_sc, *,
                   tm, chunks, halo, tiles_per_seq, last_cols):
    i, j = pl.program_id(0), pl.program_id(1)
    tn = o_ref.shape[1]

    @pl.when(j == 0)
    def _():
        seq_start = i % tiles_per_seq == 0
        head_sc[0:halo, :] = jnp.where(seq_start, jnp.zeros_like(halo_ref[...]), halo_ref[...])
        head_sc[halo:, :] = a_ref[...]

    def run(cols):
        def conv(h_sc, size, cw_ref, cb_ref):
            cw = cw_ref[:, :cols]
            out = cb_ref[:, :cols]
            for kk in range(CONV_WIDTH):
                out = out + h_sc[pl.ds(halo - (CONV_WIDTH - 1) + kk, size), :cols] * cw[kk:kk + 1, :]
            return out

        start, prev = 0, 0
        for c, size in enumerate(chunks):
            p = c % 2
            rows = slice(start, start + size)
            if c == 0:
                lhs = head_sc[0:halo + size, :]
                hg_sc[p, 0:halo + size, :cols] = _dot(lhs, wg_ref[:, :cols])
                hv_sc[p, 0:halo + size, :cols] = _dot(lhs, wv_ref[:, :cols])
            else:
                lhs = head_sc[halo + start:halo + start + size, :]
                hg_sc[p, 0:halo, :cols] = hg_sc[1 - p, prev:prev + halo, :cols]
                hv_sc[p, 0:halo, :cols] = hv_sc[1 - p, prev:prev + halo, :cols]
                hg_sc[p, halo:halo + size, :cols] = _dot(lhs, wg_ref[:, :cols])
                hv_sc[p, halo:halo + size, :cols] = _dot(lhs, wv_ref[:, :cols])
            gate = conv(hg_sc.at[p], size, cwg_ref, cbg_ref)
            val = conv(hv_sc.at[p], size, cwv_ref, cbv_ref)
            o_ref[rows, :cols] = (gate * jax.nn.sigmoid(gate) * val).astype(o_ref.dtype)
            start, prev = start + size, size
        if cols < tn:
            o_ref[:, cols:] = jnp.zeros((tm, tn - cols), o_ref.dtype)

    if last_cols == tn:
        run(tn)
    else:
        last = pl.num_programs(1) - 1
        pl.when(j < last)(lambda: run(tn))
        pl.when(j == last)(lambda: run(last_cols))


def _ffn_up(xb, w_up, conv_w, conv_b, seq, d_ff, *, tm, tn, chunks, halo=16):
    m, kdim = xb.shape
    f_pad = w_up.shape[1] // 2
    nj = f_pad // tn
    assert seq % tm == 0 and sum(chunks) == tm and all(c % halo == 0 for c in chunks)
    chunk = max(chunks)
    hb = tm // halo
    last_cols = d_ff - (nj - 1) * tn
    assert 0 < last_cols <= tn and last_cols % HEAD_DIM == 0
    return pl.pallas_call(
        functools.partial(_ffn_up_kernel, tm=tm, chunks=tuple(chunks), halo=halo, tiles_per_seq=seq // tm,
                          last_cols=last_cols),
        grid=(m // tm, nj),
        in_specs=[
            pl.BlockSpec((tm, kdim), lambda i, j: (i, 0)),
            pl.BlockSpec((halo, kdim), lambda i, j: (jnp.maximum(i * hb - 1, 0), 0)),
            pl.BlockSpec((kdim, tn), lambda i, j: (0, j)),
            pl.BlockSpec((kdim, tn), lambda i, j: (0, nj + j)),
            pl.BlockSpec((CONV_WIDTH, tn), lambda i, j: (0, j)),
            pl.BlockSpec((CONV_WIDTH, tn), lambda i, j: (0, nj + j)),
            pl.BlockSpec((1, tn), lambda i, j: (0, j)),
            pl.BlockSpec((1, tn), lambda i, j: (0, nj + j)),
        ],
        out_specs=pl.BlockSpec((tm, tn), lambda i, j: (i, j)),
        out_shape=jax.ShapeDtypeStruct((m, f_pad), BF16),
        scratch_shapes=[pltpu.VMEM((halo + tm, kdim), BF16), pltpu.VMEM((2, halo + chunk, tn), F32),
                        pltpu.VMEM((2, halo + chunk, tn), F32)],
        compiler_params=_params(("parallel", "arbitrary")),
        name="ffn_up_conv_gate",
    )(xb, xb, w_up, w_up, conv_w, conv_w, conv_b, conv_b)


def _pool_kernel(x_ref, halo_ref, w_ref, ps_ref, g_ref, b_ref, of_ref, ob_ref, d_sc, *, tm, halo, tiles_per_seq, group_dim):
    i = pl.program_id(0)
    keep = jnp.where(i % tiles_per_seq == 0, 0.0, 1.0)
    t_in_seq = (i % tiles_per_seq) * tm + lax.broadcasted_iota(jnp.int32, (tm, group_dim), 0)
    for gi, win in enumerate(POOL_WINDOWS):
        cs = slice(gi * group_dim, (gi + 1) * group_dim)
        ext = jnp.concatenate([halo_ref[:, cs] * keep, x_ref[:, cs]], axis=0)
        acc, span = ext, 1
        while span < win:
            acc = acc + pltpu.roll(acc, span, axis=0)
            span *= 2
        count = jnp.minimum(t_in_seq + 1, win).astype(F32)
        d = acc[halo:, :] / count - ext[halo:, :]
        d_sc[:, cs] = _dot(d.astype(BF16), w_ref[gi])
    y = d_sc[...] * ps_ref[...]
    _ln_epilogue(y, x_ref, g_ref, b_ref, of_ref, ob_ref)


def _pool_mixer_ln(x, w_pool, pool_scale, g, b, seq, *, tm, halo=16):
    m, n = x.shape
    ngrp, group_dim = w_pool.shape[0], w_pool.shape[1]
    assert seq % tm == 0 and tm % halo == 0 and max(POOL_WINDOWS) <= halo
    hb = tm // halo
    row = pl.BlockSpec((tm, n), lambda i: (i, 0))
    vec = pl.BlockSpec((1, n), lambda i: (0, 0))
    return pl.pallas_call(
        functools.partial(_pool_kernel, tm=tm, halo=halo, tiles_per_seq=seq // tm, group_dim=group_dim),
        grid=(m // tm,),
        in_specs=[row, pl.BlockSpec((halo, n), lambda i: (jnp.maximum(i * hb - 1, 0), 0)),
                  pl.BlockSpec(w_pool.shape, lambda i: (0, 0, 0), pipeline_mode=pl.Buffered(1)), vec, vec, vec],
        out_specs=[row, row],
        out_shape=[jax.ShapeDtypeStruct((m, n), F32), jax.ShapeDtypeStruct((m, n), BF16)],
        scratch_shapes=[pltpu.VMEM((tm, n), F32)],
        compiler_params=_params(("parallel",)),
        name="pool_mixer_ln",
    )(x, x, w_pool, pool_scale, g, b)


def _rope_tables(pos):
    inv_freq = 1.0 / (ROPE_THETA ** (jnp.arange(0, HEAD_DIM, 2, dtype=F32) / HEAD_DIM))
    ang = pos.astype(F32)[:, None] * inv_freq[None, :]
    cos, sin = jnp.cos(ang), jnp.sin(ang)
    return jnp.concatenate([cos, cos], axis=-1), jnp.concatenate([-sin, sin], axis=-1)


def _overlap_t(seq):
    n_cmp = (seq - CMP_BLOCK) // CMP_STRIDE + 1
    n_blk = seq // SEL_BLOCK
    c_start = np.arange(n_cmp)[:, None] * CMP_STRIDE
    b_start = np.arange(n_blk)[None, :] * SEL_BLOCK
    ov = np.clip(np.minimum(c_start + CMP_BLOCK, b_start + SEL_BLOCK) - np.maximum(c_start, b_start), 0, None) / CMP_BLOCK
    out = np.zeros((HEAD_DIM, HEAD_DIM), np.float32)
    out[:n_blk, :n_cmp] = ov.T
    return jnp.asarray(out)


def _block_indicator(seq):
    out = np.zeros((seq, HEAD_DIM), np.float32)
    out[np.arange(seq), np.arange(seq) // SEL_BLOCK] = 1.0
    return jnp.asarray(out, BF16)


def _later_sum_matrix(n):
    return jnp.asarray(np.tril(np.ones((n, n), np.float32), -1), BF16)


def _pad_cols(w, n):
    return jnp.pad(w, ((0, 0), (0, n - w.shape[1])))


def _attention_layer(x, w_in, w_out, w_cmp_k, w_cmp_v, pe_k, pe_v, g, b, bsz, seq):
    d = x.shape[1]
    n_cmp = (seq - CMP_BLOCK) // CMP_STRIDE + 1
    n_rows = seq // CMP_STRIDE
    w_heads = jnp.concatenate([w_in[:, :GATE_OFF], w_in[:, GATE_OFF + GATE_DIM:]], axis=1).astype(BF16)
    gate_cols = np.array([[GATE_OFF + c * NSA_HEADS + k * NSA_GROUP + gg for c in range(3) for gg in range(NSA_GROUP)]
                          for k in range(NSA_KV_HEADS)])
    w_gate = jnp.concatenate([_pad_cols(w_in[:, gate_cols[k]], HEAD_DIM) for k in range(NSA_KV_HEADS)], axis=1).astype(BF16)

    cos_t, sin_t = _rope_tables(jnp.arange(seq))
    gates, xb = _gate_proj(x, w_gate, tm=512)
    h = _inproj(xb, w_heads, cos_t, sin_t, seq, tm=1024, tn=1024)

    def blocks16(col0):
        part = h[:, col0 * HEAD_DIM:(col0 + NSA_KV_HEADS) * HEAD_DIM]
        part = part.reshape(bsz, n_rows, CMP_STRIDE, NSA_KV_HEADS, HEAD_DIM).transpose(0, 3, 1, 2, 4)
        return part.reshape(bsz, NSA_KV_HEADS, n_rows, CMP_STRIDE * HEAD_DIM)

    cmp_end = jnp.arange(n_rows) * CMP_STRIDE + CMP_BLOCK - 1
    cos_c, sin_c = _rope_tables(cmp_end)
    flat_pe = lambda pe: jnp.broadcast_to(pe.reshape(1, CMP_BLOCK * HEAD_DIM), (8, CMP_BLOCK * HEAD_DIM)).astype(BF16)
    flat_w = lambda w: w.reshape(CMP_BLOCK * HEAD_DIM, HEAD_DIM).astype(BF16)
    kc, vct = _compress(blocks16(COL_KC), blocks16(COL_VC), flat_w(w_cmp_k), flat_w(w_cmp_v),
                        flat_pe(pe_k), flat_pe(pe_v), cos_c, sin_c, n_cmp)

    o_a = _nsa(h, kc, vct, gates, _overlap_t(seq), _block_indicator(seq), bsz, seq, tq=256, tk=256)
    o_b = _stick_breaking(h, _later_sum_matrix(256), bsz, seq, tq=256, tk=256, heads=8)
    return _matmul_ln([o_a, o_b], w_out.astype(BF16), x, g, b, tm=512, tk=1024, name="out_proj_ln")


def _ffn_layer(x, xb, w_up_all, conv_w, conv_b, w_down_all, layer, g, b, seq, *, tn=512, tk=1024):
    d_ff = w_down_all.shape[1]
    f_pad = -(-d_ff // tk) * tk
    assert f_pad % tn == 0
    split = lambda a: jnp.concatenate([_pad_cols(a[:, :d_ff], f_pad), _pad_cols(a[:, d_ff:], f_pad)], axis=1)
    w_up_p = _stage_ffn_weight(w_up_all, layer, d_ff, f_pad, axis=1)
    conv_w_p = split(conv_w)
    conv_b_p = split(conv_b.reshape(1, -1))
    w_down_p = _stage_ffn_weight(w_down_all, layer, d_ff, f_pad, axis=0)
    gated = _ffn_up(xb, w_up_p, conv_w_p, conv_b_p, seq, d_ff, tm=1024, tn=tn, chunks=(256, 256, 256, 128, 128))
    return _matmul_ln([gated], w_down_p, x, g, b, tm=512, tk=tk, name="ffn_down_ln", k_valid=d_ff)


def kernel(x, attn_w_in, attn_w_out, cmp_w_k, cmp_w_v, cmp_pe_k, cmp_pe_v, pool_w, pool_scale, ffn_w_up, ffn_conv_w,
           ffn_conv_b, ffn_w_down, ln_mix_g, ln_mix_b, ln_ffn_g, ln_ffn_b):
    bsz, seq, d = x.shape
    xf = x.reshape(bsz * seq, d)
    xb = None
    vec = lambda a, layer: a[layer].reshape(1, d)
    for layer in range(DEPTH):
        i = layer // 2
        if layer % 2 == 0:
            xf, xb = _attention_layer(xf, attn_w_in[i], attn_w_out[i], cmp_w_k[i], cmp_w_v[i], cmp_pe_k[i],
                                      cmp_pe_v[i], vec(ln_mix_g, layer), vec(ln_mix_b, layer), bsz, seq)
        else:
            xf, xb = _pool_mixer_ln(xf, pool_w[i].astype(BF16), pool_scale[i].reshape(1, d), vec(ln_mix_g, layer),
                                    vec(ln_mix_b, layer), seq, tm=256)
        xf, xb = _ffn_layer(xf, xb, ffn_w_up, ffn_conv_w[layer], ffn_conv_b[layer], ffn_w_down, layer,
                            vec(ln_ffn_g, layer), vec(ln_ffn_b, layer), seq)
    return xf.reshape(bsz, seq, d)
```

```python
import functools

import jax
import jax.numpy as jnp
import numpy as np
from jax import lax
from jax.experimental import pallas as pl
from jax.experimental.pallas import tpu as pltpu

F32 = jnp.float32
BF16 = jnp.bfloat16

HEAD_DIM = 128
NSA_HEADS = 16
NSA_KV_HEADS = 4
NSA_GROUP = 4
SB_HEADS = 16
CMP_BLOCK = 32
CMP_STRIDE = 16
SEL_BLOCK = 64
SEL_TOP_N = 16
WINDOW = 512
ROPE_THETA = 10000.0
POOL_WINDOWS = (2, 4, 8, 16)
CONV_WIDTH = 3
LN_EPS = 1e-5
NEG_INF = -1e30
FORCE_SCORE = 1e9
DEPTH = 2
DEEPNORM_ALPHA = (2 * DEPTH) ** 0.25
SCALE = HEAD_DIM ** -0.5
LOG2E = 1.4426950408889634

COL_Q_A, COL_KC, COL_VC, COL_KS, COL_VS, COL_KW, COL_VW, COL_Q_B, COL_K_B, COL_V_B = 0, 16, 20, 24, 28, 32, 36, 40, 56, 72
Q_NSA_DIM = NSA_HEADS * HEAD_DIM
KV_NSA_DIM = NSA_KV_HEADS * HEAD_DIM
GATE_DIM = 3 * NSA_HEADS
GATE_OFF = Q_NSA_DIM + 6 * KV_NSA_DIM

VMEM_LIMIT_BYTES = 56 * 1024 * 1024


PROJ_ROWS, PROJ_COLS = 1024, 1024
GATE_ROWS = 512
ATTN_TILE = 256
SB_HEADS_PER_STEP = 16
LN_ROWS, LN_K = 512, 1024
FFN_ROWS, FFN_COLS = 1024, 512
FFN_ROW_CHUNKS = (256, 256, 256, 128, 128)
POOL_ROWS = 256


def _params(sem, vmem=VMEM_LIMIT_BYTES):
    return pltpu.CompilerParams(dimension_semantics=sem, vmem_limit_bytes=vmem)


def _dot(a, b):
    return jnp.dot(a, b, preferred_element_type=F32)


def _dot_nt(a, b):
    return lax.dot_general(a, b, (((1,), (1,)), ((), ())), preferred_element_type=F32)


def _layer_norm_rows(y, g, b):
    mu = jnp.mean(y, axis=-1, keepdims=True)
    d = y - mu
    var = jnp.mean(d * d, axis=-1, keepdims=True)
    return d * lax.rsqrt(var + LN_EPS) * g + b


def _transpose_bf16(x):
    return x.astype(F32).T.astype(BF16)


def _inproj_kernel(a_ref, b_ref, cos_ref, sin_ref, o_ref, *, head_base, heads_per_tile, chunk):
    j = pl.program_id(1)

    def head_scales(head):
        is_rope = (head < COL_KC) | ((head >= COL_KS) & (head < COL_VS)) | ((head >= COL_KW) & (head < COL_VW))
        is_query = (head < COL_KC) | ((head >= COL_Q_B) & (head < COL_K_B))
        q_scale = jnp.where(is_query, SCALE * LOG2E, 1.0).astype(F32)
        return jnp.where(is_rope, q_scale, 0.0), jnp.where(is_rope, 0.0, q_scale)

    scales = [head_scales(head_base + j * heads_per_tile + hh) for hh in range(heads_per_tile)]
    for c in range(a_ref.shape[0] // chunk):
        rows = slice(c * chunk, (c + 1) * chunk)
        acc = _dot(a_ref[rows, :], b_ref[...])
        cos_c, sin_c = cos_ref[rows, :], sin_ref[rows, :]
        for hh, (rope_sel, pass_sel) in enumerate(scales):
            blk = acc[:, hh * HEAD_DIM:(hh + 1) * HEAD_DIM]
            rot = pltpu.roll(blk, HEAD_DIM // 2, axis=1)
            out = blk * (cos_c * rope_sel + pass_sel) + rot * (sin_c * rope_sel)
            o_ref[rows, hh * HEAD_DIM:(hh + 1) * HEAD_DIM] = out.astype(o_ref.dtype)


def _inproj(xb, w_heads, cos_t, sin_t, seq, *, head_base, tm, tn, chunk=256):
    m, kdim = xb.shape
    n = w_heads.shape[1]
    assert tn % HEAD_DIM == 0 and n % tn == 0 and seq % tm == 0 and tm % chunk == 0
    return pl.pallas_call(
        functools.partial(_inproj_kernel, head_base=head_base, heads_per_tile=tn // HEAD_DIM, chunk=chunk),
        grid=(m // tm, n // tn),
        in_specs=[
            pl.BlockSpec((tm, kdim), lambda i, j: (i, 0)),
            pl.BlockSpec((kdim, tn), lambda i, j: (0, j)),
            pl.BlockSpec((tm, HEAD_DIM), lambda i, j: (i % (seq // tm), 0)),
            pl.BlockSpec((tm, HEAD_DIM), lambda i, j: (i % (seq // tm), 0)),
        ],
        out_specs=pl.BlockSpec((tm, tn), lambda i, j: (i, j)),
        out_shape=jax.ShapeDtypeStruct((m, n), BF16),
        compiler_params=_params(("parallel", "arbitrary")),
        name="inproj_rope",
    )(xb, w_heads, cos_t, sin_t)


def _gate_kernel(x_ref, b_ref, o_ref, xb_ref):
    xb = x_ref[...].astype(BF16)
    xb_ref[...] = xb
    o_ref[...] = jax.nn.sigmoid(_dot(xb, b_ref[...]))


def _gate_proj(x, w_gate, *, tm):
    m, kdim = x.shape
    n = w_gate.shape[1]
    return pl.pallas_call(
        _gate_kernel,
        grid=(m // tm,),
        in_specs=[pl.BlockSpec((tm, kdim), lambda i: (i, 0)), pl.BlockSpec((kdim, n), lambda i: (0, 0))],
        out_specs=[pl.BlockSpec((tm, n), lambda i: (i, 0)), pl.BlockSpec((tm, kdim), lambda i: (i, 0))],
        out_shape=[jax.ShapeDtypeStruct((m, n), F32), jax.ShapeDtypeStruct((m, kdim), BF16)],
        compiler_params=_params(("parallel",)),
        name="gate_proj",
    )(x, w_gate)


def _compress_kernel(k2_ref, v2_ref, wk_ref, wv_ref, pek_ref, pev_ref, cos_ref, sin_ref, kc_ref, vct_ref, *, n_cmp):
    half = wk_ref.shape[0] // 2
    n_rows = k2_ref.shape[0]
    row = lax.broadcasted_iota(jnp.int32, (n_rows, HEAD_DIM), 0)

    def compress(x2_ref, w_ref, pe_ref):
        x2 = x2_ref[...]
        lo = _dot(x2, w_ref[:half, :])
        hi = _dot(x2, w_ref[half:, :])
        pe_term = _dot(pe_ref[...], w_ref[...])[0:1, :]
        return lo + pltpu.roll(hi, n_rows - 1, axis=0) + pe_term

    kc = compress(k2_ref, wk_ref, pek_ref)
    vc = compress(v2_ref, wv_ref, pev_ref)
    kc = kc * cos_ref[...] + pltpu.roll(kc, HEAD_DIM // 2, axis=1) * sin_ref[...]
    kc_ref[...] = jnp.where(row < n_cmp, kc, 0.0).astype(kc_ref.dtype)
    vct_ref[...] = jnp.where(row < n_cmp, vc, 0.0).T.astype(vct_ref.dtype)


def _compress(k2, v2, wk, wv, pek, pev, cos_c, sin_c, n_cmp):
    bsz, nkv, n_rows, wide = k2.shape
    assert n_rows == HEAD_DIM
    blk4 = pl.BlockSpec((None, None, n_rows, wide), lambda b, k: (b, k, 0, 0))
    full = lambda a: pl.BlockSpec(a.shape, lambda b, k: (0,) * a.ndim)
    out_blk = pl.BlockSpec((None, None, n_rows, HEAD_DIM), lambda b, k: (b, k, 0, 0))
    out_sds = jax.ShapeDtypeStruct((bsz, nkv, n_rows, HEAD_DIM), BF16)
    return pl.pallas_call(
        functools.partial(_compress_kernel, n_cmp=n_cmp),
        grid=(bsz, nkv),
        in_specs=[blk4, blk4, full(wk), full(wv), full(pek), full(pev), full(cos_c), full(sin_c)],
        out_specs=[out_blk, out_blk],
        out_shape=[out_sds, out_sds],
        compiler_params=_params(("parallel", "parallel")),
        name="nsa_compress",
    )(k2, v2, wk, wv, pek, pev, cos_c, sin_c)


def _nsa_kernel(q_ref, kc_ref, vct_ref, ks_ref, vs_ref, kw_ref, vw_ref, g_ref, ovl_ref, eind_ref, o_ref,
                vst_sc, vwt_sc, m_sc, l_sc, acc_sc, *, tq, tk, n_cmp, n_blk):
    qi = pl.program_id(2)
    t0 = qi * tq
    cols = NSA_GROUP * tq

    @pl.when(qi == 0)
    def _():
        for kt in range(vst_sc.shape[0]):
            vst_sc[kt] = _transpose_bf16(vs_ref[kt * tk:(kt + 1) * tk, :])
            vwt_sc[kt] = _transpose_bf16(vw_ref[kt * tk:(kt + 1) * tk, :])

    q = q_ref[...]
    q4 = jnp.concatenate([q[:, g * HEAD_DIM:(g + 1) * HEAD_DIM] for g in range(NSA_GROUP)], axis=0)
    key_minus_query = (lax.broadcasted_iota(jnp.int32, (tk, cols), 0)
                       - (lax.broadcasted_iota(jnp.int32, (tk, cols), 1) & (tq - 1)))

    n_rows = kc_ref.shape[0]
    row = lax.broadcasted_iota(jnp.int32, (n_rows, cols), 0)
    t_abs = t0 + (lax.broadcasted_iota(jnp.int32, (n_rows, cols), 1) & (tq - 1))
    s = _dot_nt(kc_ref[...], q4)
    valid = (row * CMP_STRIDE + (CMP_BLOCK - 1) <= t_abs) & (row < n_cmp)
    s = jnp.where(valid, s, NEG_INF)
    e = jnp.exp2(s - jnp.max(s, axis=0, keepdims=True))
    p = e * (1.0 / jnp.sum(e, axis=0, keepdims=True))
    p = jnp.where(t_abs >= CMP_BLOCK - 1, p, 0.0)
    o_cmp = _dot(vct_ref[...], p.astype(BF16))

    psum = p[:, 0:tq] + p[:, tq:2 * tq] + p[:, 2 * tq:3 * tq] + p[:, 3 * tq:4 * tq]
    imp = jnp.dot(ovl_ref[...], psum, precision=lax.Precision.HIGHEST, preferred_element_type=F32)[0:n_blk, :]
    jrow = lax.broadcasted_iota(jnp.int32, (n_blk, tq), 0)
    t_lane = t0 + lax.broadcasted_iota(jnp.int32, (n_blk, tq), 1)
    cur = t_lane >> 6
    forced = (jrow == 0) | (jrow == cur) | (jrow == cur - 1)
    imp = jnp.where(forced, FORCE_SCORE, jnp.where(jrow * SEL_BLOCK <= t_lane, imp, NEG_INF))
    rank = jnp.zeros((n_blk, tq), F32)
    for jp in range(n_blk):
        other = imp[jp:jp + 1, :]
        ahead = (other > imp) | ((other == imp) & (jrow > jp))
        rank = rank + jnp.where(ahead, 1.0, 0.0)
    bias_t = jnp.where(rank < SEL_TOP_N, 0.0, NEG_INF)
    bias_t = jnp.concatenate([bias_t, jnp.zeros((HEAD_DIM - n_blk, tq), F32)], axis=0)
    bias = bias_t.T.astype(BF16)
    q_aug = jnp.concatenate([q4, jnp.concatenate([bias] * NSA_GROUP, axis=0)], axis=1)

    def win_tile(back):
        kt = qi - back
        kt_c = jnp.maximum(kt, 0)
        sc = _dot_nt(kw_ref[pl.ds(pl.multiple_of(kt_c * tk, tk), tk), :], q4)
        hi, lo = back * tk, back * tk - WINDOW
        if lo >= -(tq - 1):
            lo_dyn = jnp.where(kt >= 0, lo, tk) if back else lo
            sc = jnp.where(key_minus_query > lo_dyn, sc, NEG_INF)
        elif back:
            sc = jnp.where(kt >= 0, sc, NEG_INF)
        if hi < tk - 1:
            sc = jnp.where(key_minus_query <= hi, sc, NEG_INF)
        return sc, vwt_sc[kt_c]

    w_tiles = [win_tile(back) for back in range(WINDOW // tk, -1, -1)]
    m_w = functools.reduce(jnp.maximum, [jnp.max(sc, axis=0, keepdims=True) for sc, _ in w_tiles])
    l_w = jnp.zeros_like(m_w)
    o_win = jnp.zeros((HEAD_DIM, cols), F32)
    for sc, vt in w_tiles:
        pr = jnp.exp2(sc - m_w)
        l_w = l_w + jnp.sum(pr, axis=0, keepdims=True)
        o_win = o_win + _dot(vt, pr.astype(BF16))
    o_win = o_win * (1.0 / l_w)

    def slc_scores(kt, n_tiles=1):
        s0 = pl.multiple_of(kt * tk, tk)
        k_aug = jnp.concatenate([ks_ref[pl.ds(s0, n_tiles * tk), :], eind_ref[pl.ds(s0, n_tiles * tk), :]], axis=1)
        return _dot_nt(k_aug, q_aug)

    def online_step(sc, vt):
        m_old = m_sc[...]
        m_new = jnp.maximum(m_old, jnp.max(sc, axis=0, keepdims=True))
        pr = jnp.exp2(sc - m_new)
        corr = jnp.exp2(m_old - m_new)
        l_sc[...] = corr * l_sc[...] + jnp.sum(pr, axis=0, keepdims=True)
        acc_sc[...] = corr * acc_sc[...] + _dot(vt, pr.astype(BF16))
        m_sc[...] = m_new

    m_sc[...] = jnp.full(m_sc.shape, NEG_INF, F32)
    l_sc[...] = jnp.zeros(l_sc.shape, F32)
    acc_sc[...] = jnp.zeros(acc_sc.shape, F32)

    def slc_pair(pt, carry):
        kt = 2 * pt
        online_step(slc_scores(kt, 2), jnp.concatenate([vst_sc[kt], vst_sc[kt + 1]], axis=1))
        return carry

    lax.fori_loop(0, qi // 2, slc_pair, 0)

    @pl.when(qi % 2 == 1)
    def _():
        online_step(slc_scores(qi - 1), vst_sc[qi - 1])

    online_step(jnp.where(key_minus_query <= 0, slc_scores(qi), NEG_INF), vst_sc[qi])
    o_slc = acc_sc[...] * (1.0 / l_sc[...])

    gates_t = g_ref[...].T
    for g in range(NSA_GROUP):
        sl = slice(g * tq, (g + 1) * tq)
        o = (gates_t[g:g + 1, :] * o_cmp[:, sl]
             + gates_t[NSA_GROUP + g:NSA_GROUP + g + 1, :] * o_slc[:, sl]
             + gates_t[2 * NSA_GROUP + g:2 * NSA_GROUP + g + 1, :] * o_win[:, sl])
        o_ref[:, g * HEAD_DIM:(g + 1) * HEAD_DIM] = o.T.astype(o_ref.dtype)


def _nsa(h, kc, vct, gates, ovl, eind, bsz, seq, *, tq, tk):
    assert tq == tk and seq % tq == 0 and WINDOW % tk == 0
    m = h.shape[0]
    nq = seq // tq
    n_cmp = (seq - CMP_BLOCK) // CMP_STRIDE + 1
    n_blk = seq // SEL_BLOCK
    qw = NSA_GROUP * HEAD_DIM
    cols = NSA_GROUP * tq
    kv_spec = lambda col0: pl.BlockSpec((seq, HEAD_DIM), lambda b, k, i: (b, col0 + k))
    cmp_spec = pl.BlockSpec((None, None, kc.shape[2], HEAD_DIM), lambda b, k, i: (b, k, 0, 0))
    return pl.pallas_call(
        functools.partial(_nsa_kernel, tq=tq, tk=tk, n_cmp=n_cmp, n_blk=n_blk),
        grid=(bsz, NSA_KV_HEADS, nq),
        in_specs=[
            pl.BlockSpec((tq, qw), lambda b, k, i: (b * nq + i, k)),
            cmp_spec, cmp_spec,
            kv_spec(COL_KS), kv_spec(COL_VS), kv_spec(COL_KW), kv_spec(COL_VW),
            pl.BlockSpec((tq, HEAD_DIM), lambda b, k, i: (b * nq + i, k)),
            pl.BlockSpec(ovl.shape, lambda b, k, i: (0, 0)),
            pl.BlockSpec(eind.shape, lambda b, k, i: (0, 0)),
        ],
        out_specs=pl.BlockSpec((tq, qw), lambda b, k, i: (b * nq + i, k)),
        out_shape=jax.ShapeDtypeStruct((m, Q_NSA_DIM), BF16),
        scratch_shapes=[pltpu.VMEM((seq // tk, HEAD_DIM, tk), BF16), pltpu.VMEM((seq // tk, HEAD_DIM, tk), BF16),
                        pltpu.VMEM((1, cols), F32), pltpu.VMEM((1, cols), F32), pltpu.VMEM((HEAD_DIM, cols), F32)],
        compiler_params=_params(("arbitrary", "arbitrary", "arbitrary")),
        name="nsa_attention",
    )(h, kc, vct, h, h, h, h, gates, ovl, eind)


def _sb_kernel(q_ref, k_ref, v_ref, u_ref, o_ref, carry_sc, acc_sc, *, tq, tk, heads):
    qi = pl.program_id(2)
    query_minus_key = lax.broadcasted_iota(jnp.int32, (tq, tk), 0) - lax.broadcasted_iota(jnp.int32, (tq, tk), 1)
    carry_sc[...] = jnp.zeros(carry_sc.shape, F32)
    acc_sc[...] = jnp.zeros(acc_sc.shape, F32)
    u = u_ref[...]

    def tile(kt, diagonal):
        s0 = pl.multiple_of(kt * tk, tk)
        for hh in range(heads):
            cs = slice(hh * HEAD_DIM, (hh + 1) * HEAD_DIM)
            z = _dot_nt(q_ref[:, cs], k_ref[pl.ds(s0, tk), cs])
            sp = jnp.maximum(z, 0.0) + jnp.log2(1.0 + jnp.exp2(-jnp.abs(z)))
            if diagonal:
                mask = query_minus_key > 0
                sp = jnp.where(mask, sp, 0.0)
            sp_b = sp.astype(BF16)
            sums = _dot(sp_b, u)
            carry = carry_sc[hh]
            later = sums + jnp.concatenate([carry] * (tk // HEAD_DIM), axis=1)
            a = jnp.exp2(z - sp - later)
            if diagonal:
                a = jnp.where(mask, a, 0.0)
            acc_sc[hh] += _dot(a.astype(BF16), v_ref[pl.ds(s0, tk), cs])
            row_total = sums[:, 0:1] + sp_b[:, 0:1].astype(F32)
            carry_sc[hh] = carry + jnp.broadcast_to(row_total, carry.shape)

    tile(qi, True)

    def body(it, c):
        tile(qi - 1 - it, False)
        return c

    lax.fori_loop(0, qi, body, 0)
    for hh in range(heads):
        o_ref[:, hh * HEAD_DIM:(hh + 1) * HEAD_DIM] = acc_sc[hh].astype(o_ref.dtype)


def _stick_breaking(h, u_tri, bsz, seq, *, tq, tk, heads):
    assert seq % tq == 0 and tq == tk and u_tri.shape == (tk, tk)
    assert SB_HEADS % heads == 0 and h.shape[1] == 3 * SB_HEADS * HEAD_DIM
    groups = SB_HEADS // heads
    m = h.shape[0]
    nq = seq // tq
    wide = heads * HEAD_DIM
    return pl.pallas_call(
        functools.partial(_sb_kernel, tq=tq, tk=tk, heads=heads),
        grid=(bsz, groups, nq),
        in_specs=[
            pl.BlockSpec((tq, wide), lambda b, hg, i: (b * nq + i, hg)),
            pl.BlockSpec((seq, wide), lambda b, hg, i: (b, groups + hg)),
            pl.BlockSpec((seq, wide), lambda b, hg, i: (b, 2 * groups + hg)),
            pl.BlockSpec(u_tri.shape, lambda b, hg, i: (0, 0)),
        ],
        out_specs=pl.BlockSpec((tq, wide), lambda b, hg, i: (b * nq + i, hg)),
        out_shape=jax.ShapeDtypeStruct((m, SB_HEADS * HEAD_DIM), BF16),
        scratch_shapes=[pltpu.VMEM((heads, tq, HEAD_DIM), F32), pltpu.VMEM((heads, tq, HEAD_DIM), F32)],
        compiler_params=_params(("parallel", "parallel", "arbitrary")),
        name="stick_breaking",
    )(h, h, h, u_tri)


def _ln_epilogue(y, x_ref, g_ref, b_ref, of_ref, ob_ref, rows=slice(None)):
    out = _layer_norm_rows(DEEPNORM_ALPHA * x_ref[rows, :] + y, g_ref[...], b_ref[...])
    of_ref[rows, :] = out
    ob_ref[rows, :] = out.astype(ob_ref.dtype)


def _mm_ln_kernel(*refs, k_splits, chunk, last_k):
    n_a = len(k_splits) - 1
    a_refs = refs[:n_a]
    w_ref, x_hbm, g_ref, b_ref, of_hbm, ob_hbm, acc_sc, xo_sc, ob_sc, sems = refs[n_a:]
    i, k = pl.program_id(0), pl.program_id(1)
    n_tiles, nk = pl.num_programs(0), k_splits[-1]
    tm = acc_sc.shape[0]

    def tile_rows(tile):
        return pl.ds(pl.multiple_of(tile * tm, tm), tm)

    def residual_copy():
        return pltpu.make_async_copy(x_hbm.at[tile_rows(i)], xo_sc, sems.at[0])

    def output_copies(tile):
        return (pltpu.make_async_copy(xo_sc, of_hbm.at[tile_rows(tile)], sems.at[1]),
                pltpu.make_async_copy(ob_sc, ob_hbm.at[tile_rows(tile)], sems.at[2]))

    @pl.when(k == 1)
    def _():
        @pl.when(i > 0)
        def _():
            for cp in output_copies(i - 1):
                cp.wait()

        residual_copy().start()

    for s, a_ref in enumerate(a_refs):
        lo, hi = k_splits[s], k_splits[s + 1]
        first = lo == 0
        last = hi == nk

        if first:
            @pl.when(k == 0)
            def _(a_ref=a_ref):
                acc_sc[...] = _dot(a_ref[...], w_ref[...])

        lo_acc, hi_acc = lo + (1 if first else 0), hi - (1 if last else 0)
        if hi_acc > lo_acc:
            @pl.when((k >= lo_acc) & (k < hi_acc))
            def _(a_ref=a_ref):
                acc_sc[...] += _dot(a_ref[...], w_ref[...])

        if last:
            @pl.when(k == nk - 1)
            def _(a_ref=a_ref):
                residual_copy().wait()
                for c in range(tm // chunk):
                    rows = slice(c * chunk, (c + 1) * chunk)
                    y = acc_sc[rows, :] + _dot(a_ref[rows, :last_k], w_ref[:last_k, :])
                    _ln_epilogue(y, xo_sc, g_ref, b_ref, xo_sc, ob_sc, rows)

                for cp in output_copies(i):
                    cp.start()

                @pl.when(i == n_tiles - 1)
                def _():
                    for cp in output_copies(i):
                        cp.wait()


def _matmul_ln(a_list, w, x, g, b, *, tm, tk, name, chunk=128, k_valid=None):
    m, n = x.shape
    kdim = w.shape[0]
    nk = kdim // tk
    k_splits = [0]
    for a in a_list:
        assert a.shape[1] % tk == 0
        k_splits.append(k_splits[-1] + a.shape[1] // tk)
    assert k_splits[-1] == nk and nk >= 3 and tm % chunk == 0 and m % tm == 0
    last_k = tk if k_valid is None else k_valid - (nk - 1) * tk
    assert 0 < last_k <= tk and last_k % HEAD_DIM == 0
    vec = pl.BlockSpec((1, n), lambda i, k: (0, 0))
    hbm = pl.BlockSpec(memory_space=pl.ANY)

    def a_spec(lo, hi):
        return pl.BlockSpec((tm, tk), lambda i, k: (i, jnp.clip(k - lo, 0, hi - lo - 1)))

    return pl.pallas_call(
        functools.partial(_mm_ln_kernel, k_splits=tuple(k_splits), chunk=chunk, last_k=last_k),
        grid=(m // tm, nk),
        in_specs=[a_spec(k_splits[s], k_splits[s + 1]) for s in range(len(a_list))]
        + [pl.BlockSpec((tk, n), lambda i, k: (k, 0)), hbm, vec, vec],
        out_specs=[hbm, hbm],
        out_shape=[jax.ShapeDtypeStruct((m, n), F32), jax.ShapeDtypeStruct((m, n), BF16)],
        scratch_shapes=[pltpu.VMEM((tm, n), F32), pltpu.VMEM((tm, n), F32), pltpu.VMEM((tm, n), BF16),
                        pltpu.SemaphoreType.DMA((3,))],
        compiler_params=_params(("arbitrary", "arbitrary")),
        name=name,
    )(*a_list, w, x, g, b)


def _stage_tiles_kernel(x_ref, o_ref, *, tiles_in, tiles_out):
    valid = pl.program_id(0) % tiles_out < tiles_in
    o_ref[...] = jnp.where(valid, x_ref[...], 0.0).astype(o_ref.dtype)


def _stage_ffn_weight(w_all, layer, d_ff, f_pad, axis, *, tile=256):
    assert d_ff % tile == 0 and f_pad % tile == 0
    tiles_in, tiles_out = d_ff // tile, f_pad // tile
    shape = list(w_all.shape[1:])
    sections = shape[axis] // d_ff
    src = lambda t: (t // tiles_out) * tiles_in + jnp.minimum(t % tiles_out, tiles_in - 1)
    if axis == 0:
        blk, out_shape = (tile, shape[1]), (sections * f_pad, shape[1])
        in_spec = pl.BlockSpec((None,) + blk, lambda t: (layer, src(t), 0))
        out_spec = pl.BlockSpec(blk, lambda t: (t, 0))
    else:
        blk, out_shape = (shape[0], tile), (shape[0], sections * f_pad)
        in_spec = pl.BlockSpec((None,) + blk, lambda t: (layer, 0, src(t)))
        out_spec = pl.BlockSpec(blk, lambda t: (0, t))
    return pl.pallas_call(
        functools.partial(_stage_tiles_kernel, tiles_in=tiles_in, tiles_out=tiles_out),
        grid=(sections * tiles_out,),
        in_specs=[in_spec],
        out_specs=out_spec,
        out_shape=jax.ShapeDtypeStruct(out_shape, BF16),
        compiler_params=_params(("parallel",)),
        name="stage_ffn_weight",
    )(w_all)


def _ffn_up_kernel(a_ref, halo_ref, wg_ref, wv_ref, cwg_ref, cwv_ref, cbg_ref, cbv_ref, o_ref, head_sc, hg_sc, hv_sc, *,
                   tm, chunks, halo, tiles_per_seq, last_cols):
    i, j = pl.program_id(0), pl.program_id(1)
    tn = o_ref.shape[1]

    @pl.when(j == 0)
    def _():
        seq_start = i % tiles_per_seq == 0
        head_sc[0:halo, :] = jnp.where(seq_start, jnp.zeros_like(halo_ref[...]), halo_ref[...])
        head_sc[halo:, :] = a_ref[...]

    def run(cols):
        def conv(h_sc, size, cw_ref, cb_ref):
            cw = cw_ref[:, :cols]
            out = cb_ref[:, :cols]
            for kk in range(CONV_WIDTH):
                out = out + h_sc[pl.ds(halo - (CONV_WIDTH - 1) + kk, size), :cols] * cw[kk:kk + 1, :]
            return out

        start, prev = 0, 0
        for c, size in enumerate(chunks):
            p = c % 2
            rows = slice(start, start + size)
            if c == 0:
                lhs = head_sc[0:halo + size, :]
                hg_sc[p, 0:halo + size, :cols] = _dot(lhs, wg_ref[:, :cols])
                hv_sc[p, 0:halo + size, :cols] = _dot(lhs, wv_ref[:, :cols])
            else:
                lhs = head_sc[halo + start:halo + start + size, :]
                hg_sc[p, 0:halo, :cols] = hg_sc[1 - p, prev:prev + halo, :cols]
                hv_sc[p, 0:halo, :cols] = hv_sc[1 - p, prev:prev + halo, :cols]
                hg_sc[p, halo:halo + size, :cols] = _dot(lhs, wg_ref[:, :cols])
                hv_sc[p, halo:halo + size, :cols] = _dot(lhs, wv_ref[:, :cols])
            gate = conv(hg_sc.at[p], size, cwg_ref, cbg_ref)
            val = conv(hv_sc.at[p], size, cwv_ref, cbv_ref)
            o_ref[rows, :cols] = (gate * jax.nn.sigmoid(gate) * val).astype(o_ref.dtype)
            start, prev = start + size, size
        if cols < tn:
            o_ref[:, cols:] = jnp.zeros((tm, tn - cols), o_ref.dtype)

    if last_cols == tn:
        run(tn)
    else:
        last = pl.num_programs(1) - 1
        pl.when(j < last)(lambda: run(tn))
        pl.when(j == last)(lambda: run(last_cols))


def _ffn_up(xb, w_up, conv_w, conv_b, seq, d_ff, *, tm, tn, chunks, halo=16):
    m, kdim = xb.shape
    f_pad = w_up.shape[1] // 2
    nj = f_pad // tn
    assert seq % tm == 0 and sum(chunks) == tm and all(c % halo == 0 for c in chunks)
    chunk = max(chunks)
    hb = tm // halo
    last_cols = d_ff - (nj - 1) * tn
    assert 0 < last_cols <= tn and last_cols % HEAD_DIM == 0
    return pl.pallas_call(
        functools.partial(_ffn_up_kernel, tm=tm, chunks=tuple(chunks), halo=halo, tiles_per_seq=seq // tm,
                          last_cols=last_cols),
        grid=(m // tm, nj),
        in_specs=[
            pl.BlockSpec((tm, kdim), lambda i, j: (i, 0)),
            pl.BlockSpec((halo, kdim), lambda i, j: (jnp.maximum(i * hb - 1, 0), 0)),
            pl.BlockSpec((kdim, tn), lambda i, j: (0, j)),
            pl.BlockSpec((kdim, tn), lambda i, j: (0, nj + j)),
            pl.BlockSpec((CONV_WIDTH, tn), lambda i, j: (0, j)),
            pl.BlockSpec((CONV_WIDTH, tn), lambda i, j: (0, nj + j)),
            pl.BlockSpec((1, tn), lambda i, j: (0, j)),
            pl.BlockSpec((1, tn), lambda i, j: (0, nj + j)),
        ],
        out_specs=pl.BlockSpec((tm, tn), lambda i, j: (i, j)),
        out_shape=jax.ShapeDtypeStruct((m, f_pad), BF16),
        scratch_shapes=[pltpu.VMEM((halo + tm, kdim), BF16), pltpu.VMEM((2, halo + chunk, tn), F32),
                        pltpu.VMEM((2, halo + chunk, tn), F32)],
        compiler_params=_params(("parallel", "arbitrary")),
        name="ffn_up_conv_gate",
    )(xb, xb, w_up, w_up, conv_w, conv_w, conv_b, conv_b)


def _pool_kernel(x_ref, halo_ref, w_ref, ps_ref, g_ref, b_ref, of_ref, ob_ref, d_sc, *, tm, halo, tiles_per_seq, group_dim):
    i = pl.program_id(0)
    keep = jnp.where(i % tiles_per_seq == 0, 0.0, 1.0)
    t_in_seq = (i % tiles_per_seq) * tm + lax.broadcasted_iota(jnp.int32, (tm, group_dim), 0)
    for gi, win in enumerate(POOL_WINDOWS):
        cs = slice(gi * group_dim, (gi + 1) * group_dim)
        ext = jnp.concatenate([halo_ref[:, cs] * keep, x_ref[:, cs]], axis=0)
        acc, span = ext, 1
        while span < win:
            acc = acc + pltpu.roll(acc, span, axis=0)
            span *= 2
        count = jnp.minimum(t_in_seq + 1, win).astype(F32)
        d = acc[halo:, :] / count - ext[halo:, :]
        d_sc[:, cs] = _dot(d.astype(BF16), w_ref[gi])
    y = d_sc[...] * ps_ref[...]
    _ln_epilogue(y, x_ref, g_ref, b_ref, of_ref, ob_ref)


def _pool_mixer_ln(x, w_pool, pool_scale, g, b, seq, *, tm, halo=16):
    m, n = x.shape
    ngrp, group_dim = w_pool.shape[0], w_pool.shape[1]
    assert seq % tm == 0 and tm % halo == 0 and max(POOL_WINDOWS) <= halo
    hb = tm // halo
    row = pl.BlockSpec((tm, n), lambda i: (i, 0))
    vec = pl.BlockSpec((1, n), lambda i: (0, 0))
    return pl.pallas_call(
        functools.partial(_pool_kernel, tm=tm, halo=halo, tiles_per_seq=seq // tm, group_dim=group_dim),
        grid=(m // tm,),
        in_specs=[row, pl.BlockSpec((halo, n), lambda i: (jnp.maximum(i * hb - 1, 0), 0)),
                  pl.BlockSpec(w_pool.shape, lambda i: (0, 0, 0), pipeline_mode=pl.Buffered(1)), vec, vec, vec],
        out_specs=[row, row],
        out_shape=[jax.ShapeDtypeStruct((m, n), F32), jax.ShapeDtypeStruct((m, n), BF16)],
        scratch_shapes=[pltpu.VMEM((tm, n), F32)],
        compiler_params=_params(("parallel",)),
        name="pool_mixer_ln",
    )(x, x, w_pool, pool_scale, g, b)


def _rope_tables(pos):
    inv_freq = 1.0 / (ROPE_THETA ** (jnp.arange(0, HEAD_DIM, 2, dtype=F32) / HEAD_DIM))
    ang = pos.astype(F32)[:, None] * inv_freq[None, :]
    cos, sin = jnp.cos(ang), jnp.sin(ang)
    return jnp.concatenate([cos, cos], axis=-1), jnp.concatenate([-sin, sin], axis=-1)


def _overlap_t(seq):
    n_cmp = (seq - CMP_BLOCK) // CMP_STRIDE + 1
    n_blk = seq // SEL_BLOCK
    c_start = np.arange(n_cmp)[:, None] * CMP_STRIDE
    b_start = np.arange(n_blk)[None, :] * SEL_BLOCK
    ov = np.clip(np.minimum(c_start + CMP_BLOCK, b_start + SEL_BLOCK) - np.maximum(c_start, b_start), 0, None) / CMP_BLOCK
    out = np.zeros((HEAD_DIM, HEAD_DIM), np.float32)
    out[:n_blk, :n_cmp] = ov.T
    return jnp.asarray(out)


def _block_indicator(seq):
    out = np.zeros((seq, HEAD_DIM), np.float32)
    out[np.arange(seq), np.arange(seq) // SEL_BLOCK] = 1.0
    return jnp.asarray(out, BF16)


def _later_sum_matrix(n):
    return jnp.asarray(np.tril(np.ones((n, n), np.float32), -1), BF16)


def _pad_cols(w, n):
    return jnp.pad(w, ((0, 0), (0, n - w.shape[1])))


def _attention_layer(x, w_in, w_out, w_cmp_k, w_cmp_v, pe_k, pe_v, g, b, bsz, seq):
    n_cmp = (seq - CMP_BLOCK) // CMP_STRIDE + 1
    n_rows = seq // CMP_STRIDE
    w_nsa = w_in[:, :GATE_OFF].astype(BF16)
    w_sb = w_in[:, GATE_OFF + GATE_DIM:].astype(BF16)
    gate_cols = np.array([[GATE_OFF + c * NSA_HEADS + k * NSA_GROUP + gg for c in range(3) for gg in range(NSA_GROUP)]
                          for k in range(NSA_KV_HEADS)])
    w_gate = jnp.concatenate([_pad_cols(w_in[:, gate_cols[k]], HEAD_DIM) for k in range(NSA_KV_HEADS)], axis=1).astype(BF16)

    cos_t, sin_t = _rope_tables(jnp.arange(seq))
    gates, xb = _gate_proj(x, w_gate, tm=GATE_ROWS)
    h = _inproj(xb, w_nsa, cos_t, sin_t, seq, head_base=COL_Q_A, tm=PROJ_ROWS, tn=PROJ_COLS)
    h_sb = _inproj(xb, w_sb, cos_t, sin_t, seq, head_base=COL_Q_B, tm=PROJ_ROWS, tn=PROJ_COLS)

    def blocks16(col0):
        part = h[:, col0 * HEAD_DIM:(col0 + NSA_KV_HEADS) * HEAD_DIM]
        part = part.reshape(bsz, n_rows, CMP_STRIDE, NSA_KV_HEADS, HEAD_DIM).transpose(0, 3, 1, 2, 4)
        return part.reshape(bsz, NSA_KV_HEADS, n_rows, CMP_STRIDE * HEAD_DIM)

    cmp_end = jnp.arange(n_rows) * CMP_STRIDE + CMP_BLOCK - 1
    cos_c, sin_c = _rope_tables(cmp_end)
    flat_pe = lambda pe: jnp.broadcast_to(pe.reshape(1, CMP_BLOCK * HEAD_DIM), (8, CMP_BLOCK * HEAD_DIM)).astype(BF16)
    flat_w = lambda w: w.reshape(CMP_BLOCK * HEAD_DIM, HEAD_DIM).astype(BF16)
    kc, vct = _compress(blocks16(COL_KC), blocks16(COL_VC), flat_w(w_cmp_k), flat_w(w_cmp_v),
                        flat_pe(pe_k), flat_pe(pe_v), cos_c, sin_c, n_cmp)

    o_a = _nsa(h, kc, vct, gates, _overlap_t(seq), _block_indicator(seq), bsz, seq, tq=ATTN_TILE, tk=ATTN_TILE)
    o_b = _stick_breaking(h_sb, _later_sum_matrix(ATTN_TILE), bsz, seq, tq=ATTN_TILE, tk=ATTN_TILE,
                          heads=SB_HEADS_PER_STEP)
    return _matmul_ln([o_a, o_b], w_out.astype(BF16), x, g, b, tm=LN_ROWS, tk=LN_K, name="out_proj_ln")


def _ffn_layer(x, xb, w_up_all, conv_w, conv_b, w_down_all, layer, g, b, seq, *, tn=FFN_COLS, tk=LN_K):
    d_ff = w_down_all.shape[1]
    f_pad = -(-d_ff // tk) * tk
    assert f_pad % tn == 0
    split = lambda a: jnp.concatenate([_pad_cols(a[:, :d_ff], f_pad), _pad_cols(a[:, d_ff:], f_pad)], axis=1)
    w_up_p = _stage_ffn_weight(w_up_all, layer, d_ff, f_pad, axis=1)
    conv_w_p = split(conv_w)
    conv_b_p = split(conv_b.reshape(1, -1))
    w_down_p = _stage_ffn_weight(w_down_all, layer, d_ff, f_pad, axis=0)
    gated = _ffn_up(xb, w_up_p, conv_w_p, conv_b_p, seq, d_ff, tm=FFN_ROWS, tn=tn, chunks=FFN_ROW_CHUNKS)
    return _matmul_ln([gated], w_down_p, x, g, b, tm=LN_ROWS, tk=tk, name="ffn_down_ln", k_valid=d_ff)


def kernel(x, attn_w_in, attn_w_out, cmp_w_k, cmp_w_v, cmp_pe_k, cmp_pe_v, pool_w, pool_scale, ffn_w_up, ffn_conv_w,
           ffn_conv_b, ffn_w_down, ln_mix_g, ln_mix_b, ln_ffn_g, ln_ffn_b):
    bsz, seq, d = x.shape
    xf = x.reshape(bsz * seq, d)
    xb = None
    vec = lambda a, layer: a[layer].reshape(1, d)
    for layer in range(DEPTH):
        i = layer // 2
        if layer % 2 == 0:
            xf, xb = _attention_layer(xf, attn_w_in[i], attn_w_out[i], cmp_w_k[i], cmp_w_v[i], cmp_pe_k[i],
                                      cmp_pe_v[i], vec(ln_mix_g, layer), vec(ln_mix_b, layer), bsz, seq)
        else:
            xf, xb = _pool_mixer_ln(xf, pool_w[i].astype(BF16), pool_scale[i].reshape(1, d), vec(ln_mix_g, layer),
                                    vec(ln_mix_b, layer), seq, tm=POOL_ROWS)
        xf, xb = _ffn_layer(xf, xb, ffn_w_up, ffn_conv_w[layer], ffn_conv_b[layer], ffn_w_down, layer,
                            vec(ln_ffn_g, layer), vec(ln_ffn_b, layer), seq)
    return xf.reshape(bsz, seq, d)
```

```python
import functools

import jax
import jax.numpy as jnp
import numpy as np
from jax import lax
from jax.experimental import pallas as pl
from jax.experimental.pallas import tpu as pltpu

F32 = jnp.float32
BF16 = jnp.bfloat16

HEAD_DIM = 128
NSA_HEADS = 16
NSA_KV_HEADS = 4
NSA_GROUP = 4
SB_HEADS = 16
CMP_BLOCK = 32
CMP_STRIDE = 16
SEL_BLOCK = 64
SEL_TOP_N = 16
WINDOW = 512
ROPE_THETA = 10000.0
POOL_WINDOWS = (2, 4, 8, 16)
CONV_WIDTH = 3
LN_EPS = 1e-5
NEG_INF = -1e30
FORCE_SCORE = 1e9
DEPTH = 2
DEEPNORM_ALPHA = (2 * DEPTH) ** 0.25
SCALE = HEAD_DIM ** -0.5
LOG2E = 1.4426950408889634

COL_Q_A, COL_KC, COL_VC, COL_KS, COL_VS, COL_KW, COL_VW, COL_Q_B, COL_K_B, COL_V_B = 0, 16, 20, 24, 28, 32, 36, 40, 56, 72
Q_NSA_DIM = NSA_HEADS * HEAD_DIM
KV_NSA_DIM = NSA_KV_HEADS * HEAD_DIM
GATE_DIM = 3 * NSA_HEADS
GATE_OFF = Q_NSA_DIM + 6 * KV_NSA_DIM

VMEM_LIMIT_BYTES = 56 * 1024 * 1024


PROJ_ROWS, PROJ_COLS = 1024, 1024
GATE_ROWS = 512
ATTN_TILE = 256
SB_HEADS_PER_STEP = 16
NSA_KV_PER_STEP = 2
LN_ROWS, LN_K = 512, 1024
FFN_ROWS, FFN_COLS = 1024, 512
FFN_ROW_CHUNKS = (256, 256, 256, 128, 128)
POOL_ROWS = 256


def _params(sem, vmem=VMEM_LIMIT_BYTES):
    return pltpu.CompilerParams(dimension_semantics=sem, vmem_limit_bytes=vmem)


def _dot(a, b):
    return jnp.dot(a, b, preferred_element_type=F32)


def _dot_nt(a, b):
    return lax.dot_general(a, b, (((1,), (1,)), ((), ())), preferred_element_type=F32)


def _layer_norm_rows(y, g, b):
    mu = jnp.mean(y, axis=-1, keepdims=True)
    d = y - mu
    var = jnp.mean(d * d, axis=-1, keepdims=True)
    return d * lax.rsqrt(var + LN_EPS) * g + b


def _transpose_bf16(x):
    return x.astype(F32).T.astype(BF16)


def _inproj_kernel(a_ref, b_ref, cos_ref, sin_ref, o_ref, *, head_base, heads_per_tile, chunk):
    j = pl.program_id(1)

    def head_scales(head):
        is_rope = (head < COL_KC) | ((head >= COL_KS) & (head < COL_VS)) | ((head >= COL_KW) & (head < COL_VW))
        is_query = (head < COL_KC) | ((head >= COL_Q_B) & (head < COL_K_B))
        q_scale = jnp.where(is_query, SCALE * LOG2E, 1.0).astype(F32)
        return jnp.where(is_rope, q_scale, 0.0), jnp.where(is_rope, 0.0, q_scale)

    scales = [head_scales(head_base + j * heads_per_tile + hh) for hh in range(heads_per_tile)]
    for c in range(a_ref.shape[0] // chunk):
        rows = slice(c * chunk, (c + 1) * chunk)
        acc = _dot(a_ref[rows, :], b_ref[...])
        cos_c, sin_c = cos_ref[rows, :], sin_ref[rows, :]
        for hh, (rope_sel, pass_sel) in enumerate(scales):
            blk = acc[:, hh * HEAD_DIM:(hh + 1) * HEAD_DIM]
            rot = pltpu.roll(blk, HEAD_DIM // 2, axis=1)
            out = blk * (cos_c * rope_sel + pass_sel) + rot * (sin_c * rope_sel)
            o_ref[rows, hh * HEAD_DIM:(hh + 1) * HEAD_DIM] = out.astype(o_ref.dtype)


def _inproj(xb, w_heads, cos_t, sin_t, seq, *, head_base, tm, tn, chunk=256):
    m, kdim = xb.shape
    n = w_heads.shape[1]
    assert tn % HEAD_DIM == 0 and n % tn == 0 and seq % tm == 0 and tm % chunk == 0
    return pl.pallas_call(
        functools.partial(_inproj_kernel, head_base=head_base, heads_per_tile=tn // HEAD_DIM, chunk=chunk),
        grid=(m // tm, n // tn),
        in_specs=[
            pl.BlockSpec((tm, kdim), lambda i, j: (i, 0)),
            pl.BlockSpec((kdim, tn), lambda i, j: (0, j)),
            pl.BlockSpec((tm, HEAD_DIM), lambda i, j: (i % (seq // tm), 0)),
            pl.BlockSpec((tm, HEAD_DIM), lambda i, j: (i % (seq // tm), 0)),
        ],
        out_specs=pl.BlockSpec((tm, tn), lambda i, j: (i, j)),
        out_shape=jax.ShapeDtypeStruct((m, n), BF16),
        compiler_params=_params(("parallel", "arbitrary")),
        name="inproj_rope",
    )(xb, w_heads, cos_t, sin_t)


def _gate_kernel(x_ref, b_ref, o_ref, xb_ref):
    xb = x_ref[...].astype(BF16)
    xb_ref[...] = xb
    o_ref[...] = jax.nn.sigmoid(_dot(xb, b_ref[...]))


def _gate_proj(x, w_gate, *, tm):
    m, kdim = x.shape
    n = w_gate.shape[1]
    return pl.pallas_call(
        _gate_kernel,
        grid=(m // tm,),
        in_specs=[pl.BlockSpec((tm, kdim), lambda i: (i, 0)), pl.BlockSpec((kdim, n), lambda i: (0, 0))],
        out_specs=[pl.BlockSpec((tm, n), lambda i: (i, 0)), pl.BlockSpec((tm, kdim), lambda i: (i, 0))],
        out_shape=[jax.ShapeDtypeStruct((m, n), F32), jax.ShapeDtypeStruct((m, kdim), BF16)],
        compiler_params=_params(("parallel",)),
        name="gate_proj",
    )(x, w_gate)


def _compress_kernel(k2_ref, v2_ref, wk_ref, wv_ref, pek_ref, pev_ref, cos_ref, sin_ref, kc_ref, vct_ref, *, n_cmp):
    half = wk_ref.shape[0] // 2
    n_rows = k2_ref.shape[0]
    row = lax.broadcasted_iota(jnp.int32, (n_rows, HEAD_DIM), 0)

    def compress(x2_ref, w_ref, pe_ref):
        x2 = x2_ref[...]
        lo = _dot(x2, w_ref[:half, :])
        hi = _dot(x2, w_ref[half:, :])
        pe_term = _dot(pe_ref[...], w_ref[...])[0:1, :]
        return lo + pltpu.roll(hi, n_rows - 1, axis=0) + pe_term

    kc = compress(k2_ref, wk_ref, pek_ref)
    vc = compress(v2_ref, wv_ref, pev_ref)
    kc = kc * cos_ref[...] + pltpu.roll(kc, HEAD_DIM // 2, axis=1) * sin_ref[...]
    kc_ref[...] = jnp.where(row < n_cmp, kc, 0.0).astype(kc_ref.dtype)
    vct_ref[...] = jnp.where(row < n_cmp, vc, 0.0).T.astype(vct_ref.dtype)


def _compress(k2, v2, wk, wv, pek, pev, cos_c, sin_c, n_cmp):
    bsz, nkv, n_rows, wide = k2.shape
    assert n_rows == HEAD_DIM
    blk4 = pl.BlockSpec((None, None, n_rows, wide), lambda b, k: (b, k, 0, 0))
    full = lambda a: pl.BlockSpec(a.shape, lambda b, k: (0,) * a.ndim)
    out_blk = pl.BlockSpec((None, None, n_rows, HEAD_DIM), lambda b, k: (b, k, 0, 0))
    out_sds = jax.ShapeDtypeStruct((bsz, nkv, n_rows, HEAD_DIM), BF16)
    return pl.pallas_call(
        functools.partial(_compress_kernel, n_cmp=n_cmp),
        grid=(bsz, nkv),
        in_specs=[blk4, blk4, full(wk), full(wv), full(pek), full(pev), full(cos_c), full(sin_c)],
        out_specs=[out_blk, out_blk],
        out_shape=[out_sds, out_sds],
        compiler_params=_params(("parallel", "parallel")),
        name="nsa_compress",
    )(k2, v2, wk, wv, pek, pev, cos_c, sin_c)


def _nsa_kernel(q_ref, kc_ref, vct_ref, ks_ref, vs_ref, kw_ref, vw_ref, g_ref, ovl_ref, eind_ref, o_ref,
                vst_sc, vwt_sc, m_sc, l_sc, acc_sc, *, tq, tk, n_cmp, n_blk, kv_per_step):
    qi = pl.program_id(2)
    t0 = qi * tq
    cols = NSA_GROUP * tq
    qw = NSA_GROUP * HEAD_DIM
    kv_heads = range(kv_per_step)
    head_cols = lambda kv: slice(kv * HEAD_DIM, (kv + 1) * HEAD_DIM)

    @pl.when(qi == 0)
    def _():
        for kv in kv_heads:
            for kt in range(vst_sc.shape[1]):
                vst_sc[kv, kt] = _transpose_bf16(vs_ref[kt * tk:(kt + 1) * tk, head_cols(kv)])
                vwt_sc[kv, kt] = _transpose_bf16(vw_ref[kt * tk:(kt + 1) * tk, head_cols(kv)])

    key_minus_query = (lax.broadcasted_iota(jnp.int32, (tk, cols), 0)
                       - (lax.broadcasted_iota(jnp.int32, (tk, cols), 1) & (tq - 1)))
    n_rows = kc_ref.shape[1]
    row = lax.broadcasted_iota(jnp.int32, (n_rows, cols), 0)
    t_abs = t0 + (lax.broadcasted_iota(jnp.int32, (n_rows, cols), 1) & (tq - 1))
    jrow = lax.broadcasted_iota(jnp.int32, (n_blk, tq), 0)
    t_lane = t0 + lax.broadcasted_iota(jnp.int32, (n_blk, tq), 1)
    cur = t_lane >> 6
    forced = (jrow == 0) | (jrow == cur) | (jrow == cur - 1)

    def compressed_and_ranking(kv, q4):
        s = _dot_nt(kc_ref[kv], q4)
        valid = (row * CMP_STRIDE + (CMP_BLOCK - 1) <= t_abs) & (row < n_cmp)
        s = jnp.where(valid, s, NEG_INF)
        e = jnp.exp2(s - jnp.max(s, axis=0, keepdims=True))
        p = e * (1.0 / jnp.sum(e, axis=0, keepdims=True))
        p = jnp.where(t_abs >= CMP_BLOCK - 1, p, 0.0)
        o_cmp = _dot(vct_ref[kv], p.astype(BF16))

        psum = p[:, 0:tq] + p[:, tq:2 * tq] + p[:, 2 * tq:3 * tq] + p[:, 3 * tq:4 * tq]
        imp = jnp.dot(ovl_ref[...], psum, precision=lax.Precision.HIGHEST, preferred_element_type=F32)[0:n_blk, :]
        imp = jnp.where(forced, FORCE_SCORE, jnp.where(jrow * SEL_BLOCK <= t_lane, imp, NEG_INF))
        rank = jnp.zeros((n_blk, tq), F32)
        for jp in range(n_blk):
            other = imp[jp:jp + 1, :]
            ahead = (other > imp) | ((other == imp) & (jrow > jp))
            rank = rank + jnp.where(ahead, 1.0, 0.0)
        bias_t = jnp.where(rank < SEL_TOP_N, 0.0, NEG_INF)
        bias_t = jnp.concatenate([bias_t, jnp.zeros((HEAD_DIM - n_blk, tq), F32)], axis=0)
        bias = bias_t.T.astype(BF16)
        q_aug = jnp.concatenate([q4, jnp.concatenate([bias] * NSA_GROUP, axis=0)], axis=1)
        return o_cmp, q_aug

    def window(kv, q4):
        def win_tile(back):
            kt = qi - back
            kt_c = jnp.maximum(kt, 0)
            sc = _dot_nt(kw_ref[pl.ds(pl.multiple_of(kt_c * tk, tk), tk), head_cols(kv)], q4)
            hi, lo = back * tk, back * tk - WINDOW
            if lo >= -(tq - 1):
                lo_dyn = jnp.where(kt >= 0, lo, tk) if back else lo
                sc = jnp.where(key_minus_query > lo_dyn, sc, NEG_INF)
            elif back:
                sc = jnp.where(kt >= 0, sc, NEG_INF)
            if hi < tk - 1:
                sc = jnp.where(key_minus_query <= hi, sc, NEG_INF)
            return sc, vwt_sc[kv, kt_c]

        w_tiles = [win_tile(back) for back in range(WINDOW // tk, -1, -1)]
        m_w = functools.reduce(jnp.maximum, [jnp.max(sc, axis=0, keepdims=True) for sc, _ in w_tiles])
        l_w = jnp.zeros_like(m_w)
        o_win = jnp.zeros((HEAD_DIM, cols), F32)
        for sc, vt in w_tiles:
            pr = jnp.exp2(sc - m_w)
            l_w = l_w + jnp.sum(pr, axis=0, keepdims=True)
            o_win = o_win + _dot(vt, pr.astype(BF16))
        return o_win * (1.0 / l_w)

    o_cmp, q_aug, o_win = [], [], []
    for kv in kv_heads:
        q = q_ref[:, kv * qw:(kv + 1) * qw]
        q4 = jnp.concatenate([q[:, g * HEAD_DIM:(g + 1) * HEAD_DIM] for g in range(NSA_GROUP)], axis=0)
        oc, qa = compressed_and_ranking(kv, q4)
        o_cmp.append(oc)
        q_aug.append(qa)
        o_win.append(window(kv, q4))

    def slc_scores(kv, kt, n_tiles=1):
        s0 = pl.multiple_of(kt * tk, tk)
        k_aug = jnp.concatenate([ks_ref[pl.ds(s0, n_tiles * tk), head_cols(kv)], eind_ref[pl.ds(s0, n_tiles * tk), :]],
                                axis=1)
        return _dot_nt(k_aug, q_aug[kv])

    def online_step(kv, sc, vt):
        m_old = m_sc[kv]
        m_new = jnp.maximum(m_old, jnp.max(sc, axis=0, keepdims=True))
        pr = jnp.exp2(sc - m_new)
        corr = jnp.exp2(m_old - m_new)
        l_sc[kv] = corr * l_sc[kv] + jnp.sum(pr, axis=0, keepdims=True)
        acc_sc[kv] = corr * acc_sc[kv] + _dot(vt, pr.astype(BF16))
        m_sc[kv] = m_new

    m_sc[...] = jnp.full(m_sc.shape, NEG_INF, F32)
    l_sc[...] = jnp.zeros(l_sc.shape, F32)
    acc_sc[...] = jnp.zeros(acc_sc.shape, F32)

    def slc_pair(pt, carry):
        kt = 2 * pt
        for kv in kv_heads:
            online_step(kv, slc_scores(kv, kt, 2), jnp.concatenate([vst_sc[kv, kt], vst_sc[kv, kt + 1]], axis=1))
        return carry

    lax.fori_loop(0, qi // 2, slc_pair, 0)

    @pl.when(qi % 2 == 1)
    def _():
        for kv in kv_heads:
            online_step(kv, slc_scores(kv, qi - 1), vst_sc[kv, qi - 1])

    for kv in kv_heads:
        online_step(kv, jnp.where(key_minus_query <= 0, slc_scores(kv, qi), NEG_INF), vst_sc[kv, qi])

    for kv in kv_heads:
        o_slc = acc_sc[kv] * (1.0 / l_sc[kv])
        gates_t = g_ref[:, head_cols(kv)].T
        for g in range(NSA_GROUP):
            sl = slice(g * tq, (g + 1) * tq)
            o = (gates_t[g:g + 1, :] * o_cmp[kv][:, sl]
                 + gates_t[NSA_GROUP + g:NSA_GROUP + g + 1, :] * o_slc[:, sl]
                 + gates_t[2 * NSA_GROUP + g:2 * NSA_GROUP + g + 1, :] * o_win[kv][:, sl])
            o_ref[:, kv * qw + g * HEAD_DIM:kv * qw + (g + 1) * HEAD_DIM] = o.T.astype(o_ref.dtype)


def _nsa(h, kc, vct, gates, ovl, eind, bsz, seq, *, tq, tk, kv_per_step):
    assert tq == tk and seq % tq == 0 and WINDOW % tk == 0
    assert NSA_KV_HEADS % kv_per_step == 0 and all(c % kv_per_step == 0 for c in (COL_KS, COL_VS, COL_KW, COL_VW))
    m = h.shape[0]
    nq = seq // tq
    n_cmp = (seq - CMP_BLOCK) // CMP_STRIDE + 1
    n_blk = seq // SEL_BLOCK
    qw = kv_per_step * NSA_GROUP * HEAD_DIM
    kvw = kv_per_step * HEAD_DIM
    cols = NSA_GROUP * tq
    kv_spec = lambda col0: pl.BlockSpec((seq, kvw), lambda b, k, i: (b, col0 // kv_per_step + k))
    cmp_spec = pl.BlockSpec((None, kv_per_step, kc.shape[2], HEAD_DIM), lambda b, k, i: (b, k, 0, 0))
    tiles = seq // tk
    return pl.pallas_call(
        functools.partial(_nsa_kernel, tq=tq, tk=tk, n_cmp=n_cmp, n_blk=n_blk, kv_per_step=kv_per_step),
        grid=(bsz, NSA_KV_HEADS // kv_per_step, nq),
        in_specs=[
            pl.BlockSpec((tq, qw), lambda b, k, i: (b * nq + i, k)),
            cmp_spec, cmp_spec,
            kv_spec(COL_KS), kv_spec(COL_VS), kv_spec(COL_KW), kv_spec(COL_VW),
            pl.BlockSpec((tq, kvw), lambda b, k, i: (b * nq + i, k)),
            pl.BlockSpec(ovl.shape, lambda b, k, i: (0, 0)),
            pl.BlockSpec(eind.shape, lambda b, k, i: (0, 0)),
        ],
        out_specs=pl.BlockSpec((tq, qw), lambda b, k, i: (b * nq + i, k)),
        out_shape=jax.ShapeDtypeStruct((m, Q_NSA_DIM), BF16),
        scratch_shapes=[pltpu.VMEM((kv_per_step, tiles, HEAD_DIM, tk), BF16),
                        pltpu.VMEM((kv_per_step, tiles, HEAD_DIM, tk), BF16),
                        pltpu.VMEM((kv_per_step, 1, cols), F32), pltpu.VMEM((kv_per_step, 1, cols), F32),
                        pltpu.VMEM((kv_per_step, HEAD_DIM, cols), F32)],
        compiler_params=_params(("arbitrary", "arbitrary", "arbitrary")),
        name="nsa_attention",
    )(h, kc, vct, h, h, h, h, gates, ovl, eind)


def _sb_kernel(q_ref, k_ref, v_ref, u_ref, o_ref, carry_sc, acc_sc, *, tq, tk, heads):
    qi = pl.program_id(2)
    query_minus_key = lax.broadcasted_iota(jnp.int32, (tq, tk), 0) - lax.broadcasted_iota(jnp.int32, (tq, tk), 1)
    carry_sc[...] = jnp.zeros(carry_sc.shape, F32)
    acc_sc[...] = jnp.zeros(acc_sc.shape, F32)
    u = u_ref[...]

    def tile(kt, diagonal):
        s0 = pl.multiple_of(kt * tk, tk)
        for hh in range(heads):
            cs = slice(hh * HEAD_DIM, (hh + 1) * HEAD_DIM)
            z = _dot_nt(q_ref[:, cs], k_ref[pl.ds(s0, tk), cs])
            sp = jnp.maximum(z, 0.0) + jnp.log2(1.0 + jnp.exp2(-jnp.abs(z)))
            if diagonal:
                mask = query_minus_key > 0
                sp = jnp.where(mask, sp, 0.0)
            sp_b = sp.astype(BF16)
            sums = _dot(sp_b, u)
            carry = carry_sc[hh]
            later = sums + jnp.concatenate([carry] * (tk // HEAD_DIM), axis=1)
            a = jnp.exp2(z - sp - later)
            if diagonal:
                a = jnp.where(mask, a, 0.0)
            acc_sc[hh] += _dot(a.astype(BF16), v_ref[pl.ds(s0, tk), cs])
            row_total = sums[:, 0:1] + sp_b[:, 0:1].astype(F32)
            carry_sc[hh] = carry + jnp.broadcast_to(row_total, carry.shape)

    tile(qi, True)

    def body(it, c):
        tile(qi - 1 - it, False)
        return c

    lax.fori_loop(0, qi, body, 0)
    for hh in range(heads):
        o_ref[:, hh * HEAD_DIM:(hh + 1) * HEAD_DIM] = acc_sc[hh].astype(o_ref.dtype)


def _stick_breaking(h, u_tri, bsz, seq, *, tq, tk, heads):
    assert seq % tq == 0 and tq == tk and u_tri.shape == (tk, tk)
    assert SB_HEADS % heads == 0 and h.shape[1] == 3 * SB_HEADS * HEAD_DIM
    groups = SB_HEADS // heads
    m = h.shape[0]
    nq = seq // tq
    wide = heads * HEAD_DIM
    return pl.pallas_call(
        functools.partial(_sb_kernel, tq=tq, tk=tk, heads=heads),
        grid=(bsz, groups, nq),
        in_specs=[
            pl.BlockSpec((tq, wide), lambda b, hg, i: (b * nq + i, hg)),
            pl.BlockSpec((seq, wide), lambda b, hg, i: (b, groups + hg)),
            pl.BlockSpec((seq, wide), lambda b, hg, i: (b, 2 * groups + hg)),
            pl.BlockSpec(u_tri.shape, lambda b, hg, i: (0, 0)),
        ],
        out_specs=pl.BlockSpec((tq, wide), lambda b, hg, i: (b * nq + i, hg)),
        out_shape=jax.ShapeDtypeStruct((m, SB_HEADS * HEAD_DIM), BF16),
        scratch_shapes=[pltpu.VMEM((heads, tq, HEAD_DIM), F32), pltpu.VMEM((heads, tq, HEAD_DIM), F32)],
        compiler_params=_params(("parallel", "parallel", "arbitrary")),
        name="stick_breaking",
    )(h, h, h, u_tri)


def _ln_epilogue(y, x_ref, g_ref, b_ref, of_ref, ob_ref, rows=slice(None)):
    out = _layer_norm_rows(DEEPNORM_ALPHA * x_ref[rows, :] + y, g_ref[...], b_ref[...])
    of_ref[rows, :] = out
    ob_ref[rows, :] = out.astype(ob_ref.dtype)


def _mm_ln_kernel(*refs, k_splits, chunk, last_k):
    n_a = len(k_splits) - 1
    a_refs = refs[:n_a]
    w_ref, x_hbm, g_ref, b_ref, of_hbm, ob_hbm, acc_sc, xo_sc, ob_sc, sems = refs[n_a:]
    i, k = pl.program_id(0), pl.program_id(1)
    n_tiles, nk = pl.num_programs(0), k_splits[-1]
    tm = acc_sc.shape[0]

    def tile_rows(tile):
        return pl.ds(pl.multiple_of(tile * tm, tm), tm)

    def residual_copy():
        return pltpu.make_async_copy(x_hbm.at[tile_rows(i)], xo_sc, sems.at[0])

    def output_copies(tile):
        return (pltpu.make_async_copy(xo_sc, of_hbm.at[tile_rows(tile)], sems.at[1]),
                pltpu.make_async_copy(ob_sc, ob_hbm.at[tile_rows(tile)], sems.at[2]))

    @pl.when(k == 1)
    def _():
        @pl.when(i > 0)
        def _():
            for cp in output_copies(i - 1):
                cp.wait()

        residual_copy().start()

    for s, a_ref in enumerate(a_refs):
        lo, hi = k_splits[s], k_splits[s + 1]
        first = lo == 0
        last = hi == nk

        if first:
            @pl.when(k == 0)
            def _(a_ref=a_ref):
                acc_sc[...] = _dot(a_ref[...], w_ref[...])

        lo_acc, hi_acc = lo + (1 if first else 0), hi - (1 if last else 0)
        if hi_acc > lo_acc:
            @pl.when((k >= lo_acc) & (k < hi_acc))
            def _(a_ref=a_ref):
                acc_sc[...] += _dot(a_ref[...], w_ref[...])

        if last:
            @pl.when(k == nk - 1)
            def _(a_ref=a_ref):
                residual_copy().wait()
                for c in range(tm // chunk):
                    rows = slice(c * chunk, (c + 1) * chunk)
                    y = acc_sc[rows, :] + _dot(a_ref[rows, :last_k], w_ref[:last_k, :])
                    _ln_epilogue(y, xo_sc, g_ref, b_ref, xo_sc, ob_sc, rows)

                for cp in output_copies(i):
                    cp.start()

                @pl.when(i == n_tiles - 1)
                def _():
                    for cp in output_copies(i):
                        cp.wait()


def _matmul_ln(a_list, w, x, g, b, *, tm, tk, name, chunk=128, k_valid=None):
    m, n = x.shape
    kdim = w.shape[0]
    nk = kdim // tk
    k_splits = [0]
    for a in a_list:
        assert a.shape[1] % tk == 0
        k_splits.append(k_splits[-1] + a.shape[1] // tk)
    assert k_splits[-1] == nk and nk >= 3 and tm % chunk == 0 and m % tm == 0
    last_k = tk if k_valid is None else k_valid - (nk - 1) * tk
    assert 0 < last_k <= tk and last_k % HEAD_DIM == 0
    vec = pl.BlockSpec((1, n), lambda i, k: (0, 0))
    hbm = pl.BlockSpec(memory_space=pl.ANY)

    def a_spec(lo, hi):
        return pl.BlockSpec((tm, tk), lambda i, k: (i, jnp.clip(k - lo, 0, hi - lo - 1)))

    return pl.pallas_call(
        functools.partial(_mm_ln_kernel, k_splits=tuple(k_splits), chunk=chunk, last_k=last_k),
        grid=(m // tm, nk),
        in_specs=[a_spec(k_splits[s], k_splits[s + 1]) for s in range(len(a_list))]
        + [pl.BlockSpec((tk, n), lambda i, k: (k, 0)), hbm, vec, vec],
        out_specs=[hbm, hbm],
        out_shape=[jax.ShapeDtypeStruct((m, n), F32), jax.ShapeDtypeStruct((m, n), BF16)],
        scratch_shapes=[pltpu.VMEM((tm, n), F32), pltpu.VMEM((tm, n), F32), pltpu.VMEM((tm, n), BF16),
                        pltpu.SemaphoreType.DMA((3,))],
        compiler_params=_params(("arbitrary", "arbitrary")),
        name=name,
    )(*a_list, w, x, g, b)


def _stage_tiles_kernel(x_ref, o_ref, *, tiles_in, tiles_out):
    valid = pl.program_id(0) % tiles_out < tiles_in
    o_ref[...] = jnp.where(valid, x_ref[...], 0.0).astype(o_ref.dtype)


def _stage_ffn_weight(w_all, layer, d_ff, f_pad, axis, *, tile=256):
    assert d_ff % tile == 0 and f_pad % tile == 0
    tiles_in, tiles_out = d_ff // tile, f_pad // tile
    shape = list(w_all.shape[1:])
    sections = shape[axis] // d_ff
    src = lambda t: (t // tiles_out) * tiles_in + jnp.minimum(t % tiles_out, tiles_in - 1)
    if axis == 0:
        blk, out_shape = (tile, shape[1]), (sections * f_pad, shape[1])
        in_spec = pl.BlockSpec((None,) + blk, lambda t: (layer, src(t), 0))
        out_spec = pl.BlockSpec(blk, lambda t: (t, 0))
    else:
        blk, out_shape = (shape[0], tile), (shape[0], sections * f_pad)
        in_spec = pl.BlockSpec((None,) + blk, lambda t: (layer, 0, src(t)))
        out_spec = pl.BlockSpec(blk, lambda t: (0, t))
    return pl.pallas_call(
        functools.partial(_stage_tiles_kernel, tiles_in=tiles_in, tiles_out=tiles_out),
        grid=(sections * tiles_out,),
        in_specs=[in_spec],
        out_specs=out_spec,
        out_shape=jax.ShapeDtypeStruct(out_shape, BF16),
        compiler_params=_params(("parallel",)),
        name="stage_ffn_weight",
    )(w_all)


def _ffn_up_kernel(a_ref, halo_ref, wg_ref, wv_ref, cwg_ref, cwv_ref, cbg_ref, cbv_ref, o_ref, head_sc, hg_sc, hv_sc, *,
                   tm, chunks, halo, tiles_per_seq, last_cols):
    i, j = pl.program_id(0), pl.program_id(1)
    tn = o_ref.shape[1]

    @pl.when(j == 0)
    def _():
        seq_start = i % tiles_per_seq == 0
        head_sc[0:halo, :] = jnp.where(seq_start, jnp.zeros_like(halo_ref[...]), halo_ref[...])
        head_sc[halo:, :] = a_ref[...]

    def run(cols):
        def conv(h_sc, size, cw_ref, cb_ref):
            cw = cw_ref[:, :cols]
            out = cb_ref[:, :cols]
            for kk in range(CONV_WIDTH):
                out = out + h_sc[pl.ds(halo - (CONV_WIDTH - 1) + kk, size), :cols] * cw[kk:kk + 1, :]
            return out

        start, prev = 0, 0
        for c, size in enumerate(chunks):
            p = c % 2
            rows = slice(start, start + size)
            if c == 0:
                lhs = head_sc[0:halo + size, :]
                hg_sc[p, 0:halo + size, :cols] = _dot(lhs, wg_ref[:, :cols])
                hv_sc[p, 0:halo + size, :cols] = _dot(lhs, wv_ref[:, :cols])
            else:
                lhs = head_sc[halo + start:halo + start + size, :]
                hg_sc[p, 0:halo, :cols] = hg_sc[1 - p, prev:prev + halo, :cols]
                hv_sc[p, 0:halo, :cols] = hv_sc[1 - p, prev:prev + halo, :cols]
                hg_sc[p, halo:halo + size, :cols] = _dot(lhs, wg_ref[:, :cols])
                hv_sc[p, halo:halo + size, :cols] = _dot(lhs, wv_ref[:, :cols])
            gate = conv(hg_sc.at[p], size, cwg_ref, cbg_ref)
            val = conv(hv_sc.at[p], size, cwv_ref, cbv_ref)
            o_ref[rows, :cols] = (gate * jax.nn.sigmoid(gate) * val).astype(o_ref.dtype)
            start, prev = start + size, size
        if cols < tn:
            o_ref[:, cols:] = jnp.zeros((tm, tn - cols), o_ref.dtype)

    if last_cols == tn:
        run(tn)
    else:
        last = pl.num_programs(1) - 1
        pl.when(j < last)(lambda: run(tn))
        pl.when(j == last)(lambda: run(last_cols))


def _ffn_up(xb, w_up, conv_w, conv_b, seq, d_ff, *, tm, tn, chunks, halo=16):
    m, kdim = xb.shape
    f_pad = w_up.shape[1] // 2
    nj = f_pad // tn
    assert seq % tm == 0 and sum(chunks) == tm and all(c % halo == 0 for c in chunks)
    chunk = max(chunks)
    hb = tm // halo
    last_cols = d_ff - (nj - 1) * tn
    assert 0 < last_cols <= tn and last_cols % HEAD_DIM == 0
    return pl.pallas_call(
        functools.partial(_ffn_up_kernel, tm=tm, chunks=tuple(chunks), halo=halo, tiles_per_seq=seq // tm,
                          last_cols=last_cols),
        grid=(m // tm, nj),
        in_specs=[
            pl.BlockSpec((tm, kdim), lambda i, j: (i, 0)),
            pl.BlockSpec((halo, kdim), lambda i, j: (jnp.maximum(i * hb - 1, 0), 0)),
            pl.BlockSpec((kdim, tn), lambda i, j: (0, j)),
            pl.BlockSpec((kdim, tn), lambda i, j: (0, nj + j)),
            pl.BlockSpec((CONV_WIDTH, tn), lambda i, j: (0, j)),
            pl.BlockSpec((CONV_WIDTH, tn), lambda i, j: (0, nj + j)),
            pl.BlockSpec((1, tn), lambda i, j: (0, j)),
            pl.BlockSpec((1, tn), lambda i, j: (0, nj + j)),
        ],
        out_specs=pl.BlockSpec((tm, tn), lambda i, j: (i, j)),
        out_shape=jax.ShapeDtypeStruct((m, f_pad), BF16),
        scratch_shapes=[pltpu.VMEM((halo + tm, kdim), BF16), pltpu.VMEM((2, halo + chunk, tn), F32),
                        pltpu.VMEM((2, halo + chunk, tn), F32)],
        compiler_params=_params(("parallel", "arbitrary")),
        name="ffn_up_conv_gate",
    )(xb, xb, w_up, w_up, conv_w, conv_w, conv_b, conv_b)


def _pool_kernel(x_ref, halo_ref, w_ref, ps_ref, g_ref, b_ref, of_ref, ob_ref, d_sc, *, tm, halo, tiles_per_seq, group_dim):
    i = pl.program_id(0)
    keep = jnp.where(i % tiles_per_seq == 0, 0.0, 1.0)
    t_in_seq = (i % tiles_per_seq) * tm + lax.broadcasted_iota(jnp.int32, (tm, group_dim), 0)
    for gi, win in enumerate(POOL_WINDOWS):
        cs = slice(gi * group_dim, (gi + 1) * group_dim)
        ext = jnp.concatenate([halo_ref[:, cs] * keep, x_ref[:, cs]], axis=0)
        acc, span = ext, 1
        while span < win:
            acc = acc + pltpu.roll(acc, span, axis=0)
            span *= 2
        count = jnp.minimum(t_in_seq + 1, win).astype(F32)
        d = acc[halo:, :] / count - ext[halo:, :]
        d_sc[:, cs] = _dot(d.astype(BF16), w_ref[gi])
    y = d_sc[...] * ps_ref[...]
    _ln_epilogue(y, x_ref, g_ref, b_ref, of_ref, ob_ref)


def _pool_mixer_ln(x, w_pool, pool_scale, g, b, seq, *, tm, halo=16):
    m, n = x.shape
    ngrp, group_dim = w_pool.shape[0], w_pool.shape[1]
    assert seq % tm == 0 and tm % halo == 0 and max(POOL_WINDOWS) <= halo
    hb = tm // halo
    row = pl.BlockSpec((tm, n), lambda i: (i, 0))
    vec = pl.BlockSpec((1, n), lambda i: (0, 0))
    return pl.pallas_call(
        functools.partial(_pool_kernel, tm=tm, halo=halo, tiles_per_seq=seq // tm, group_dim=group_dim),
        grid=(m // tm,),
        in_specs=[row, pl.BlockSpec((halo, n), lambda i: (jnp.maximum(i * hb - 1, 0), 0)),
                  pl.BlockSpec(w_pool.shape, lambda i: (0, 0, 0), pipeline_mode=pl.Buffered(1)), vec, vec, vec],
        out_specs=[row, row],
        out_shape=[jax.ShapeDtypeStruct((m, n), F32), jax.ShapeDtypeStruct((m, n), BF16)],
        scratch_shapes=[pltpu.VMEM((tm, n), F32)],
        compiler_params=_params(("parallel",)),
        name="pool_mixer_ln",
    )(x, x, w_pool, pool_scale, g, b)


def _rope_tables(pos):
    inv_freq = 1.0 / (ROPE_THETA ** (jnp.arange(0, HEAD_DIM, 2, dtype=F32) / HEAD_DIM))
    ang = pos.astype(F32)[:, None] * inv_freq[None, :]
    cos, sin = jnp.cos(ang), jnp.sin(ang)
    return jnp.concatenate([cos, cos], axis=-1), jnp.concatenate([-sin, sin], axis=-1)


def _overlap_t(seq):
    n_cmp = (seq - CMP_BLOCK) // CMP_STRIDE + 1
    n_blk = seq // SEL_BLOCK
    c_start = np.arange(n_cmp)[:, None] * CMP_STRIDE
    b_start = np.arange(n_blk)[None, :] * SEL_BLOCK
    ov = np.clip(np.minimum(c_start + CMP_BLOCK, b_start + SEL_BLOCK) - np.maximum(c_start, b_start), 0, None) / CMP_BLOCK
    out = np.zeros((HEAD_DIM, HEAD_DIM), np.float32)
    out[:n_blk, :n_cmp] = ov.T
    return jnp.asarray(out)


def _block_indicator(seq):
    out = np.zeros((seq, HEAD_DIM), np.float32)
    out[np.arange(seq), np.arange(seq) // SEL_BLOCK] = 1.0
    return jnp.asarray(out, BF16)


def _later_sum_matrix(n):
    return jnp.asarray(np.tril(np.ones((n, n), np.float32), -1), BF16)


def _pad_cols(w, n):
    return jnp.pad(w, ((0, 0), (0, n - w.shape[1])))


def _attention_layer(x, w_in, w_out, w_cmp_k, w_cmp_v, pe_k, pe_v, g, b, bsz, seq):
    n_cmp = (seq - CMP_BLOCK) // CMP_STRIDE + 1
    n_rows = seq // CMP_STRIDE
    w_nsa = w_in[:, :GATE_OFF].astype(BF16)
    w_sb = w_in[:, GATE_OFF + GATE_DIM:].astype(BF16)
    gate_cols = np.array([[GATE_OFF + c * NSA_HEADS + k * NSA_GROUP + gg for c in range(3) for gg in range(NSA_GROUP)]
                          for k in range(NSA_KV_HEADS)])
    w_gate = jnp.concatenate([_pad_cols(w_in[:, gate_cols[k]], HEAD_DIM) for k in range(NSA_KV_HEADS)], axis=1).astype(BF16)

    cos_t, sin_t = _rope_tables(jnp.arange(seq))
    gates, xb = _gate_proj(x, w_gate, tm=GATE_ROWS)
    h = _inproj(xb, w_nsa, cos_t, sin_t, seq, head_base=COL_Q_A, tm=PROJ_ROWS, tn=PROJ_COLS)
    h_sb = _inproj(xb, w_sb, cos_t, sin_t, seq, head_base=COL_Q_B, tm=PROJ_ROWS, tn=PROJ_COLS)

    def blocks16(col0):
        part = h[:, col0 * HEAD_DIM:(col0 + NSA_KV_HEADS) * HEAD_DIM]
        part = part.reshape(bsz, n_rows, CMP_STRIDE, NSA_KV_HEADS, HEAD_DIM).transpose(0, 3, 1, 2, 4)
        return part.reshape(bsz, NSA_KV_HEADS, n_rows, CMP_STRIDE * HEAD_DIM)

    cmp_end = jnp.arange(n_rows) * CMP_STRIDE + CMP_BLOCK - 1
    cos_c, sin_c = _rope_tables(cmp_end)
    flat_pe = lambda pe: jnp.broadcast_to(pe.reshape(1, CMP_BLOCK * HEAD_DIM), (8, CMP_BLOCK * HEAD_DIM)).astype(BF16)
    flat_w = lambda w: w.reshape(CMP_BLOCK * HEAD_DIM, HEAD_DIM).astype(BF16)
    kc, vct = _compress(blocks16(COL_KC), blocks16(COL_VC), flat_w(w_cmp_k), flat_w(w_cmp_v),
                        flat_pe(pe_k), flat_pe(pe_v), cos_c, sin_c, n_cmp)

    o_a = _nsa(h, kc, vct, gates, _overlap_t(seq), _block_indicator(seq), bsz, seq, tq=ATTN_TILE, tk=ATTN_TILE,
               kv_per_step=NSA_KV_PER_STEP)
    o_b = _stick_breaking(h_sb, _later_sum_matrix(ATTN_TILE), bsz, seq, tq=ATTN_TILE, tk=ATTN_TILE,
                          heads=SB_HEADS_PER_STEP)
    return _matmul_ln([o_a, o_b], w_out.astype(BF16), x, g, b, tm=LN_ROWS, tk=LN_K, name="out_proj_ln")


def _ffn_layer(x, xb, w_up_all, conv_w, conv_b, w_down_all, layer, g, b, seq, *, tn=FFN_COLS, tk=LN_K):
    d_ff = w_down_all.shape[1]
    f_pad = -(-d_ff // tk) * tk
    assert f_pad % tn == 0
    split = lambda a: jnp.concatenate([_pad_cols(a[:, :d_ff], f_pad), _pad_cols(a[:, d_ff:], f_pad)], axis=1)
    w_up_p = _stage_ffn_weight(w_up_all, layer, d_ff, f_pad, axis=1)
    conv_w_p = split(conv_w)
    conv_b_p = split(conv_b.reshape(1, -1))
    w_down_p = _stage_ffn_weight(w_down_all, layer, d_ff, f_pad, axis=0)
    gated = _ffn_up(xb, w_up_p, conv_w_p, conv_b_p, seq, d_ff, tm=FFN_ROWS, tn=tn, chunks=FFN_ROW_CHUNKS)
    return _matmul_ln([gated], w_down_p, x, g, b, tm=LN_ROWS, tk=tk, name="ffn_down_ln", k_valid=d_ff)


def kernel(x, attn_w_in, attn_w_out, cmp_w_k, cmp_w_v, cmp_pe_k, cmp_pe_v, pool_w, pool_scale, ffn_w_up, ffn_conv_w,
           ffn_conv_b, ffn_w_down, ln_mix_g, ln_mix_b, ln_ffn_g, ln_ffn_b):
    bsz, seq, d = x.shape
    xf = x.reshape(bsz * seq, d)
    xb = None
    vec = lambda a, layer: a[layer].reshape(1, d)
    for layer in range(DEPTH):
        i = layer // 2
        if layer % 2 == 0:
            xf, xb = _attention_layer(xf, attn_w_in[i], attn_w_out[i], cmp_w_k[i], cmp_w_v[i], cmp_pe_k[i],
                                      cmp_pe_v[i], vec(ln_mix_g, layer), vec(ln_mix_b, layer), bsz, seq)
        else:
            xf, xb = _pool_mixer_ln(xf, pool_w[i].astype(BF16), pool_scale[i].reshape(1, d), vec(ln_mix_g, layer),
                                    vec(ln_mix_b, layer), seq, tm=POOL_ROWS)
        xf, xb = _ffn_layer(xf, xb, ffn_w_up, ffn_conv_w[layer], ffn_conv_b[layer], ffn_w_down, layer,
                            vec(ln_ffn_g, layer), vec(ln_ffn_b, layer), seq)
    return xf.reshape(bsz, seq, d)
```

```python
import functools

import jax
import jax.numpy as jnp
import numpy as np
from jax import lax
from jax.experimental import pallas as pl
from jax.experimental.pallas import tpu as pltpu

F32 = jnp.float32
BF16 = jnp.bfloat16

HEAD_DIM = 128
NSA_HEADS = 16
NSA_KV_HEADS = 4
NSA_GROUP = 4
SB_HEADS = 16
CMP_BLOCK = 32
CMP_STRIDE = 16
SEL_BLOCK = 64
SEL_TOP_N = 16
WINDOW = 512
ROPE_THETA = 10000.0
POOL_WINDOWS = (2, 4, 8, 16)
CONV_WIDTH = 3
LN_EPS = 1e-5
NEG_INF = -1e30
FORCE_SCORE = 1e9
DEPTH = 2
DEEPNORM_ALPHA = (2 * DEPTH) ** 0.25
SCALE = HEAD_DIM ** -0.5
LOG2E = 1.4426950408889634

COL_Q_A, COL_KC, COL_VC, COL_KS, COL_VS, COL_KW, COL_VW, COL_Q_B, COL_K_B, COL_V_B = 0, 16, 20, 24, 28, 32, 36, 40, 56, 72
Q_NSA_DIM = NSA_HEADS * HEAD_DIM
KV_NSA_DIM = NSA_KV_HEADS * HEAD_DIM
GATE_DIM = 3 * NSA_HEADS
GATE_OFF = Q_NSA_DIM + 6 * KV_NSA_DIM

VMEM_LIMIT_BYTES = 56 * 1024 * 1024


PROJ_ROWS, PROJ_COLS = 1024, 1024
GATE_ROWS = 512
ATTN_TILE = 256
SB_HEADS_PER_STEP = 16
NSA_KV_PER_STEP = 4
LN_ROWS, LN_K = 512, 1024
FFN_ROWS, FFN_COLS = 1024, 512
FFN_ROW_CHUNKS = (256, 256, 256, 128, 128)
POOL_ROWS = 256


def _params(sem, vmem=VMEM_LIMIT_BYTES):
    return pltpu.CompilerParams(dimension_semantics=sem, vmem_limit_bytes=vmem)


def _dot(a, b):
    return jnp.dot(a, b, preferred_element_type=F32)


def _dot_nt(a, b):
    return lax.dot_general(a, b, (((1,), (1,)), ((), ())), preferred_element_type=F32)


def _layer_norm_rows(y, g, b):
    mu = jnp.mean(y, axis=-1, keepdims=True)
    d = y - mu
    var = jnp.mean(d * d, axis=-1, keepdims=True)
    return d * lax.rsqrt(var + LN_EPS) * g + b


def _transpose_bf16(x):
    return x.astype(F32).T.astype(BF16)


def _inproj_kernel(a_ref, b_ref, cos_ref, sin_ref, o_ref, *, head_base, heads_per_tile, chunk):
    j = pl.program_id(1)

    def head_scales(head):
        is_rope = (head < COL_KC) | ((head >= COL_KS) & (head < COL_VS)) | ((head >= COL_KW) & (head < COL_VW))
        is_query = (head < COL_KC) | ((head >= COL_Q_B) & (head < COL_K_B))
        q_scale = jnp.where(is_query, SCALE * LOG2E, 1.0).astype(F32)
        return jnp.where(is_rope, q_scale, 0.0), jnp.where(is_rope, 0.0, q_scale)

    scales = [head_scales(head_base + j * heads_per_tile + hh) for hh in range(heads_per_tile)]
    for c in range(a_ref.shape[0] // chunk):
        rows = slice(c * chunk, (c + 1) * chunk)
        acc = _dot(a_ref[rows, :], b_ref[...])
        cos_c, sin_c = cos_ref[rows, :], sin_ref[rows, :]
        for hh, (rope_sel, pass_sel) in enumerate(scales):
            blk = acc[:, hh * HEAD_DIM:(hh + 1) * HEAD_DIM]
            rot = pltpu.roll(blk, HEAD_DIM // 2, axis=1)
            out = blk * (cos_c * rope_sel + pass_sel) + rot * (sin_c * rope_sel)
            o_ref[rows, hh * HEAD_DIM:(hh + 1) * HEAD_DIM] = out.astype(o_ref.dtype)


def _inproj(xb, w_heads, cos_t, sin_t, seq, *, head_base, tm, tn, chunk=256):
    m, kdim = xb.shape
    n = w_heads.shape[1]
    assert tn % HEAD_DIM == 0 and n % tn == 0 and seq % tm == 0 and tm % chunk == 0
    return pl.pallas_call(
        functools.partial(_inproj_kernel, head_base=head_base, heads_per_tile=tn // HEAD_DIM, chunk=chunk),
        grid=(m // tm, n // tn),
        in_specs=[
            pl.BlockSpec((tm, kdim), lambda i, j: (i, 0)),
            pl.BlockSpec((kdim, tn), lambda i, j: (0, j)),
            pl.BlockSpec((tm, HEAD_DIM), lambda i, j: (i % (seq // tm), 0)),
            pl.BlockSpec((tm, HEAD_DIM), lambda i, j: (i % (seq // tm), 0)),
        ],
        out_specs=pl.BlockSpec((tm, tn), lambda i, j: (i, j)),
        out_shape=jax.ShapeDtypeStruct((m, n), BF16),
        compiler_params=_params(("parallel", "arbitrary")),
        name="inproj_rope",
    )(xb, w_heads, cos_t, sin_t)


def _gate_kernel(x_ref, b_ref, o_ref, xb_ref):
    xb = x_ref[...].astype(BF16)
    xb_ref[...] = xb
    o_ref[...] = jax.nn.sigmoid(_dot(xb, b_ref[...]))


def _gate_proj(x, w_gate, *, tm):
    m, kdim = x.shape
    n = w_gate.shape[1]
    return pl.pallas_call(
        _gate_kernel,
        grid=(m // tm,),
        in_specs=[pl.BlockSpec((tm, kdim), lambda i: (i, 0)), pl.BlockSpec((kdim, n), lambda i: (0, 0))],
        out_specs=[pl.BlockSpec((tm, n), lambda i: (i, 0)), pl.BlockSpec((tm, kdim), lambda i: (i, 0))],
        out_shape=[jax.ShapeDtypeStruct((m, n), F32), jax.ShapeDtypeStruct((m, kdim), BF16)],
        compiler_params=_params(("parallel",)),
        name="gate_proj",
    )(x, w_gate)


def _compress_kernel(k2_ref, v2_ref, wk_ref, wv_ref, pek_ref, pev_ref, cos_ref, sin_ref, kc_ref, vct_ref, *, n_cmp):
    half = wk_ref.shape[0] // 2
    n_rows = k2_ref.shape[0]
    row = lax.broadcasted_iota(jnp.int32, (n_rows, HEAD_DIM), 0)

    def compress(x2_ref, w_ref, pe_ref):
        x2 = x2_ref[...]
        lo = _dot(x2, w_ref[:half, :])
        hi = _dot(x2, w_ref[half:, :])
        pe_term = _dot(pe_ref[...], w_ref[...])[0:1, :]
        return lo + pltpu.roll(hi, n_rows - 1, axis=0) + pe_term

    kc = compress(k2_ref, wk_ref, pek_ref)
    vc = compress(v2_ref, wv_ref, pev_ref)
    kc = kc * cos_ref[...] + pltpu.roll(kc, HEAD_DIM // 2, axis=1) * sin_ref[...]
    kc_ref[...] = jnp.where(row < n_cmp, kc, 0.0).astype(kc_ref.dtype)
    vct_ref[...] = jnp.where(row < n_cmp, vc, 0.0).T.astype(vct_ref.dtype)


def _compress(k2, v2, wk, wv, pek, pev, cos_c, sin_c, n_cmp):
    bsz, nkv, n_rows, wide = k2.shape
    assert n_rows == HEAD_DIM
    blk4 = pl.BlockSpec((None, None, n_rows, wide), lambda b, k: (b, k, 0, 0))
    full = lambda a: pl.BlockSpec(a.shape, lambda b, k: (0,) * a.ndim)
    out_blk = pl.BlockSpec((None, None, n_rows, HEAD_DIM), lambda b, k: (b, k, 0, 0))
    out_sds = jax.ShapeDtypeStruct((bsz, nkv, n_rows, HEAD_DIM), BF16)
    return pl.pallas_call(
        functools.partial(_compress_kernel, n_cmp=n_cmp),
        grid=(bsz, nkv),
        in_specs=[blk4, blk4, full(wk), full(wv), full(pek), full(pev), full(cos_c), full(sin_c)],
        out_specs=[out_blk, out_blk],
        out_shape=[out_sds, out_sds],
        compiler_params=_params(("parallel", "parallel")),
        name="nsa_compress",
    )(k2, v2, wk, wv, pek, pev, cos_c, sin_c)


def _nsa_kernel(q_ref, kc_ref, vct_ref, ks_ref, vs_ref, kw_ref, vw_ref, g_ref, ovl_ref, eind_ref, o_ref,
                vst_sc, vwt_sc, m_sc, l_sc, acc_sc, *, tq, tk, n_cmp, n_blk, kv_per_step):
    qi = pl.program_id(2)
    t0 = qi * tq
    cols = NSA_GROUP * tq
    qw = NSA_GROUP * HEAD_DIM
    kv_heads = range(kv_per_step)
    head_cols = lambda kv: slice(kv * HEAD_DIM, (kv + 1) * HEAD_DIM)

    @pl.when(qi == 0)
    def _():
        for kv in kv_heads:
            for kt in range(vst_sc.shape[1]):
                vst_sc[kv, kt] = _transpose_bf16(vs_ref[kt * tk:(kt + 1) * tk, head_cols(kv)])
                vwt_sc[kv, kt] = _transpose_bf16(vw_ref[kt * tk:(kt + 1) * tk, head_cols(kv)])

    key_minus_query = (lax.broadcasted_iota(jnp.int32, (tk, cols), 0)
                       - (lax.broadcasted_iota(jnp.int32, (tk, cols), 1) & (tq - 1)))
    n_rows = kc_ref.shape[1]
    row = lax.broadcasted_iota(jnp.int32, (n_rows, cols), 0)
    t_abs = t0 + (lax.broadcasted_iota(jnp.int32, (n_rows, cols), 1) & (tq - 1))
    jrow = lax.broadcasted_iota(jnp.int32, (n_blk, tq), 0)
    t_lane = t0 + lax.broadcasted_iota(jnp.int32, (n_blk, tq), 1)
    cur = t_lane >> 6
    forced = (jrow == 0) | (jrow == cur) | (jrow == cur - 1)

    def compressed_and_ranking(kv, q4):
        s = _dot_nt(kc_ref[kv], q4)
        valid = (row * CMP_STRIDE + (CMP_BLOCK - 1) <= t_abs) & (row < n_cmp)
        s = jnp.where(valid, s, NEG_INF)
        e = jnp.exp2(s - jnp.max(s, axis=0, keepdims=True))
        p = e * (1.0 / jnp.sum(e, axis=0, keepdims=True))
        p = jnp.where(t_abs >= CMP_BLOCK - 1, p, 0.0)
        o_cmp = _dot(vct_ref[kv], p.astype(BF16))

        psum = p[:, 0:tq] + p[:, tq:2 * tq] + p[:, 2 * tq:3 * tq] + p[:, 3 * tq:4 * tq]
        imp = jnp.dot(ovl_ref[...], psum, precision=lax.Precision.HIGHEST, preferred_element_type=F32)[0:n_blk, :]
        imp = jnp.where(forced, FORCE_SCORE, jnp.where(jrow * SEL_BLOCK <= t_lane, imp, NEG_INF))
        rank = jnp.zeros((n_blk, tq), F32)
        for jp in range(n_blk):
            other = imp[jp:jp + 1, :]
            ahead = (other > imp) | ((other == imp) & (jrow > jp))
            rank = rank + jnp.where(ahead, 1.0, 0.0)
        bias_t = jnp.where(rank < SEL_TOP_N, 0.0, NEG_INF)
        bias_t = jnp.concatenate([bias_t, jnp.zeros((HEAD_DIM - n_blk, tq), F32)], axis=0)
        bias = bias_t.T.astype(BF16)
        q_aug = jnp.concatenate([q4, jnp.concatenate([bias] * NSA_GROUP, axis=0)], axis=1)
        return o_cmp, q_aug

    def window(kv, q4):
        def win_tile(back):
            kt = qi - back
            kt_c = jnp.maximum(kt, 0)
            sc = _dot_nt(kw_ref[pl.ds(pl.multiple_of(kt_c * tk, tk), tk), head_cols(kv)], q4)
            hi, lo = back * tk, back * tk - WINDOW
            if lo >= -(tq - 1):
                lo_dyn = jnp.where(kt >= 0, lo, tk) if back else lo
                sc = jnp.where(key_minus_query > lo_dyn, sc, NEG_INF)
            elif back:
                sc = jnp.where(kt >= 0, sc, NEG_INF)
            if hi < tk - 1:
                sc = jnp.where(key_minus_query <= hi, sc, NEG_INF)
            return sc, vwt_sc[kv, kt_c]

        w_tiles = [win_tile(back) for back in range(WINDOW // tk, -1, -1)]
        m_w = functools.reduce(jnp.maximum, [jnp.max(sc, axis=0, keepdims=True) for sc, _ in w_tiles])
        l_w = jnp.zeros_like(m_w)
        o_win = jnp.zeros((HEAD_DIM, cols), F32)
        for sc, vt in w_tiles:
            pr = jnp.exp2(sc - m_w)
            l_w = l_w + jnp.sum(pr, axis=0, keepdims=True)
            o_win = o_win + _dot(vt, pr.astype(BF16))
        return o_win * (1.0 / l_w)

    o_cmp, q_aug, o_win = [], [], []
    for kv in kv_heads:
        q = q_ref[:, kv * qw:(kv + 1) * qw]
        q4 = jnp.concatenate([q[:, g * HEAD_DIM:(g + 1) * HEAD_DIM] for g in range(NSA_GROUP)], axis=0)
        oc, qa = compressed_and_ranking(kv, q4)
        o_cmp.append(oc)
        q_aug.append(qa)
        o_win.append(window(kv, q4))

    def slc_scores(kv, kt, n_tiles=1):
        s0 = pl.multiple_of(kt * tk, tk)
        k_aug = jnp.concatenate([ks_ref[pl.ds(s0, n_tiles * tk), head_cols(kv)], eind_ref[pl.ds(s0, n_tiles * tk), :]],
                                axis=1)
        return _dot_nt(k_aug, q_aug[kv])

    def online_step(kv, sc, vt):
        m_old = m_sc[kv]
        m_new = jnp.maximum(m_old, jnp.max(sc, axis=0, keepdims=True))
        pr = jnp.exp2(sc - m_new)
        corr = jnp.exp2(m_old - m_new)
        l_sc[kv] = corr * l_sc[kv] + jnp.sum(pr, axis=0, keepdims=True)
        acc_sc[kv] = corr * acc_sc[kv] + _dot(vt, pr.astype(BF16))
        m_sc[kv] = m_new

    m_sc[...] = jnp.full(m_sc.shape, NEG_INF, F32)
    l_sc[...] = jnp.zeros(l_sc.shape, F32)
    acc_sc[...] = jnp.zeros(acc_sc.shape, F32)

    def slc_pair(pt, carry):
        kt = 2 * pt
        for kv in kv_heads:
            online_step(kv, slc_scores(kv, kt, 2), jnp.concatenate([vst_sc[kv, kt], vst_sc[kv, kt + 1]], axis=1))
        return carry

    lax.fori_loop(0, qi // 2, slc_pair, 0)

    @pl.when(qi % 2 == 1)
    def _():
        for kv in kv_heads:
            online_step(kv, slc_scores(kv, qi - 1), vst_sc[kv, qi - 1])

    for kv in kv_heads:
        online_step(kv, jnp.where(key_minus_query <= 0, slc_scores(kv, qi), NEG_INF), vst_sc[kv, qi])

    for kv in kv_heads:
        o_slc = acc_sc[kv] * (1.0 / l_sc[kv])
        gates_t = g_ref[:, head_cols(kv)].T
        for g in range(NSA_GROUP):
            sl = slice(g * tq, (g + 1) * tq)
            o = (gates_t[g:g + 1, :] * o_cmp[kv][:, sl]
                 + gates_t[NSA_GROUP + g:NSA_GROUP + g + 1, :] * o_slc[:, sl]
                 + gates_t[2 * NSA_GROUP + g:2 * NSA_GROUP + g + 1, :] * o_win[kv][:, sl])
            o_ref[:, kv * qw + g * HEAD_DIM:kv * qw + (g + 1) * HEAD_DIM] = o.T.astype(o_ref.dtype)


def _nsa(h, kc, vct, gates, ovl, eind, bsz, seq, *, tq, tk, kv_per_step):
    assert tq == tk and seq % tq == 0 and WINDOW % tk == 0
    assert NSA_KV_HEADS % kv_per_step == 0 and all(c % kv_per_step == 0 for c in (COL_KS, COL_VS, COL_KW, COL_VW))
    m = h.shape[0]
    nq = seq // tq
    n_cmp = (seq - CMP_BLOCK) // CMP_STRIDE + 1
    n_blk = seq // SEL_BLOCK
    qw = kv_per_step * NSA_GROUP * HEAD_DIM
    kvw = kv_per_step * HEAD_DIM
    cols = NSA_GROUP * tq
    kv_spec = lambda col0: pl.BlockSpec((seq, kvw), lambda b, k, i: (b, col0 // kv_per_step + k))
    cmp_spec = pl.BlockSpec((None, kv_per_step, kc.shape[2], HEAD_DIM), lambda b, k, i: (b, k, 0, 0))
    tiles = seq // tk
    return pl.pallas_call(
        functools.partial(_nsa_kernel, tq=tq, tk=tk, n_cmp=n_cmp, n_blk=n_blk, kv_per_step=kv_per_step),
        grid=(bsz, NSA_KV_HEADS // kv_per_step, nq),
        in_specs=[
            pl.BlockSpec((tq, qw), lambda b, k, i: (b * nq + i, k)),
            cmp_spec, cmp_spec,
            kv_spec(COL_KS), kv_spec(COL_VS), kv_spec(COL_KW), kv_spec(COL_VW),
            pl.BlockSpec((tq, kvw), lambda b, k, i: (b * nq + i, k)),
            pl.BlockSpec(ovl.shape, lambda b, k, i: (0, 0)),
            pl.BlockSpec(eind.shape, lambda b, k, i: (0, 0)),
        ],
        out_specs=pl.BlockSpec((tq, qw), lambda b, k, i: (b * nq + i, k)),
        out_shape=jax.ShapeDtypeStruct((m, Q_NSA_DIM), BF16),
        scratch_shapes=[pltpu.VMEM((kv_per_step, tiles, HEAD_DIM, tk), BF16),
                        pltpu.VMEM((kv_per_step, tiles, HEAD_DIM, tk), BF16),
                        pltpu.VMEM((kv_per_step, 1, cols), F32), pltpu.VMEM((kv_per_step, 1, cols), F32),
                        pltpu.VMEM((kv_per_step, HEAD_DIM, cols), F32)],
        compiler_params=_params(("arbitrary", "arbitrary", "arbitrary")),
        name="nsa_attention",
    )(h, kc, vct, h, h, h, h, gates, ovl, eind)


def _sb_kernel(q_ref, k_ref, v_ref, u_ref, o_ref, carry_sc, acc_sc, *, tq, tk, heads):
    qi = pl.program_id(2)
    query_minus_key = lax.broadcasted_iota(jnp.int32, (tq, tk), 0) - lax.broadcasted_iota(jnp.int32, (tq, tk), 1)
    carry_sc[...] = jnp.zeros(carry_sc.shape, F32)
    acc_sc[...] = jnp.zeros(acc_sc.shape, F32)
    u = u_ref[...]

    def tile(kt, diagonal):
        s0 = pl.multiple_of(kt * tk, tk)
        for hh in range(heads):
            cs = slice(hh * HEAD_DIM, (hh + 1) * HEAD_DIM)
            z = _dot_nt(q_ref[:, cs], k_ref[pl.ds(s0, tk), cs])
            sp = jnp.maximum(z, 0.0) + jnp.log2(1.0 + jnp.exp2(-jnp.abs(z)))
            if diagonal:
                mask = query_minus_key > 0
                sp = jnp.where(mask, sp, 0.0)
            sp_b = sp.astype(BF16)
            sums = _dot(sp_b, u)
            carry = carry_sc[hh]
            later = sums + jnp.concatenate([carry] * (tk // HEAD_DIM), axis=1)
            a = jnp.exp2(z - sp - later)
            if diagonal:
                a = jnp.where(mask, a, 0.0)
            acc_sc[hh] += _dot(a.astype(BF16), v_ref[pl.ds(s0, tk), cs])
            row_total = sums[:, 0:1] + sp_b[:, 0:1].astype(F32)
            carry_sc[hh] = carry + jnp.broadcast_to(row_total, carry.shape)

    tile(qi, True)

    def body(it, c):
        tile(qi - 1 - it, False)
        return c

    lax.fori_loop(0, qi, body, 0)
    for hh in range(heads):
        o_ref[:, hh * HEAD_DIM:(hh + 1) * HEAD_DIM] = acc_sc[hh].astype(o_ref.dtype)


def _stick_breaking(h, u_tri, bsz, seq, *, tq, tk, heads):
    assert seq % tq == 0 and tq == tk and u_tri.shape == (tk, tk)
    assert SB_HEADS % heads == 0 and h.shape[1] == 3 * SB_HEADS * HEAD_DIM
    groups = SB_HEADS // heads
    m = h.shape[0]
    nq = seq // tq
    wide = heads * HEAD_DIM
    return pl.pallas_call(
        functools.partial(_sb_kernel, tq=tq, tk=tk, heads=heads),
        grid=(bsz, groups, nq),
        in_specs=[
            pl.BlockSpec((tq, wide), lambda b, hg, i: (b * nq + i, hg)),
            pl.BlockSpec((seq, wide), lambda b, hg, i: (b, groups + hg)),
            pl.BlockSpec((seq, wide), lambda b, hg, i: (b, 2 * groups + hg)),
            pl.BlockSpec(u_tri.shape, lambda b, hg, i: (0, 0)),
        ],
        out_specs=pl.BlockSpec((tq, wide), lambda b, hg, i: (b * nq + i, hg)),
        out_shape=jax.ShapeDtypeStruct((m, SB_HEADS * HEAD_DIM), BF16),
        scratch_shapes=[pltpu.VMEM((heads, tq, HEAD_DIM), F32), pltpu.VMEM((heads, tq, HEAD_DIM), F32)],
        compiler_params=_params(("parallel", "parallel", "arbitrary")),
        name="stick_breaking",
    )(h, h, h, u_tri)


def _ln_epilogue(y, x_ref, g_ref, b_ref, of_ref, ob_ref, rows=slice(None)):
    out = _layer_norm_rows(DEEPNORM_ALPHA * x_ref[rows, :] + y, g_ref[...], b_ref[...])
    of_ref[rows, :] = out
    ob_ref[rows, :] = out.astype(ob_ref.dtype)


def _mm_ln_kernel(*refs, k_splits, chunk, last_k):
    n_a = len(k_splits) - 1
    a_refs = refs[:n_a]
    w_ref, x_hbm, g_ref, b_ref, of_hbm, ob_hbm, acc_sc, xo_sc, ob_sc, sems = refs[n_a:]
    i, k = pl.program_id(0), pl.program_id(1)
    n_tiles, nk = pl.num_programs(0), k_splits[-1]
    tm = acc_sc.shape[0]

    def tile_rows(tile):
        return pl.ds(pl.multiple_of(tile * tm, tm), tm)

    def residual_copy():
        return pltpu.make_async_copy(x_hbm.at[tile_rows(i)], xo_sc, sems.at[0])

    def output_copies(tile):
        return (pltpu.make_async_copy(xo_sc, of_hbm.at[tile_rows(tile)], sems.at[1]),
                pltpu.make_async_copy(ob_sc, ob_hbm.at[tile_rows(tile)], sems.at[2]))

    @pl.when(k == 1)
    def _():
        @pl.when(i > 0)
        def _():
            for cp in output_copies(i - 1):
                cp.wait()

        residual_copy().start()

    for s, a_ref in enumerate(a_refs):
        lo, hi = k_splits[s], k_splits[s + 1]
        first = lo == 0
        last = hi == nk

        if first:
            @pl.when(k == 0)
            def _(a_ref=a_ref):
                acc_sc[...] = _dot(a_ref[...], w_ref[...])

        lo_acc, hi_acc = lo + (1 if first else 0), hi - (1 if last else 0)
        if hi_acc > lo_acc:
            @pl.when((k >= lo_acc) & (k < hi_acc))
            def _(a_ref=a_ref):
                acc_sc[...] += _dot(a_ref[...], w_ref[...])

        if last:
            @pl.when(k == nk - 1)
            def _(a_ref=a_ref):
                residual_copy().wait()
                for c in range(tm // chunk):
                    rows = slice(c * chunk, (c + 1) * chunk)
                    y = acc_sc[rows, :] + _dot(a_ref[rows, :last_k], w_ref[:last_k, :])
                    _ln_epilogue(y, xo_sc, g_ref, b_ref, xo_sc, ob_sc, rows)

                for cp in output_copies(i):
                    cp.start()

                @pl.when(i == n_tiles - 1)
                def _():
                    for cp in output_copies(i):
                        cp.wait()


def _matmul_ln(a_list, w, x, g, b, *, tm, tk, name, chunk=128, k_valid=None):
    m, n = x.shape
    kdim = w.shape[0]
    nk = kdim // tk
    k_splits = [0]
    for a in a_list:
        assert a.shape[1] % tk == 0
        k_splits.append(k_splits[-1] + a.shape[1] // tk)
    assert k_splits[-1] == nk and nk >= 3 and tm % chunk == 0 and m % tm == 0
    last_k = tk if k_valid is None else k_valid - (nk - 1) * tk
    assert 0 < last_k <= tk and last_k % HEAD_DIM == 0
    vec = pl.BlockSpec((1, n), lambda i, k: (0, 0))
    hbm = pl.BlockSpec(memory_space=pl.ANY)

    def a_spec(lo, hi):
        return pl.BlockSpec((tm, tk), lambda i, k: (i, jnp.clip(k - lo, 0, hi - lo - 1)))

    return pl.pallas_call(
        functools.partial(_mm_ln_kernel, k_splits=tuple(k_splits), chunk=chunk, last_k=last_k),
        grid=(m // tm, nk),
        in_specs=[a_spec(k_splits[s], k_splits[s + 1]) for s in range(len(a_list))]
        + [pl.BlockSpec((tk, n), lambda i, k: (k, 0)), hbm, vec, vec],
        out_specs=[hbm, hbm],
        out_shape=[jax.ShapeDtypeStruct((m, n), F32), jax.ShapeDtypeStruct((m, n), BF16)],
        scratch_shapes=[pltpu.VMEM((tm, n), F32), pltpu.VMEM((tm, n), F32), pltpu.VMEM((tm, n), BF16),
                        pltpu.SemaphoreType.DMA((3,))],
        compiler_params=_params(("arbitrary", "arbitrary")),
        name=name,
    )(*a_list, w, x, g, b)


def _stage_tiles_kernel(x_ref, o_ref, *, tiles_in, tiles_out):
    valid = pl.program_id(0) % tiles_out < tiles_in
    o_ref[...] = jnp.where(valid, x_ref[...], 0.0).astype(o_ref.dtype)


def _stage_ffn_weight(w_all, layer, d_ff, f_pad, axis, *, tile=256):
    assert d_ff % tile == 0 and f_pad % tile == 0
    tiles_in, tiles_out = d_ff // tile, f_pad // tile
    shape = list(w_all.shape[1:])
    sections = shape[axis] // d_ff
    src = lambda t: (t // tiles_out) * tiles_in + jnp.minimum(t % tiles_out, tiles_in - 1)
    if axis == 0:
        blk, out_shape = (tile, shape[1]), (sections * f_pad, shape[1])
        in_spec = pl.BlockSpec((None,) + blk, lambda t: (layer, src(t), 0))
        out_spec = pl.BlockSpec(blk, lambda t: (t, 0))
    else:
        blk, out_shape = (shape[0], tile), (shape[0], sections * f_pad)
        in_spec = pl.BlockSpec((None,) + blk, lambda t: (layer, 0, src(t)))
        out_spec = pl.BlockSpec(blk, lambda t: (0, t))
    return pl.pallas_call(
        functools.partial(_stage_tiles_kernel, tiles_in=tiles_in, tiles_out=tiles_out),
        grid=(sections * tiles_out,),
        in_specs=[in_spec],
        out_specs=out_spec,
        out_shape=jax.ShapeDtypeStruct(out_shape, BF16),
        compiler_params=_params(("parallel",)),
        name="stage_ffn_weight",
    )(w_all)


def _ffn_up_kernel(a_ref, halo_ref, wg_ref, wv_ref, cwg_ref, cwv_ref, cbg_ref, cbv_ref, o_ref, head_sc, hg_sc, hv_sc, *,
                   tm, chunks, halo, tiles_per_seq, last_cols):
    i, j = pl.program_id(0), pl.program_id(1)
    tn = o_ref.shape[1]

    @pl.when(j == 0)
    def _():
        seq_start = i % tiles_per_seq == 0
        head_sc[0:halo, :] = jnp.where(seq_start, jnp.zeros_like(halo_ref[...]), halo_ref[...])
        head_sc[halo:, :] = a_ref[...]

    def run(cols):
        def conv(h_sc, size, cw_ref, cb_ref):
            cw = cw_ref[:, :cols]
            out = cb_ref[:, :cols]
            for kk in range(CONV_WIDTH):
                out = out + h_sc[pl.ds(halo - (CONV_WIDTH - 1) + kk, size), :cols] * cw[kk:kk + 1, :]
            return out

        start, prev = 0, 0
        for c, size in enumerate(chunks):
            p = c % 2
            rows = slice(start, start + size)
            if c == 0:
                lhs = head_sc[0:halo + size, :]
                hg_sc[p, 0:halo + size, :cols] = _dot(lhs, wg_ref[:, :cols])
                hv_sc[p, 0:halo + size, :cols] = _dot(lhs, wv_ref[:, :cols])
            else:
                lhs = head_sc[halo + start:halo + start + size, :]
                hg_sc[p, 0:halo, :cols] = hg_sc[1 - p, prev:prev + halo, :cols]
                hv_sc[p, 0:halo, :cols] = hv_sc[1 - p, prev:prev + halo, :cols]
                hg_sc[p, halo:halo + size, :cols] = _dot(lhs, wg_ref[:, :cols])
                hv_sc[p, halo:halo + size, :cols] = _dot(lhs, wv_ref[:, :cols])
            gate = conv(hg_sc.at[p], size, cwg_ref, cbg_ref)
            val = conv(hv_sc.at[p], size, cwv_ref, cbv_ref)
            o_ref[rows, :cols] = (gate * jax.nn.sigmoid(gate) * val).astype(o_ref.dtype)
            start, prev = start + size, size
        if cols < tn:
            o_ref[:, cols:] = jnp.zeros((tm, tn - cols), o_ref.dtype)

    if last_cols == tn:
        run(tn)
    else:
        last = pl.num_programs(1) - 1
        pl.when(j < last)(lambda: run(tn))
        pl.when(j == last)(lambda: run(last_cols))


def _ffn_up(xb, w_up, conv_w, conv_b, seq, d_ff, *, tm, tn, chunks, halo=16):
    m, kdim = xb.shape
    f_pad = w_up.shape[1] // 2
    nj = f_pad // tn
    assert seq % tm == 0 and sum(chunks) == tm and all(c % halo == 0 for c in chunks)
    chunk = max(chunks)
    hb = tm // halo
    last_cols = d_ff - (nj - 1) * tn
    assert 0 < last_cols <= tn and last_cols % HEAD_DIM == 0
    return pl.pallas_call(
        functools.partial(_ffn_up_kernel, tm=tm, chunks=tuple(chunks), halo=halo, tiles_per_seq=seq // tm,
                          last_cols=last_cols),
        grid=(m // tm, nj),
        in_specs=[
            pl.BlockSpec((tm, kdim), lambda i, j: (i, 0)),
            pl.BlockSpec((halo, kdim), lambda i, j: (jnp.maximum(i * hb - 1, 0), 0)),
            pl.BlockSpec((kdim, tn), lambda i, j: (0, j)),
            pl.BlockSpec((kdim, tn), lambda i, j: (0, nj + j)),
            pl.BlockSpec((CONV_WIDTH, tn), lambda i, j: (0, j)),
            pl.BlockSpec((CONV_WIDTH, tn), lambda i, j: (0, nj + j)),
            pl.BlockSpec((1, tn), lambda i, j: (0, j)),
            pl.BlockSpec((1, tn), lambda i, j: (0, nj + j)),
        ],
        out_specs=pl.BlockSpec((tm, tn), lambda i, j: (i, j)),
        out_shape=jax.ShapeDtypeStruct((m, f_pad), BF16),
        scratch_shapes=[pltpu.VMEM((halo + tm, kdim), BF16), pltpu.VMEM((2, halo + chunk, tn), F32),
                        pltpu.VMEM((2, halo + chunk, tn), F32)],
        compiler_params=_params(("parallel", "arbitrary")),
        name="ffn_up_conv_gate",
    )(xb, xb, w_up, w_up, conv_w, conv_w, conv_b, conv_b)


def _pool_kernel(x_ref, halo_ref, w_ref, ps_ref, g_ref, b_ref, of_ref, ob_ref, d_sc, *, tm, halo, tiles_per_seq, group_dim):
    i = pl.program_id(0)
    keep = jnp.where(i % tiles_per_seq == 0, 0.0, 1.0)
    t_in_seq = (i % tiles_per_seq) * tm + lax.broadcasted_iota(jnp.int32, (tm, group_dim), 0)
    for gi, win in enumerate(POOL_WINDOWS):
        cs = slice(gi * group_dim, (gi + 1) * group_dim)
        ext = jnp.concatenate([halo_ref[:, cs] * keep, x_ref[:, cs]], axis=0)
        acc, span = ext, 1
        while span < win:
            acc = acc + pltpu.roll(acc, span, axis=0)
            span *= 2
        count = jnp.minimum(t_in_seq + 1, win).astype(F32)
        d = acc[halo:, :] / count - ext[halo:, :]
        d_sc[:, cs] = _dot(d.astype(BF16), w_ref[gi])
    y = d_sc[...] * ps_ref[...]
    _ln_epilogue(y, x_ref, g_ref, b_ref, of_ref, ob_ref)


def _pool_mixer_ln(x, w_pool, pool_scale, g, b, seq, *, tm, halo=16):
    m, n = x.shape
    ngrp, group_dim = w_pool.shape[0], w_pool.shape[1]
    assert seq % tm == 0 and tm % halo == 0 and max(POOL_WINDOWS) <= halo
    hb = tm // halo
    row = pl.BlockSpec((tm, n), lambda i: (i, 0))
    vec = pl.BlockSpec((1, n), lambda i: (0, 0))
    return pl.pallas_call(
        functools.partial(_pool_kernel, tm=tm, halo=halo, tiles_per_seq=seq // tm, group_dim=group_dim),
        grid=(m // tm,),
        in_specs=[row, pl.BlockSpec((halo, n), lambda i: (jnp.maximum(i * hb - 1, 0), 0)),
                  pl.BlockSpec(w_pool.shape, lambda i: (0, 0, 0), pipeline_mode=pl.Buffered(1)), vec, vec, vec],
        out_specs=[row, row],
        out_shape=[jax.ShapeDtypeStruct((m, n), F32), jax.ShapeDtypeStruct((m, n), BF16)],
        scratch_shapes=[pltpu.VMEM((tm, n), F32)],
        compiler_params=_params(("parallel",)),
        name="pool_mixer_ln",
    )(x, x, w_pool, pool_scale, g, b)


def _rope_tables(pos):
    inv_freq = 1.0 / (ROPE_THETA ** (jnp.arange(0, HEAD_DIM, 2, dtype=F32) / HEAD_DIM))
    ang = pos.astype(F32)[:, None] * inv_freq[None, :]
    cos, sin = jnp.cos(ang), jnp.sin(ang)
    return jnp.concatenate([cos, cos], axis=-1), jnp.concatenate([-sin, sin], axis=-1)


def _overlap_t(seq):
    n_cmp = (seq - CMP_BLOCK) // CMP_STRIDE + 1
    n_blk = seq // SEL_BLOCK
    c_start = np.arange(n_cmp)[:, None] * CMP_STRIDE
    b_start = np.arange(n_blk)[None, :] * SEL_BLOCK
    ov = np.clip(np.minimum(c_start + CMP_BLOCK, b_start + SEL_BLOCK) - np.maximum(c_start, b_start), 0, None) / CMP_BLOCK
    out = np.zeros((HEAD_DIM, HEAD_DIM), np.float32)
    out[:n_blk, :n_cmp] = ov.T
    return jnp.asarray(out)


def _block_indicator(seq):
    out = np.zeros((seq, HEAD_DIM), np.float32)
    out[np.arange(seq), np.arange(seq) // SEL_BLOCK] = 1.0
    return jnp.asarray(out, BF16)


def _later_sum_matrix(n):
    return jnp.asarray(np.tril(np.ones((n, n), np.float32), -1), BF16)


def _pad_cols(w, n):
    return jnp.pad(w, ((0, 0), (0, n - w.shape[1])))


def _attention_layer(x, w_in, w_out, w_cmp_k, w_cmp_v, pe_k, pe_v, g, b, bsz, seq):
    n_cmp = (seq - CMP_BLOCK) // CMP_STRIDE + 1
    n_rows = seq // CMP_STRIDE
    w_nsa = w_in[:, :GATE_OFF].astype(BF16)
    w_sb = w_in[:, GATE_OFF + GATE_DIM:].astype(BF16)
    gate_cols = np.array([[GATE_OFF + c * NSA_HEADS + k * NSA_GROUP + gg for c in range(3) for gg in range(NSA_GROUP)]
                          for k in range(NSA_KV_HEADS)])
    w_gate = jnp.concatenate([_pad_cols(w_in[:, gate_cols[k]], HEAD_DIM) for k in range(NSA_KV_HEADS)], axis=1).astype(BF16)

    cos_t, sin_t = _rope_tables(jnp.arange(seq))
    gates, xb = _gate_proj(x, w_gate, tm=GATE_ROWS)
    h = _inproj(xb, w_nsa, cos_t, sin_t, seq, head_base=COL_Q_A, tm=PROJ_ROWS, tn=PROJ_COLS)
    h_sb = _inproj(xb, w_sb, cos_t, sin_t, seq, head_base=COL_Q_B, tm=PROJ_ROWS, tn=PROJ_COLS)

    def blocks16(col0):
        part = h[:, col0 * HEAD_DIM:(col0 + NSA_KV_HEADS) * HEAD_DIM]
        part = part.reshape(bsz, n_rows, CMP_STRIDE, NSA_KV_HEADS, HEAD_DIM).transpose(0, 3, 1, 2, 4)
        return part.reshape(bsz, NSA_KV_HEADS, n_rows, CMP_STRIDE * HEAD_DIM)

    cmp_end = jnp.arange(n_rows) * CMP_STRIDE + CMP_BLOCK - 1
    cos_c, sin_c = _rope_tables(cmp_end)
    flat_pe = lambda pe: jnp.broadcast_to(pe.reshape(1, CMP_BLOCK * HEAD_DIM), (8, CMP_BLOCK * HEAD_DIM)).astype(BF16)
    flat_w = lambda w: w.reshape(CMP_BLOCK * HEAD_DIM, HEAD_DIM).astype(BF16)
    kc, vct = _compress(blocks16(COL_KC), blocks16(COL_VC), flat_w(w_cmp_k), flat_w(w_cmp_v),
                        flat_pe(pe_k), flat_pe(pe_v), cos_c, sin_c, n_cmp)

    o_a = _nsa(h, kc, vct, gates, _overlap_t(seq), _block_indicator(seq), bsz, seq, tq=ATTN_TILE, tk=ATTN_TILE,
               kv_per_step=NSA_KV_PER_STEP)
    o_b = _stick_breaking(h_sb, _later_sum_matrix(ATTN_TILE), bsz, seq, tq=ATTN_TILE, tk=ATTN_TILE,
                          heads=SB_HEADS_PER_STEP)
    return _matmul_ln([o_a, o_b], w_out.astype(BF16), x, g, b, tm=LN_ROWS, tk=LN_K, name="out_proj_ln")


def _ffn_layer(x, xb, w_up_all, conv_w, conv_b, w_down_all, layer, g, b, seq, *, tn=FFN_COLS, tk=LN_K):
    d_ff = w_down_all.shape[1]
    f_pad = -(-d_ff // tk) * tk
    assert f_pad % tn == 0
    split = lambda a: jnp.concatenate([_pad_cols(a[:, :d_ff], f_pad), _pad_cols(a[:, d_ff:], f_pad)], axis=1)
    w_up_p = _stage_ffn_weight(w_up_all, layer, d_ff, f_pad, axis=1)
    conv_w_p = split(conv_w)
    conv_b_p = split(conv_b.reshape(1, -1))
    w_down_p = _stage_ffn_weight(w_down_all, layer, d_ff, f_pad, axis=0)
    gated = _ffn_up(xb, w_up_p, conv_w_p, conv_b_p, seq, d_ff, tm=FFN_ROWS, tn=tn, chunks=FFN_ROW_CHUNKS)
    return _matmul_ln([gated], w_down_p, x, g, b, tm=LN_ROWS, tk=tk, name="ffn_down_ln", k_valid=d_ff)


def kernel(x, attn_w_in, attn_w_out, cmp_w_k, cmp_w_v, cmp_pe_k, cmp_pe_v, pool_w, pool_scale, ffn_w_up, ffn_conv_w,
           ffn_conv_b, ffn_w_down, ln_mix_g, ln_mix_b, ln_ffn_g, ln_ffn_b):
    bsz, seq, d = x.shape
    xf = x.reshape(bsz * seq, d)
    xb = None
    vec = lambda a, layer: a[layer].reshape(1, d)
    for layer in range(DEPTH):
        i = layer // 2
        if layer % 2 == 0:
            xf, xb = _attention_layer(xf, attn_w_in[i], attn_w_out[i], cmp_w_k[i], cmp_w_v[i], cmp_pe_k[i],
                                      cmp_pe_v[i], vec(ln_mix_g, layer), vec(ln_mix_b, layer), bsz, seq)
        else:
            xf, xb = _pool_mixer_ln(xf, pool_w[i].astype(BF16), pool_scale[i].reshape(1, d), vec(ln_mix_g, layer),
                                    vec(ln_mix_b, layer), seq, tm=POOL_ROWS)
        xf, xb = _ffn_layer(xf, xb, ffn_w_up, ffn_conv_w[layer], ffn_conv_b[layer], ffn_w_down, layer,
                            vec(ln_ffn_g, layer), vec(ln_ffn_b, layer), seq)
    return xf.reshape(bsz, seq, d)
```
